```python
import jax, jax.numpy as jnp
from jax import lax
import numpy as np

D_MODEL = 1024
BATCH = 4
SEQ = 8192
DEPTH = 2
DEC_BATCH = 32
DEC_SEQ = 1
PAST_LEN = 16384
PAGE_SIZE = 128

HEAD_DIM = 64
ROPE_THETA = 10000.0
NORM_EPS = 1e-6

H_A = 8
D_A = H_A * HEAD_DIM
DECAY_LORA = 64
AAA_LORA = 64
GATE_LORA = 128
A_COLS = 3 * D_A + DECAY_LORA + AAA_LORA + GATE_LORA
GN_EPS = 64e-5

H_B = 8
D_B = H_B * HEAD_DIM
H_IDX = 8
D_IDX = 64
IDX_TOPK_MAX = 256
IDX_SCALE = (H_IDX ** -0.5) * (D_IDX ** -0.5)
B_COLS = 3 * D_B + H_IDX * D_IDX + H_IDX + D_IDX
Q_BLOCK = 128

H_C = 16
D_C = H_C * HEAD_DIM
C_PATTERNS = ((128, 1), (512, 4), (2048, 16))
C_WINDOW_MAX = 2048
C_BLOCK = 128

D_FF = 2816
N_EXPERTS = 8
TOP_K = 2
D_FF_EXPERT = 3584

kernel_name = 'hybrid_rwkv7_dsa_dilated_decoder_step'


def rms_norm(x, g):
    xf = x.astype(jnp.float32)
    y = xf * lax.rsqrt(jnp.mean(xf * xf, axis=-1, keepdims=True) + NORM_EPS)
    return (y * g.astype(jnp.float32)).astype(x.dtype)


def rope(x, pos):
    half = x.shape[-1] // 2
    inv_freq = jnp.power(ROPE_THETA, -jnp.arange(half, dtype=jnp.float32) / half)
    ang = pos.astype(jnp.float32)[:, None] * inv_freq[None, :]
    cos = jnp.cos(ang)[:, None, :]
    sin = jnp.sin(ang)[:, None, :]
    xf = x.astype(jnp.float32)
    x1, x2 = xf[..., :half], xf[..., half:]
    return jnp.concatenate([x1 * cos - x2 * sin, x2 * cos + x1 * sin], axis=-1).astype(x.dtype)


def split_cols(x, sizes):
    offs = np.cumsum(sizes)[:-1].tolist()
    return jnp.split(x, offs, axis=-1)


def swiglu(h, wg, wu, wd):
    return (jax.nn.silu(h @ wg) * (h @ wu)) @ wd


def moe_swiglu(h, router_w, router_b, eg, eu, ed):
    logits = (h @ router_w).astype(jnp.float32) + router_b.astype(jnp.float32)
    top_v, top_i = lax.top_k(logits, TOP_K)
    gates = jax.nn.softmax(top_v, axis=-1)
    dense_gate = jnp.sum(jax.nn.one_hot(top_i, N_EXPERTS, dtype=jnp.float32) * gates[..., None], axis=-2)
    dense_gate = dense_gate.astype(h.dtype)
    y = jnp.zeros_like(h)
    for e in range(N_EXPERTS):
        y = y + dense_gate[..., e:e + 1] * swiglu(h, eg[e], eu[e], ed[e])
    return y


def rwkv7_mix(p, shift_prev, wkv_prev, mu, w0, w2, a0, a2, g2, k_k, k_a, r_k, ln_w, ln_b):
    f32 = jnp.float32
    Bn, T = p.shape[:2]
    prev = jnp.concatenate([shift_prev[:, None, :].astype(p.dtype), p[:, :-1]], axis=1)
    xm = p + (prev - p) * mu
    r, k, v, wl, al, gl = split_cols(xm, (D_A, D_A, D_A, DECAY_LORA, AAA_LORA, GATE_LORA))
    logw = -jax.nn.softplus(-(w0 + jnp.tanh(wl) @ w2).astype(f32)) - 0.5
    decay = jnp.exp(-jnp.exp(logw))
    a = jax.nn.sigmoid((a0 + al @ a2).astype(f32))
    g = (jax.nn.sigmoid(gl) @ g2).astype(f32)
    r, k, v = r.astype(f32), k.astype(f32), v.astype(f32)
    heads = lambda t: t.reshape(Bn, T, H_A, HEAD_DIM)
    kk = heads(k * k_k.astype(f32))
    kk = kk / jnp.maximum(jnp.sqrt(jnp.sum(kk * kk, axis=-1, keepdims=True)), 1e-12)
    k = k * (1.0 + (a - 1.0) * k_a.astype(f32))
    r, k, v, decay, a = heads(r), heads(k), heads(v), heads(decay), heads(a)
    a_vec = -kk
    b_vec = kk * a

    def step(S, inp):
        r_t, w_t, k_t, v_t, a_t, b_t = inp
        sa = jnp.einsum('bhij,bhj->bhi', S, a_t)
        S = S * w_t[:, :, None, :] + sa[..., None] * b_t[:, :, None, :] + v_t[..., None] * k_t[:, :, None, :]
        return S, jnp.einsum('bhij,bhj->bhi', S, r_t)

    xs = tuple(jnp.moveaxis(t, 1, 0) for t in (r, decay, k, v, a_vec, b_vec))
    wkv_new, y = lax.scan(step, wkv_prev.astype(f32), xs)
    y = jnp.moveaxis(y, 0, 1)
    mean = jnp.mean(y, axis=-1, keepdims=True)
    var = jnp.mean(jnp.square(y - mean), axis=-1, keepdims=True)
    y = ((y - mean) * lax.rsqrt(var + GN_EPS)).reshape(Bn, T, D_A) * ln_w.astype(f32) + ln_b.astype(f32)
    bonus = jnp.sum(r * k * r_k.astype(f32), axis=-1, keepdims=True) * v
    y = (y + bonus.reshape(Bn, T, D_A)) * g
    return y.astype(p.dtype), wkv_new, p[:, -1]


def dsa_select_attend(q, qi, wi, q_pos, kidx, key_pos, gather_kv, topk):
    f32 = jnp.float32
    s = jnp.einsum('bqhd,bld->bqhl', qi.astype(f32), kidx.astype(f32))
    score = jnp.einsum('bqhl,bqh->bql', jax.nn.relu(s), wi.astype(f32) * IDX_SCALE)
    causal = key_pos[None, :] <= q_pos[:, None]
    score = jnp.where(causal[None], score, -jnp.inf)
    vals, idx = lax.top_k(score, topk)
    valid = jnp.isfinite(vals)
    k_sel, v_sel = gather_kv(idx)
    sc = jnp.einsum('bqhd,bqkhd->bqhk', q.astype(f32), k_sel.astype(f32)) * (HEAD_DIM ** -0.5)
    sc = jnp.where(valid[:, :, None, :], sc, -jnp.inf)
    pr = jax.nn.softmax(sc, axis=-1)
    out = jnp.einsum('bqhk,bqkhd->bqhd', pr, v_sel.astype(f32))
    return out.astype(q.dtype)


def dsa_prompt(q, k, v, qi, wi, kidx, pos):
    Bn, T = q.shape[:2]
    topk = min(IDX_TOPK_MAX, T // 4)
    nb = T // Q_BLOCK
    blocks = lambda t: jnp.moveaxis(t.reshape((Bn, nb, Q_BLOCK) + t.shape[2:]), 1, 0)
    take = jax.vmap(lambda kk_, vv_, ii: (kk_[ii], vv_[ii]))
    gather_kv = lambda idx: take(k, v, idx)

    def one_block(blk):
        qb, qib, wib, pb = blk
        return dsa_select_attend(qb, qib, wib, pb, kidx, pos, gather_kv, topk)

    out = lax.map(one_block, (blocks(q), blocks(qi), blocks(wi), pos.reshape(nb, Q_BLOCK)))
    return jnp.moveaxis(out, 0, 1).reshape(q.shape)


def dsa_sample(q, k_new, v_new, qi, wi, kidx_new, pos, cache_k, cache_v, cache_kidx, page_table):
    DB, S_ = q.shape[:2]
    past = page_table.shape[1] * PAGE_SIZE
    L = past + S_
    topk = min(IDX_TOPK_MAX, L // 4)
    kidx_past = cache_kidx[page_table].reshape(DB, past, D_IDX)
    kidx_all = jnp.concatenate([kidx_past, kidx_new.astype(kidx_past.dtype)], axis=1)
    key_pos = jnp.arange(L, dtype=jnp.int32)
    take = jax.vmap(lambda t, i: t[i])

    def gather_kv(idx):
        in_past = (idx < past)[..., None, None]
        ip = jnp.minimum(idx, past - 1)
        phys = take(page_table, ip // PAGE_SIZE)
        off = ip % PAGE_SIZE
        inew = jnp.clip(idx - past, 0, S_ - 1)
        k_sel = jnp.where(in_past, cache_k[phys, off], take(k_new, inew).astype(cache_k.dtype))
        v_sel = jnp.where(in_past, cache_v[phys, off], take(v_new, inew).astype(cache_v.dtype))
        return k_sel, v_sel

    return dsa_select_attend(q, qi, wi, pos, kidx_all, key_pos, gather_kv, topk)


def local_band_attention(q, k, v, lookback):
    f32 = jnp.float32
    N, n, H, Dh = q.shape
    nb = -(-n // C_BLOCK)
    n_pad = nb * C_BLOCK

    def blocks(t):
        t = jnp.pad(t, ((0, 0), (0, n_pad - n), (0, 0), (0, 0)))
        return t.reshape(N, nb, C_BLOCK, H, Dh).astype(f32)

    def with_prev(t):
        prev = jnp.concatenate([jnp.zeros_like(t[:, :1]), t[:, :-1]], axis=1)
        return jnp.concatenate([prev, t], axis=2)

    qb = blocks(q)
    kw, vw = with_prev(blocks(k)), with_prev(blocks(v))
    sc = jnp.einsum('nbqhd,nbkhd->nbhqk', qb, kw) * (Dh ** -0.5)
    dist = (jnp.arange(C_BLOCK)[:, None] + C_BLOCK) - jnp.arange(2 * C_BLOCK)[None, :]
    key_abs = (jnp.arange(nb)[:, None] - 1) * C_BLOCK + jnp.arange(2 * C_BLOCK)[None, :]
    mask = ((dist >= 0) & (dist <= lookback))[None] & (key_abs >= 0)[:, None, :]
    sc = jnp.where(mask[None, :, None], sc, -jnp.inf)
    lse = jax.nn.logsumexp(sc, axis=-1, keepdims=True)
    out = jnp.einsum('nbhqk,nbkhd->nbqhd', jnp.exp(sc - lse), vw)
    lse = jnp.swapaxes(lse[..., 0], 2, 3)
    return out.reshape(N, n_pad, H, Dh)[:, :n], lse.reshape(N, n_pad, H)[:, :n]


def merge_by_denominators(outs, lses):
    wts = jax.nn.softmax(jnp.stack(lses, axis=0), axis=0)
    return sum(wts[i][..., None] * outs[i] for i in range(len(outs)))


def dilated_prompt(q, k, v):
    Bn, T = q.shape[:2]
    outs, lses = [], []
    for window, dil in C_PATTERNS:
        n = T // dil
        to_streams = lambda t: jnp.moveaxis(t.reshape(Bn, n, dil, H_C, HEAD_DIM), 2, 1).reshape(Bn * dil, n, H_C, HEAD_DIM)
        o, l = local_band_attention(to_streams(q), to_streams(k), to_streams(v), window // dil)
        outs.append(jnp.moveaxis(o.reshape(Bn, dil, n, H_C, HEAD_DIM), 1, 2).reshape(Bn, T, H_C, HEAD_DIM))
        lses.append(jnp.moveaxis(l.reshape(Bn, dil, n, H_C), 1, 2).reshape(Bn, T, H_C))
    return merge_by_denominators(outs, lses).astype(q.dtype)


def dilated_sample(q, k_new, v_new, buf_k, buf_v):
    f32 = jnp.float32
    DB, S_ = q.shape[:2]
    W_ = buf_k.shape[1]
    k_all = jnp.concatenate([buf_k, k_new.astype(buf_k.dtype)], axis=1)
    v_all = jnp.concatenate([buf_v, v_new.astype(buf_v.dtype)], axis=1)
    qf = q.astype(f32)
    outs, lses = [], []
    for window, dil in C_PATTERNS:
        m = window // dil
        idx = W_ + jnp.arange(S_)[:, None] - dil * jnp.arange(m + 1)[None, :]
        valid = idx >= 0
        idc = jnp.maximum(idx, 0)
        ks = k_all[:, idc].astype(f32)
        vs = v_all[:, idc].astype(f32)
        sc = jnp.einsum('bqhd,bqkhd->bqhk', qf, ks) * (HEAD_DIM ** -0.5)
        sc = jnp.where(valid[None, :, None, :], sc, -jnp.inf)
        lse = jax.nn.logsumexp(sc, axis=-1, keepdims=True)
        outs.append(jnp.einsum('bqhk,bqkhd->bqhd', jnp.exp(sc - lse), vs))
        lses.append(lse[..., 0])
    keep = min(C_WINDOW_MAX, W_ + S_)
    return merge_by_denominators(outs, lses).astype(q.dtype), (k_all[:, -keep:], v_all[:, -keep:])


def even_mixer(h, pos, shift_prev, wkv_prev, attend_b, w_in, w_out, rwkv_params):
    Bn, T = h.shape[:2]
    proj = h @ w_in
    pa, pb = proj[..., :A_COLS], proj[..., A_COLS:]
    ya, wkv_new, shift_new = rwkv7_mix(pa, shift_prev, wkv_prev, *rwkv_params)
    q, k, v, qi, wi, ki = split_cols(pb, (D_B, D_B, D_B, H_IDX * D_IDX, H_IDX, D_IDX))
    q = rope(q.reshape(Bn, T, H_B, HEAD_DIM), pos)
    k = rope(k.reshape(Bn, T, H_B, HEAD_DIM), pos)
    v = v.reshape(Bn, T, H_B, HEAD_DIM)
    qi = rope(qi.reshape(Bn, T, H_IDX, D_IDX), pos)
    ki = rope(ki[:, :, None, :], pos)[:, :, 0]
    yb = attend_b(q, k, v, qi, wi, ki, pos)
    y = jnp.concatenate([ya, yb.reshape(Bn, T, D_B)], axis=-1) @ w_out
    return y, (wkv_new, shift_new, k, v, ki)


def odd_mixer(h, pos, attend_c, w_in, w_out):
    Bn, T = h.shape[:2]
    q, k, v = split_cols(h @ w_in, (D_C, D_C, D_C))
    q = rope(q.reshape(Bn, T, H_C, HEAD_DIM), pos)
    k = rope(k.reshape(Bn, T, H_C, HEAD_DIM), pos)
    v = v.reshape(Bn, T, H_C, HEAD_DIM)
    yc, c_state = attend_c(q, k, v)
    return yc.reshape(Bn, T, D_C) @ w_out, c_state


def setup_inputs(seed: int = 0) -> dict:
    key = jax.random.key(seed)
    ks = iter(jax.random.split(key, 48))
    nrm = lambda shape, scale: jax.random.normal(next(ks), shape, jnp.float32) * scale
    n_pages = PAST_LEN // PAGE_SIZE
    n_used = DEC_BATCH * n_pages
    n_pool = n_used + max(1, n_used // 4)
    w_buf = min(C_WINDOW_MAX, PAST_LEN)
    page_table = jax.random.permutation(next(ks), n_pool)[:n_used].reshape(DEC_BATCH, n_pages).astype(jnp.int32)
    return {
        'x_prompt': nrm((BATCH, SEQ, D_MODEL), 1.0),
        'x_sample': nrm((DEC_BATCH, DEC_SEQ, D_MODEL), 1.0),
        'state_a_wkv': nrm((DEC_BATCH, H_A, HEAD_DIM, HEAD_DIM), 1.0),
        'state_a_shift': nrm((DEC_BATCH, A_COLS), 1.0),
        'cache_b_k': nrm((n_pool, PAGE_SIZE, H_B, HEAD_DIM), 1.0),
        'cache_b_v': nrm((n_pool, PAGE_SIZE, H_B, HEAD_DIM), 1.0),
        'cache_b_kidx': nrm((n_pool, PAGE_SIZE, D_IDX), 1.0),
        'cache_c_k': nrm((DEC_BATCH, w_buf, H_C, HEAD_DIM), 1.0),
        'cache_c_v': nrm((DEC_BATCH, w_buf, H_C, HEAD_DIM), 1.0),
        'page_table': page_table,
        'norm_mix': 1.0 + nrm((DEPTH, D_MODEL), 0.05),
        'norm_ffn': 1.0 + nrm((DEPTH, D_MODEL), 0.05),
        'norm_final': 1.0 + nrm((D_MODEL,), 0.05),
        'w_in_0': nrm((D_MODEL, A_COLS + B_COLS), D_MODEL ** -0.5),
        'w_out_0': nrm((D_A + D_B, D_MODEL), (D_A + D_B) ** -0.5),
        'a_mu': jax.random.uniform(next(ks), (A_COLS,), jnp.float32),
        'a_w0': nrm((D_A,), 1.5) - 1.0,
        'a_w2': nrm((DECAY_LORA, D_A), 0.5 * DECAY_LORA ** -0.5),
        'a_a0': nrm((D_A,), 0.5),
        'a_a2': nrm((AAA_LORA, D_A), AAA_LORA ** -0.5),
        'a_g2': nrm((GATE_LORA, D_A), GATE_LORA ** -0.5),
        'a_kk': 1.0 + nrm((D_A,), 0.1),
        'a_ka': 1.0 + nrm((D_A,), 0.1),
        'a_rk': nrm((H_A, HEAD_DIM), 0.1),
        'a_ln_w': 1.0 + nrm((D_A,), 0.05),
        'a_ln_b': nrm((D_A,), 0.05),
        'ffn_wg': nrm((D_MODEL, D_FF), D_MODEL ** -0.5),
        'ffn_wu': nrm((D_MODEL, D_FF), D_MODEL ** -0.5),
        'ffn_wd': nrm((D_FF, D_MODEL), D_FF ** -0.5),
        'w_in_1': nrm((D_MODEL, 3 * D_C), D_MODEL ** -0.5),
        'w_out_1': nrm((D_C, D_MODEL), D_C ** -0.5),
        'router_w': nrm((D_MODEL, N_EXPERTS), D_MODEL ** -0.5),
        'router_b': nrm((N_EXPERTS,), 0.01),
        'moe_wg': nrm((N_EXPERTS, D_MODEL, D_FF_EXPERT), D_MODEL ** -0.5),
        'moe_wu': nrm((N_EXPERTS, D_MODEL, D_FF_EXPERT), D_MODEL ** -0.5),
        'moe_wd': nrm((N_EXPERTS, D_FF_EXPERT, D_MODEL), D_FF_EXPERT ** -0.5),
    }


def reference(x_prompt, x_sample, state_a_wkv, state_a_shift, cache_b_k, cache_b_v, cache_b_kidx, cache_c_k, cache_c_v, page_table, norm_mix, norm_ffn, norm_final, w_in_0, w_out_0, a_mu, a_w0, a_w2, a_a0, a_a2, a_g2, a_kk, a_ka, a_rk, a_ln_w, a_ln_b, ffn_wg, ffn_wu, ffn_wd, w_in_1, w_out_1, router_w, router_b, moe_wg, moe_wu, moe_wd):
    rwkv_params = (a_mu, a_w0, a_w2, a_a0, a_a2, a_g2, a_kk, a_ka, a_rk, a_ln_w, a_ln_b)

    def trunk(x, pos, shift0, wkv0, attend_b, attend_c):
        h = x
        for layer in range(DEPTH):
            hn = rms_norm(h, norm_mix[layer])
            if layer % 2 == 0:
                y, even_state = even_mixer(hn, pos, shift0, wkv0, attend_b, w_in_0, w_out_0, rwkv_params)
                h = h + y
                h = h + swiglu(rms_norm(h, norm_ffn[layer]), ffn_wg, ffn_wu, ffn_wd)
            else:
                y, odd_state = odd_mixer(hn, pos, attend_c, w_in_1, w_out_1)
                h = h + y
                h = h + moe_swiglu(rms_norm(h, norm_ffn[layer]), router_w, router_b, moe_wg, moe_wu, moe_wd)
        return rms_norm(h, norm_final), even_state, odd_state

    def attend_b_prompt(q, k, v, qi, wi, ki, pos):
        return dsa_prompt(q, k, v, qi, wi, ki, pos)

    def attend_b_sample(q, k, v, qi, wi, ki, pos):
        return dsa_sample(q, k, v, qi, wi, ki, pos, cache_b_k, cache_b_v, cache_b_kidx, page_table)

    def attend_c_prompt(q, k, v):
        keep = min(C_WINDOW_MAX, q.shape[1])
        return dilated_prompt(q, k, v), (k[:, -keep:], v[:, -keep:])

    def attend_c_sample(q, k, v):
        return dilated_sample(q, k, v, cache_c_k, cache_c_v)

    n_prompt = x_prompt.shape[0]
    pos_p = jnp.arange(x_prompt.shape[1], dtype=jnp.int32)
    pos_s = PAST_LEN + jnp.arange(x_sample.shape[1], dtype=jnp.int32)
    shift0 = jnp.zeros((n_prompt, A_COLS), x_prompt.dtype)
    wkv0 = jnp.zeros((n_prompt, H_A, HEAD_DIM, HEAD_DIM), jnp.float32)

    y_prompt, (p_a_wkv, p_a_shift, p_b_k, p_b_v, p_b_kidx), (p_c_k, p_c_v) = trunk(
        x_prompt, pos_p, shift0, wkv0, attend_b_prompt, attend_c_prompt)
    y_sample, (s_a_wkv, s_a_shift, s_b_k, s_b_v, s_b_kidx), (s_c_k, s_c_v) = trunk(
        x_sample, pos_s, state_a_shift, state_a_wkv, attend_b_sample, attend_c_sample)
    return (y_prompt, y_sample, p_a_wkv, p_a_shift, p_b_k, p_b_v, p_b_kidx, p_c_k, p_c_v, s_a_wkv, s_a_shift, s_b_k, s_b_v, s_b_kidx, s_c_k, s_c_v)
```

```python
import functools

import numpy as np
import jax
import jax.numpy as jnp
from jax import lax
from jax.experimental import pallas as pl
from jax.experimental.pallas import tpu as pltpu

F32 = jnp.float32
BF16 = jnp.bfloat16
I32 = jnp.int32

D_MODEL = 1024
HEAD_DIM = 64
ROPE_THETA = 10000.0
NORM_EPS = 1e-6
PAGE_SIZE = 128

H_A = 8
D_A = H_A * HEAD_DIM
DECAY_LORA = 64
AAA_LORA = 64
GATE_LORA = 128
A_COLS = 3 * D_A + DECAY_LORA + AAA_LORA + GATE_LORA
GN_EPS = 64e-5

H_B = 8
D_B = H_B * HEAD_DIM
H_IDX = 8
D_IDX = 64
IDX_TOPK_MAX = 256
IDX_SCALE = (H_IDX ** -0.5) * (D_IDX ** -0.5)
B_COLS_PAD = 3 * D_B + H_IDX * D_IDX + 128

H_C = 16
D_C = H_C * HEAD_DIM
C_PATTERNS = ((128, 1), (512, 4), (2048, 16))
C_WINDOW_MAX = 2048
C_BLOCK = 128

D_FF = 2816
N_EXPERTS = 8
D_FF_EXPERT = 3584

LANES = 128
VMEM_LIMIT = 56 << 20
INT_MIN = -(2 ** 31)
NEG_BIG = -1e30


def _cparams(sem, vmem=VMEM_LIMIT):
    return pltpu.CompilerParams(dimension_semantics=sem, vmem_limit_bytes=vmem)


def _dot(a, b):
    return jnp.dot(a.astype(BF16), b.astype(BF16), preferred_element_type=F32)


def _dot_nt(a, b):
    return lax.dot_general(a.astype(BF16), b.astype(BF16), (((1,), (1,)), ((), ())), preferred_element_type=F32)


def _split(x):
    hi = x.astype(BF16)
    lo = (x - hi.astype(F32)).astype(BF16)
    return hi, lo


def _dot_split_lhs(a, b_exact):
    hi, lo = _split(a)
    return jnp.dot(hi, b_exact, preferred_element_type=F32) + jnp.dot(lo, b_exact, preferred_element_type=F32)


def _dot_split_rhs(a_exact, b):
    hi, lo = _split(b)
    return jnp.dot(a_exact, hi, preferred_element_type=F32) + jnp.dot(a_exact, lo, preferred_element_type=F32)


def _dot3(a, b):
    ah, al = _split(a)
    bh, bl = _split(b)
    return (jnp.dot(ah, bh, preferred_element_type=F32) + jnp.dot(ah, bl, preferred_element_type=F32)
            + jnp.dot(al, bh, preferred_element_type=F32))


def _iota(shape, axis):
    return lax.broadcasted_iota(I32, shape, axis)


def _rope_tables(pos):
    half = HEAD_DIM // 2
    inv_freq = jnp.power(ROPE_THETA, -jnp.arange(half, dtype=F32) / half)
    ang = pos.astype(F32)[:, None] * inv_freq[None, :]
    cos = jnp.cos(ang)
    sin = jnp.sin(ang)
    return jnp.tile(cos, (1, 4)), jnp.tile(jnp.concatenate([-sin, sin], axis=1), (1, 2))


def _rope(x, cos, sin):
    w = x.shape[-1]
    reps = w // LANES
    if reps > 1:
        cos = jnp.concatenate([cos] * reps, axis=1)
        sin = jnp.concatenate([sin] * reps, axis=1)
    first_half = (_iota(x.shape, 1) % HEAD_DIM) < (HEAD_DIM // 2)
    swapped = jnp.where(first_half, pltpu.roll(x, w - HEAD_DIM // 2, 1), pltpu.roll(x, HEAD_DIM // 2, 1))
    return x * cos + swapped * sin


def _rms(x, g):
    ms = jnp.mean(x * x, axis=-1, keepdims=True)
    return x * lax.rsqrt(ms + NORM_EPS) * g


def _inproj0_kernel(x_ref, g_ref, w_ref, cos_ref, sin_ref,
                    pa_ref, q_ref, kt_ref, v_ref, kf_ref, vf_ref, kif_ref, qi_ref, tail_ref, kit_ref):
    h = _rms(x_ref[...], g_ref[...]).astype(BF16)
    p = jnp.dot(h, w_ref[...], preferred_element_type=F32)
    cos = cos_ref[...]
    sin = sin_ref[...]
    o = A_COLS
    pa_ref[...] = p[:, :o]
    q = _rope(p[:, o:o + D_B], cos, sin)
    q_ref[...] = (q * (HEAD_DIM ** -0.5)).astype(BF16)
    k = _rope(p[:, o + D_B:o + 2 * D_B], cos, sin)
    kf_ref[...] = k
    kt_ref[...] = k.T.astype(BF16)
    v = p[:, o + 2 * D_B:o + 3 * D_B]
    vf_ref[...] = v
    v_ref[...] = v.astype(BF16)
    qi = _rope(p[:, o + 3 * D_B:o + 4 * D_B], cos, sin)
    qi_ref[...] = qi.astype(BF16)
    tail = p[:, o + 4 * D_B:]
    tail_ref[...] = tail
    ki = _rope(tail, cos, sin)
    kif_ref[...] = ki[:, :D_IDX]
    kit_ref[...] = ki.T[:D_IDX, :].astype(BF16)


def _inproj0(x, g, w_pad, cos, sin, tm):
    B, T, _ = x.shape
    nt = T // tm
    ncol = w_pad.shape[1]
    tok = lambda width: pl.BlockSpec((None, tm, width), lambda b, t: (b, t, 0))
    tr = lambda rows: pl.BlockSpec((None, rows, tm), lambda b, t: (b, 0, t))
    full = lambda a: pl.BlockSpec(a.shape, lambda b, t: (0,) * a.ndim)
    tab = pl.BlockSpec((tm, LANES), lambda b, t: (t, 0))
    sds = lambda shape, dt: jax.ShapeDtypeStruct(shape, dt)
    return pl.pallas_call(
        _inproj0_kernel,
        grid=(B, nt),
        in_specs=[tok(D_MODEL), full(g), full(w_pad), tab, tab],
        out_specs=[tok(A_COLS), tok(D_B), tr(D_B), tok(D_B), tok(D_B), tok(D_B), tok(D_IDX), tok(D_B),
                   tok(LANES), tr(D_IDX)],
        out_shape=[sds((B, T, A_COLS), F32), sds((B, T, D_B), BF16), sds((B, D_B, T), BF16),
                   sds((B, T, D_B), BF16), sds((B, T, D_B), F32), sds((B, T, D_B), F32),
                   sds((B, T, D_IDX), F32), sds((B, T, D_B), BF16), sds((B, T, LANES), F32),
                   sds((B, D_IDX, T), BF16)],
        compiler_params=_cparams(("parallel", "arbitrary")),
        name="inproj0",
    )(x, g, w_pad, cos, sin)


def _seg_sum(x, bd):
    return _dot_split_lhs(x, bd)


def _rwkv_prep_kernel(seq_mode, p_ref, prev_ref, shift_ref, mu_ref, w0_ref, w2_ref, a0_ref, a2_ref, g2_ref,
                      kk_ref, ka_ref, rk_ref, bd_ref,
                      r_out, ld_out, k_out, v_out, kkn_out, ab_out, g_out, bonus_out):
    p = p_ref[...]
    if seq_mode:
        last = jnp.where(pl.program_id(1) == 0, shift_ref[...], prev_ref[7:8, :])
        prev = jnp.where(_iota(p.shape, 0) == 0, last, pltpu.roll(p, 1, 0))
    else:
        prev = prev_ref[...]
    xm = p + (prev - p) * mu_ref[...]
    r = xm[:, :D_A]
    k = xm[:, D_A:2 * D_A]
    v = xm[:, 2 * D_A:3 * D_A]
    wa = xm[:, 3 * D_A:3 * D_A + LANES]
    gl = xm[:, 3 * D_A + LANES:]
    z = -(w0_ref[...] + _dot(jnp.tanh(wa), w2_ref[...]))
    softplus = jnp.maximum(z, 0.0) + jnp.log(1.0 + jnp.exp(-jnp.abs(z)))
    ld_out[...] = -jnp.exp(-softplus - 0.5)
    a = jax.nn.sigmoid(a0_ref[...] + _dot(wa, a2_ref[...]))
    g_out[...] = _dot(jax.nn.sigmoid(gl), g2_ref[...])
    bd = bd_ref[...]
    kk = k * kk_ref[...]
    kkn = kk * lax.rsqrt(jnp.maximum(_seg_sum(kk * kk, bd), 1e-24))
    k2 = k * (1.0 + (a - 1.0) * ka_ref[...])
    r_out[...] = r
    k_out[...] = k2
    v_out[...] = v
    kkn_out[...] = kkn
    ab_out[...] = kkn * a
    bonus_out[...] = _seg_sum(r * k2 * rk_ref[...], bd) * v


def _block_diag_ones(n, seg=HEAD_DIM):
    i = np.arange(n)
    return jnp.asarray((i[:, None] // seg) == (i[None, :] // seg), BF16)


def _rwkv_prep(pa, shift_prev, prm, tm, seq_mode):
    B, T, _ = pa.shape
    nt = T // tm
    tok = lambda width: pl.BlockSpec((None, tm, width), lambda b, t: (b, t, 0))
    full = lambda a: pl.BlockSpec(a.shape, lambda b, t: (0,) * a.ndim)
    if seq_mode:
        prev_spec = pl.BlockSpec((None, 8, A_COLS), lambda b, t: (b, jnp.maximum(t * (tm // 8) - 1, 0), 0))
        prev_arr = pa
        shift_arr = shift_prev.reshape(B, 1, A_COLS)
        shift_spec = pl.BlockSpec((None, 1, A_COLS), lambda b, t: (b, 0, 0))
    else:
        prev_spec = tok(A_COLS)
        prev_arr = shift_prev.reshape(1, T, A_COLS)
        shift_arr = jnp.zeros((1, 1, A_COLS), F32)
        shift_spec = pl.BlockSpec((None, 1, A_COLS), lambda b, t: (0, 0, 0))
    params = [prm[n] for n in ("mu", "w0", "w2", "a0", "a2", "g2", "kk", "ka", "rk", "bd")]
    out = jax.ShapeDtypeStruct((B, T, D_A), F32)
    return pl.pallas_call(
        functools.partial(_rwkv_prep_kernel, seq_mode),
        grid=(B, nt),
        in_specs=[tok(A_COLS), prev_spec, shift_spec] + [full(a) for a in params],
        out_specs=[tok(D_A)] * 8,
        out_shape=[out] * 8,
        compiler_params=_cparams(("parallel", "arbitrary")),
        name="rwkv_prep",
    )(pa, prev_arr, shift_arr, *params)


RWKV_CHUNK = 64
RWKV_GROUP = 4
RWKV_W = RWKV_GROUP * HEAD_DIM


def _rwkv_chunk(r, ld, k, v, kkn, ab, h, tri, same_head, strict, incl, eye):
    cum = _dot_split_rhs(tri, ld)
    cum_end = cum[RWKV_CHUNK - 1:RWKV_CHUNK, :]
    e_in = jnp.exp(cum)
    e_ex = jnp.exp(cum - ld)
    e_neg = jnp.exp(-cum)
    e_rem = jnp.exp(cum_end - cum)

    def expand(x):
        return jnp.where(same_head, jnp.concatenate([x] * RWKV_GROUP, axis=0), 0.0).astype(BF16)

    a_t = expand(-(kkn * e_ex))
    r_t = expand(r * e_in)
    b_t = expand(ab * e_neg)
    k_t = expand(k * e_neg)
    v_e = expand(v)
    gram = _dot_nt(jnp.concatenate([a_t, r_t], axis=0), jnp.concatenate([b_t, k_t], axis=0))
    w = RWKV_W
    l_ab = jnp.where(strict, gram[:w, :w], 0.0)
    a_ak = jnp.where(strict, gram[:w, w:], 0.0)
    a_rb = jnp.where(incl, gram[w:, :w], 0.0)
    a_rk = jnp.where(incl, gram[w:, w:], 0.0)
    pinv = jnp.where(eye, 1.0, 0.0) + l_ab
    qpow = l_ab
    for _ in range(5):
        qpow = _dot3(qpow, qpow)
        pinv = pinv + _dot3(qpow, pinv)
    x0 = _dot(a_ak, v_e)
    wu = _dot3(pinv, jnp.concatenate([a_t.astype(F32), x0], axis=1))
    u_e = _dot(wu[:, :w], h) + wu[:, w:]
    uv = jnp.concatenate([u_e.astype(BF16), v_e], axis=0)
    y_e = _dot(r_t, h) + _dot(jnp.concatenate([a_rb, a_rk], axis=1), uv)
    y = y_e[:RWKV_CHUNK]
    for i in range(1, RWKV_GROUP):
        y = y + y_e[i * RWKV_CHUNK:(i + 1) * RWKV_CHUNK]
    bk_t = jnp.concatenate([jnp.where(same_head, jnp.concatenate([ab * e_rem] * RWKV_GROUP, axis=0), 0.0).T,
                            jnp.where(same_head, jnp.concatenate([k * e_rem] * RWKV_GROUP, axis=0), 0.0).T], axis=1)
    g_col = jnp.exp(jnp.broadcast_to(cum_end, (8, w))).T[:, :1]
    h_new = h * g_col + _dot(bk_t, uv)
    return y, h_new


def _rwkv_scan_kernel(n_chunks, r_ref, ld_ref, k_ref, v_ref, kkn_ref, ab_ref, g_ref, bonus_ref, lnw_ref, lnb_ref,
                      y_ref, ht_ref, h_scr):
    @pl.when(pl.program_id(1) == 0)
    def _():
        h_scr[...] = jnp.zeros_like(h_scr)

    w = RWKV_W
    c = RWKV_CHUNK
    row = _iota((w, w), 0)
    col = _iota((w, w), 1)
    same_head = (row // c) == (col // HEAD_DIM)
    strict = (row % c) > (col % c)
    incl = (row % c) >= (col % c)
    eye = row == col
    tri = jnp.where(_iota((c, c), 0) >= _iota((c, c), 1), 1.0, 0.0).astype(BF16)
    seg_avg = jnp.where((row // HEAD_DIM) == (col // HEAD_DIM), 1.0 / HEAD_DIM, 0.0).astype(BF16)
    h = h_scr[...]
    for i in range(n_chunks):
        sl = slice(i * c, (i + 1) * c)
        y, h = _rwkv_chunk(r_ref[sl, :], ld_ref[sl, :], k_ref[sl, :], v_ref[sl, :], kkn_ref[sl, :], ab_ref[sl, :],
                           h, tri, same_head, strict, incl, eye)
        mean = _dot_split_lhs(y, seg_avg)
        yc = y - mean
        var = _dot_split_lhs(yc * yc, seg_avg)
        yn = yc * lax.rsqrt(var + GN_EPS) * lnw_ref[...] + lnb_ref[...]
        y_ref[sl, :] = ((yn + bonus_ref[sl, :]) * g_ref[sl, :]).astype(BF16)
    h_scr[...] = h

    @pl.when(pl.program_id(1) == pl.num_programs(1) - 1)
    def _():
        ht_ref[...] = h.T


def _rwkv_scan(r, ld, k, v, kkn, ab, g, bonus, lnw, lnb, tb):
    B, T, _ = r.shape
    ng = D_A // RWKV_W
    blk = pl.BlockSpec((None, tb, RWKV_W), lambda c, t: (c // ng, t, c % ng))
    par = pl.BlockSpec((1, RWKV_W), lambda c, t: (0, c % ng))
    y, ht = pl.pallas_call(
        functools.partial(_rwkv_scan_kernel, tb // RWKV_CHUNK),
        grid=(B * ng, T // tb),
        in_specs=[blk] * 8 + [par, par],
        out_specs=[blk, pl.BlockSpec((None, RWKV_W, RWKV_W), lambda c, t: (c, 0, 0))],
        out_shape=[jax.ShapeDtypeStruct((B, T, D_A), BF16), jax.ShapeDtypeStruct((B * ng, RWKV_W, RWKV_W), F32)],
        scratch_shapes=[pltpu.VMEM((RWKV_W, RWKV_W), F32)],
        compiler_params=_cparams(("parallel", "arbitrary")),
        name="rwkv_scan",
    )(r, ld, k, v, kkn, ab, g, bonus, lnw, lnb)
    ht = ht.reshape(B, ng, RWKV_GROUP, HEAD_DIM, RWKV_GROUP, HEAD_DIM)
    idx = jnp.arange(RWKV_GROUP)
    wkv = ht[:, :, idx, :, idx, :]
    return y, jnp.moveaxis(wkv, 0, 2).reshape(B, H_A, HEAD_DIM, HEAD_DIM)


def _rwkv_step_kernel(s_ref, r_ref, ld_ref, k_ref, v_ref, kkn_ref, ab_ref, g_ref, bonus_ref, lnw_ref, lnb_ref,
                      y_ref, s_out):
    rows = H_A * HEAD_DIM
    pad = 16
    rep = jnp.where((_iota((rows, pad), 0) // HEAD_DIM) == _iota((rows, pad), 1), 1.0, 0.0).astype(BF16)
    rep_t = jnp.where((_iota((pad, rows), 1) // HEAD_DIM) == _iota((pad, rows), 0), 1.0, 0.0).astype(BF16)
    zeros8 = jnp.zeros((pad - H_A, HEAD_DIM), F32)
    spread = lambda x8: _dot_split_rhs(rep, jnp.concatenate([x8, zeros8], axis=0))
    diag = (_iota((rows, HEAD_DIM), 0) % HEAD_DIM) == _iota((rows, HEAD_DIM), 1)
    s = s_ref[...].reshape(rows, HEAD_DIM)
    a_rep = -spread(kkn_ref[...])
    sa = jnp.sum(s * a_rep, axis=1, keepdims=True)
    v_col = jnp.sum(jnp.where(diag, spread(v_ref[...]), 0.0), axis=1, keepdims=True)
    s_new = s * jnp.exp(spread(ld_ref[...])) + sa * spread(ab_ref[...]) + v_col * spread(k_ref[...])
    s_out[...] = s_new.reshape(H_A, HEAD_DIM, HEAD_DIM)
    y_col = jnp.sum(s_new * spread(r_ref[...]), axis=1, keepdims=True)
    y = _dot_split_rhs(rep_t, jnp.where(diag, y_col, 0.0))[:H_A]
    mean = jnp.mean(y, axis=1, keepdims=True)
    yc = y - mean
    var = jnp.mean(yc * yc, axis=1, keepdims=True)
    yn = yc * lax.rsqrt(var + GN_EPS) * lnw_ref[...] + lnb_ref[...]
    y_ref[...] = ((yn + bonus_ref[...]) * g_ref[...]).astype(BF16)


def _rwkv_step(state, r, ld, k, v, kkn, ab, g, bonus, lnw, lnb):
    DB = state.shape[0]
    heads = lambda a: a.reshape(DB, H_A, HEAD_DIM)
    vec = pl.BlockSpec((None, H_A, HEAD_DIM), lambda b: (b, 0, 0))
    par = pl.BlockSpec((H_A, HEAD_DIM), lambda b: (0, 0))
    st = pl.BlockSpec((None, H_A, HEAD_DIM, HEAD_DIM), lambda b: (b, 0, 0, 0))
    y, s_new = pl.pallas_call(
        _rwkv_step_kernel,
        grid=(DB,),
        in_specs=[st] + [vec] * 8 + [par, par],
        out_specs=[vec, st],
        out_shape=[jax.ShapeDtypeStruct((DB, H_A, HEAD_DIM), BF16), jax.ShapeDtypeStruct(state.shape, F32)],
        compiler_params=_cparams(("parallel",)),
        name="rwkv_step",
    )(state, *(heads(a) for a in (r, ld, k, v, kkn, ab, g, bonus)), lnw.reshape(H_A, HEAD_DIM), lnb.reshape(H_A, HEAD_DIM))
    return y.reshape(DB, D_A), s_new


DSA_PG = 8


def _dsa_sample_select_kernel(topk, n_groups, idx_bits, pt_ref, qi_ref, w_ref, kin_ref, *rest):
    pages = rest[:DSA_PG]
    bias_ref, bias_new_ref, key_scr = rest[DSA_PG:]
    g = pl.program_id(1)
    qi8 = qi_ref[...]
    w8 = w_ref[...]
    allp = jnp.concatenate([p[...] for p in pages], axis=0)
    res = _dot_nt(qi8, allp)
    score = jnp.sum(jnp.maximum(res, 0.0) * w8, axis=0, keepdims=True)
    key = _order_key(score)
    for i in range(DSA_PG):
        key_scr[pl.ds(g * DSA_PG + i, 1), :] = key[:, i * PAGE_SIZE:(i + 1) * PAGE_SIZE]

    @pl.when(g == n_groups - 1)
    def _():
        n_pages = n_groups * DSA_PG
        s_new = jnp.sum(jnp.maximum(_dot_nt(qi8, kin_ref[...]), 0.0) * w8, axis=0, keepdims=True)[:, :1]
        key_new = _order_key(s_new)
        keys = key_scr[...]
        idx = _iota(keys.shape, 0) * PAGE_SIZE + _iota(keys.shape, 1)
        idx_new = n_pages * PAGE_SIZE
        total = lambda m: jnp.sum(jnp.sum(m, axis=1, keepdims=True), axis=0, keepdims=True)
        count_ge = lambda c: total(jnp.where(keys >= c, 1.0, 0.0)) + jnp.where(key_new >= c, 1.0, 0.0)
        thr = _kth_largest_key(count_ge, topk, 1)
        n_gt = total(jnp.where(keys > thr, 1.0, 0.0)) + jnp.where(key_new > thr, 1.0, 0.0)
        need = float(topk) - n_gt

        def bit_body(i, j):
            cand = j + lax.shift_left(jnp.int32(1), idx_bits - 1 - i)
            ties = (total(jnp.where(keys == thr, jnp.where(idx <= cand, 1.0, 0.0), 0.0))
                    + jnp.where(key_new == thr, jnp.where(idx_new <= cand, 1.0, 0.0), 0.0))
            return jnp.where(ties < need, cand, j)

        idx_thr = lax.fori_loop(0, idx_bits, bit_body, jnp.full((1, 1), -1, I32)) + 1
        sel = lambda kk, ii: jnp.where(kk > thr, 0.0, jnp.where(kk == thr, jnp.where(ii <= idx_thr, 0.0, NEG_BIG), NEG_BIG))
        bias_ref[...] = sel(keys, idx)
        bias_new_ref[...] = jnp.broadcast_to(sel(key_new, idx_new), bias_new_ref.shape)


def _dsa_sample_select(page_table, qi, wi, ki_new, cache_kidx):
    DB, n_pages = page_table.shape
    n_groups = n_pages // DSA_PG
    L = n_pages * PAGE_SIZE + 1
    topk = min(IDX_TOPK_MAX, L // 4)
    idx_bits = int(np.ceil(np.log2(L))) + 1
    per_seq = lambda a: pl.BlockSpec((None,) + a.shape[1:], lambda b, g, pt: (b,) + (0,) * (a.ndim - 1))
    page = lambda i: pl.BlockSpec((None, PAGE_SIZE, D_IDX), lambda b, g, pt: (pt[b, g * DSA_PG + i], 0, 0))
    grid_spec = pltpu.PrefetchScalarGridSpec(
        num_scalar_prefetch=1,
        grid=(DB, n_groups),
        in_specs=[per_seq(qi), per_seq(wi), per_seq(ki_new)] + [page(i) for i in range(DSA_PG)],
        out_specs=[pl.BlockSpec((None, n_pages, PAGE_SIZE), lambda b, g, pt: (b, 0, 0)),
                   pl.BlockSpec((None, 8, LANES), lambda b, g, pt: (b, 0, 0))],
        scratch_shapes=[pltpu.VMEM((n_pages, PAGE_SIZE), I32)],
    )
    return pl.pallas_call(
        functools.partial(_dsa_sample_select_kernel, topk, n_groups, idx_bits),
        grid_spec=grid_spec,
        out_shape=[jax.ShapeDtypeStruct((DB, n_pages, PAGE_SIZE), F32), jax.ShapeDtypeStruct((DB, 8, LANES), F32)],
        compiler_params=_cparams(("parallel", "arbitrary")),
        name="dsa_sample_select",
    )(page_table, qi, wi, ki_new, *([cache_kidx] * DSA_PG))


def _head_seg_matrix(n_heads):
    m = np.zeros((n_heads * HEAD_DIM, LANES), np.float32)
    for h in range(n_heads):
        m[h * HEAD_DIM:(h + 1) * HEAD_DIM, h] = 1.0
    return jnp.asarray(m, BF16)


def _row_attend(s, bias_col, vblk, seg_t, m, l, acc):
    s = s + bias_col
    m_new = jnp.maximum(m, jnp.max(s, axis=0, keepdims=True))
    alpha = jnp.exp(m - m_new)
    p = jnp.exp(s - m_new)
    l_new = alpha * l + jnp.sum(p, axis=0, keepdims=True)
    p_wide = _dot_split_lhs(p, seg_t)
    alpha_wide = _dot_split_lhs(jnp.broadcast_to(alpha, (8, LANES)), seg_t)[:1]
    acc_new = acc * alpha_wide + jnp.sum(p_wide * vblk, axis=0, keepdims=True)
    return m_new, l_new, acc_new


def _dsa_sample_attend_kernel(n_groups, pt_ref, q_ref, kn_ref, vn_ref, bias_ref, bias_new_ref, seg_ref, segt_ref, *rest):
    kpages = rest[:DSA_PG]
    vpages = rest[DSA_PG:2 * DSA_PG]
    o_ref, m_scr, l_scr, acc_scr = rest[2 * DSA_PG:]
    g = pl.program_id(1)

    @pl.when(g == 0)
    def _():
        m_scr[...] = jnp.full(m_scr.shape, NEG_BIG, F32)
        l_scr[...] = jnp.zeros_like(l_scr)
        acc_scr[...] = jnp.zeros_like(acc_scr)

    q = q_ref[...]
    seg = seg_ref[...]
    seg_t = segt_ref[...]
    eye = jnp.where(_iota((PAGE_SIZE, PAGE_SIZE), 0) == _iota((PAGE_SIZE, PAGE_SIZE), 1), 1.0, 0.0)
    bias16 = jnp.concatenate([bias_ref[...], jnp.zeros((16 - DSA_PG, PAGE_SIZE), F32)], axis=0)
    bias_t = _dot_nt(eye, bias16)
    m, l, acc = m_scr[...], l_scr[...], acc_scr[...]
    for i in range(DSA_PG):
        kp = kpages[i][...]
        s = _dot_split_lhs(kp.astype(BF16).astype(F32) * q, seg)
        m, l, acc = _row_attend(s, bias_t[:, i:i + 1], vpages[i][...].astype(BF16).astype(F32), seg_t, m, l, acc)
    m_scr[...], l_scr[...], acc_scr[...] = m, l, acc

    @pl.when(g == n_groups - 1)
    def _():
        kn = jnp.broadcast_to(kn_ref[...], (8, D_B))
        s = _dot_split_lhs(kn.astype(BF16).astype(F32) * q, seg)
        vn = jnp.where(_iota((8, D_B), 0) == 0, jnp.broadcast_to(vn_ref[...], (8, D_B)), 0.0).astype(BF16).astype(F32)
        bias_rows = jnp.where(_iota((8, LANES), 0) == 0, bias_new_ref[...], NEG_BIG)
        m2, l2, acc2 = _row_attend(s, bias_rows, vn, seg_t, m, l, acc)
        inv = _dot_split_lhs(jnp.broadcast_to(1.0 / l2, (8, LANES)), seg_t)[:1]
        o_ref[...] = (acc2 * inv).astype(BF16)


def _dsa_sample_attend(page_table, q, k_new, v_new, bias, bias_new, cache_k, cache_v):
    DB, n_pages = page_table.shape
    n_groups = n_pages // DSA_PG
    ck = cache_k.reshape(cache_k.shape[0], PAGE_SIZE, D_B)
    cv = cache_v.reshape(cache_v.shape[0], PAGE_SIZE, D_B)
    seg = _head_seg_matrix(H_B)
    seg_t = seg.T
    per_seq = lambda a: pl.BlockSpec((None,) + a.shape[1:], lambda b, g, pt: (b,) + (0,) * (a.ndim - 1))
    full = lambda a: pl.BlockSpec(a.shape, lambda b, g, pt: (0,) * a.ndim)
    page = lambda i: pl.BlockSpec((None, PAGE_SIZE, D_B), lambda b, g, pt: (pt[b, g * DSA_PG + i], 0, 0))
    grid_spec = pltpu.PrefetchScalarGridSpec(
        num_scalar_prefetch=1,
        grid=(DB, n_groups),
        in_specs=[per_seq(q), per_seq(k_new), per_seq(v_new),
                  pl.BlockSpec((None, DSA_PG, PAGE_SIZE), lambda b, g, pt: (b, g, 0)), per_seq(bias_new),
                  full(seg), full(seg_t)] + [page(i) for i in range(DSA_PG)] * 2,
        out_specs=pl.BlockSpec((None, 1, D_B), lambda b, g, pt: (b, 0, 0)),
        scratch_shapes=[pltpu.VMEM((1, LANES), F32), pltpu.VMEM((1, LANES), F32), pltpu.VMEM((1, D_B), F32)],
    )
    return pl.pallas_call(
        functools.partial(_dsa_sample_attend_kernel, n_groups),
        grid_spec=grid_spec,
        out_shape=jax.ShapeDtypeStruct((DB, 1, D_B), BF16),
        compiler_params=_cparams(("parallel", "arbitrary")),
        name="dsa_sample_attend",
    )(page_table, q, k_new, v_new, bias, bias_new, seg, seg_t, *([ck] * DSA_PG), *([cv] * DSA_PG))


DIL_CHUNK = 512


def _dilated_sample_kernel(w_len, q_ref, kn_ref, vn_ref, seg_ref, segt_ref, kc_ref, vc_ref, o_ref, m_scr, l_scr, acc_scr):
    c = pl.program_id(1)

    @pl.when(c == 0)
    def _():
        m_scr[...] = jnp.full(m_scr.shape, NEG_BIG, F32)
        l_scr[...] = jnp.zeros_like(l_scr)
        acc_scr[...] = jnp.zeros_like(acc_scr)

    q = q_ref[...]
    seg = seg_ref[...]
    seg_t = segt_ref[...]
    rows = kc_ref.shape[0]
    dist = w_len - (c * rows + _iota((rows, 1), 0))
    count = jnp.zeros((rows, 1), F32)
    for window, dil in C_PATTERNS:
        count = count + jnp.where(dist <= window, jnp.where(dist % dil == 0, 1.0, 0.0), 0.0)
    bias = jnp.where(count > 0.0, jnp.log(jnp.maximum(count, 1.0)), NEG_BIG)
    s = _dot_split_lhs(kc_ref[...].astype(BF16).astype(F32) * q, seg)
    m, l, acc = _row_attend(s, bias, vc_ref[...].astype(BF16).astype(F32), seg_t, m_scr[...], l_scr[...], acc_scr[...])
    m_scr[...], l_scr[...], acc_scr[...] = m, l, acc

    @pl.when(c == pl.num_programs(1) - 1)
    def _():
        kn = jnp.broadcast_to(kn_ref[...], (8, D_C))
        s2 = _dot_split_lhs(kn.astype(BF16).astype(F32) * q, seg)
        vn = jnp.where(_iota((8, D_C), 0) == 0, jnp.broadcast_to(vn_ref[...], (8, D_C)), 0.0).astype(BF16).astype(F32)
        bias2 = jnp.where(_iota((8, 1), 0) == 0, float(np.log(len(C_PATTERNS))), NEG_BIG)
        m2, l2, acc2 = _row_attend(s2, bias2, vn, seg_t, m, l, acc)
        inv = _dot_split_lhs(jnp.broadcast_to(1.0 / l2, (8, LANES)), seg_t)[:1]
        o_ref[...] = (acc2 * inv).astype(BF16)


def _dilated_sample(q, k_new, v_new, cache_k, cache_v):
    DB, w_len = cache_k.shape[:2]
    ck = cache_k.reshape(DB, w_len, D_C)
    cv = cache_v.reshape(DB, w_len, D_C)
    rows = min(DIL_CHUNK, w_len)
    seg = _head_seg_matrix(H_C)
    seg_t = seg.T
    per_seq = pl.BlockSpec((None, 1, D_C), lambda b, c: (b, 0, 0))
    full = lambda a: pl.BlockSpec(a.shape, lambda b, c: (0,) * a.ndim)
    chunk = pl.BlockSpec((None, rows, D_C), lambda b, c: (b, c, 0))
    return pl.pallas_call(
        functools.partial(_dilated_sample_kernel, w_len),
        grid=(DB, w_len // rows),
        in_specs=[per_seq, per_seq, per_seq, full(seg), full(seg_t), chunk, chunk],
        out_specs=per_seq,
        out_shape=jax.ShapeDtypeStruct((DB, 1, D_C), BF16),
        scratch_shapes=[pltpu.VMEM((1, LANES), F32), pltpu.VMEM((1, LANES), F32), pltpu.VMEM((1, D_C), F32)],
        compiler_params=_cparams(("parallel", "arbitrary")),
        name="dilated_sample",
    )(q, k_new, v_new, seg, seg_t, ck, cv)


def _silu(x):
    return x * jax.nn.sigmoid(x)


def _outproj_ffn_kernel(x_ref, ya_ref, yb_ref, wo_ref, g_ref, wg_ref, wu_ref, wd_ref, o_ref, h_scr, hn_scr, acc_scr):
    j = pl.program_id(1)

    @pl.when(j == 0)
    def _():
        y = jnp.concatenate([ya_ref[...], yb_ref[...]], axis=1)
        h = x_ref[...] + jnp.dot(y, wo_ref[...], preferred_element_type=F32)
        h_scr[...] = h
        hn_scr[...] = _rms(h, g_ref[...]).astype(BF16)
        acc_scr[...] = jnp.zeros_like(acc_scr)

    hn = hn_scr[...]
    act = _silu(jnp.dot(hn, wg_ref[...], preferred_element_type=F32)) * jnp.dot(hn, wu_ref[...], preferred_element_type=F32)
    acc_scr[...] += jnp.dot(act.astype(BF16), wd_ref[...], preferred_element_type=F32)

    @pl.when(j == pl.num_programs(1) - 1)
    def _():
        o_ref[...] = h_scr[...] + acc_scr[...]


def _outproj_ffn(x, ya, yb, wo, g, wg, wu, wd, tm, tf):
    N = x.shape[0]
    nf = D_FF // tf
    tok = lambda width: pl.BlockSpec((tm, width), lambda i, j: (i, 0))
    full = lambda a: pl.BlockSpec(a.shape, lambda i, j: (0,) * a.ndim)
    return pl.pallas_call(
        _outproj_ffn_kernel,
        grid=(N // tm, nf),
        in_specs=[tok(D_MODEL), tok(D_A), tok(D_B), full(wo), full(g),
                  pl.BlockSpec((D_MODEL, tf), lambda i, j: (0, j)), pl.BlockSpec((D_MODEL, tf), lambda i, j: (0, j)),
                  pl.BlockSpec((tf, D_MODEL), lambda i, j: (j, 0))],
        out_specs=tok(D_MODEL),
        out_shape=jax.ShapeDtypeStruct((N, D_MODEL), F32),
        scratch_shapes=[pltpu.VMEM((tm, D_MODEL), F32), pltpu.VMEM((tm, D_MODEL), BF16), pltpu.VMEM((tm, D_MODEL), F32)],
        compiler_params=_cparams(("parallel", "arbitrary")),
        name="outproj_ffn",
    )(x, ya, yb, wo, g, wg, wu, wd)


def _inproj1_kernel(x_ref, g_ref, w_ref, cos_ref, sin_ref, q_ref, k_ref, v_ref, kf_ref, vf_ref):
    h = _rms(x_ref[...], g_ref[...]).astype(BF16)
    p = jnp.dot(h, w_ref[...], preferred_element_type=F32)
    cos = cos_ref[...]
    sin = sin_ref[...]
    q_ref[...] = (_rope(p[:, :D_C], cos, sin) * (HEAD_DIM ** -0.5)).astype(BF16)
    k = _rope(p[:, D_C:2 * D_C], cos, sin)
    kf_ref[...] = k
    k_ref[...] = k.astype(BF16)
    v = p[:, 2 * D_C:]
    vf_ref[...] = v
    v_ref[...] = v.astype(BF16)


def _inproj1(x, g, w, cos, sin, tm):
    B, T, _ = x.shape
    tok = pl.BlockSpec((None, tm, D_C), lambda b, t: (b, t, 0))
    full = lambda a: pl.BlockSpec(a.shape, lambda b, t: (0,) * a.ndim)
    tab = pl.BlockSpec((tm, LANES), lambda b, t: (t, 0))
    bf = jax.ShapeDtypeStruct((B, T, D_C), BF16)
    ff = jax.ShapeDtypeStruct((B, T, D_C), F32)
    return pl.pallas_call(
        _inproj1_kernel,
        grid=(B, T // tm),
        in_specs=[tok, full(g), full(w), tab, tab],
        out_specs=[tok] * 5,
        out_shape=[bf, bf, bf, ff, ff],
        compiler_params=_cparams(("parallel", "arbitrary")),
        name="inproj1",
    )(x, g, w, cos, sin)


def _dilated_kernel(lookback, q_ref, kc_ref, kp_ref, vc_ref, vp_ref, o_ref, lse_ref):
    QB = C_BLOCK
    c = pl.program_id(2)
    lane128 = _iota((QB, LANES), 1)
    upper = (lane128 // HEAD_DIM) == 1
    upper_v = (_iota((2 * QB, LANES), 1) // HEAD_DIM) == 1
    qi = _iota((QB, 2 * QB), 0)
    kj = _iota((QB, 2 * QB), 1)
    dist = qi + QB - kj
    ok = (dist >= 0) & (dist <= lookback) & ((kj >= QB) | (c > 0))
    bias = jnp.where(ok, 0.0, NEG_BIG)
    q = q_ref[...]
    lse_blk = jnp.zeros((QB, LANES), F32)
    outs = []
    for j in range(H_C // 2):
        sl = slice(j * LANES, (j + 1) * LANES)
        pair = q[:, sl]
        q2 = jnp.concatenate([jnp.where(upper, jnp.zeros_like(pair), pair),
                              jnp.where(upper, pair, jnp.zeros_like(pair))], axis=0)
        k2 = jnp.concatenate([kp_ref[:, sl], kc_ref[:, sl]], axis=0)
        v2 = jnp.concatenate([vp_ref[:, sl], vc_ref[:, sl]], axis=0)
        s2 = _dot_nt(q2, k2)
        vv = jnp.concatenate([jnp.where(upper_v, jnp.zeros_like(v2), v2),
                              jnp.where(upper_v, v2, jnp.zeros_like(v2))], axis=0)
        ps, ls = [], []
        for u in range(2):
            s = s2[u * QB:(u + 1) * QB] + bias
            m = jnp.max(s, axis=1, keepdims=True)
            p = jnp.exp(s - m)
            l = jnp.sum(p, axis=1, keepdims=True)
            ps.append(p.astype(BF16))
            ls.append(l)
            lse_blk = lse_blk + jnp.where(lane128 == 2 * j + u, m + jnp.log(l), 0.0)
        pv = jnp.dot(jnp.concatenate(ps, axis=1), vv, preferred_element_type=F32)
        outs.append(pv / jnp.where(upper, ls[1], ls[0]))
    o_ref[...] = jnp.concatenate(outs, axis=1).astype(BF16)
    lse_ref[...] = lse_blk


def _dilated_branch(q, k, v, window, dil):
    B, T, _ = q.shape
    n = T // dil
    view = lambda a: a.reshape(B, n, dil * D_C)
    cur = pl.BlockSpec((None, C_BLOCK, D_C), lambda b, r, c: (b, c, r))
    prev = pl.BlockSpec((None, C_BLOCK, D_C), lambda b, r, c: (b, jnp.maximum(c - 1, 0), r))
    o, lse = pl.pallas_call(
        functools.partial(_dilated_kernel, window // dil),
        grid=(B, dil, n // C_BLOCK),
        in_specs=[cur, cur, prev, cur, prev],
        out_specs=[cur, pl.BlockSpec((None, C_BLOCK, LANES), lambda b, r, c: (b, c, r))],
        out_shape=[jax.ShapeDtypeStruct((B, n, dil * D_C), BF16), jax.ShapeDtypeStruct((B, n, dil * LANES), F32)],
        compiler_params=_cparams(("parallel", "parallel", "arbitrary")),
        name="dilated_w%d_d%d" % (window, dil),
    )(view(q), view(k), view(k), view(v), view(v))
    return o.reshape(B, T, D_C), lse.reshape(B, T, LANES)


def _merge_outproj_kernel(x_ref, o1_ref, o2_ref, o3_ref, l1_ref, l2_ref, l3_ref, ex_ref, wo_ref, out_ref):
    lses = [l1_ref[...], l2_ref[...], l3_ref[...]]
    m = jnp.maximum(jnp.maximum(lses[0], lses[1]), lses[2])
    es = [jnp.exp(l - m) for l in lses]
    inv = 1.0 / (es[0] + es[1] + es[2])
    y = jnp.zeros(x_ref.shape, F32)
    for e, o_ref in zip(es, (o1_ref, o2_ref, o3_ref)):
        y = y + _dot_split_lhs(e * inv, ex_ref[...]) * o_ref[...].astype(F32)
    out_ref[...] = x_ref[...] + jnp.dot(y.astype(BF16), wo_ref[...], preferred_element_type=F32)


def _head_expand_matrix():
    e = np.zeros((LANES, D_C), np.float32)
    for h in range(H_C):
        e[h, h * HEAD_DIM:(h + 1) * HEAD_DIM] = 1.0
    return jnp.asarray(e, BF16)


def _merge_outproj(x, os_, lses, wo, tm):
    N = x.shape[0]
    tok = lambda width: pl.BlockSpec((tm, width), lambda i: (i, 0))
    full = lambda a: pl.BlockSpec(a.shape, lambda i: (0,) * a.ndim)
    ex = _head_expand_matrix()
    return pl.pallas_call(
        _merge_outproj_kernel,
        grid=(N // tm,),
        in_specs=[tok(D_MODEL)] + [tok(D_C)] * 3 + [tok(LANES)] * 3 + [full(ex), full(wo)],
        out_specs=tok(D_MODEL),
        out_shape=jax.ShapeDtypeStruct((N, D_MODEL), F32),
        compiler_params=_cparams(("parallel",)),
        name="merge_outproj",
    )(x, *os_, *lses, ex, wo)


def _top2_gates(logits):
    lane = _iota(logits.shape, 1)
    m1 = jnp.max(logits, axis=1, keepdims=True)
    i1 = jnp.min(jnp.where(logits == m1, lane, LANES), axis=1, keepdims=True)
    rest = jnp.where(lane == i1, -jnp.inf, logits)
    m2 = jnp.max(rest, axis=1, keepdims=True)
    i2 = jnp.min(jnp.where(rest == m2, lane, LANES), axis=1, keepdims=True)
    e2 = jnp.exp(m2 - m1)
    g1 = 1.0 / (1.0 + e2)
    return jnp.where(lane == i1, g1, 0.0) + jnp.where(lane == i2, e2 * g1, 0.0)


def _moe_dense_kernel(x_ref, g_ref, rw_ref, rb_ref, wg_ref, wu_ref, wd_ref, gf_ref, o_ref, hn_scr, gate_scr, acc_scr):
    e = pl.program_id(1)
    j = pl.program_id(2)

    @pl.when((e == 0) & (j == 0))
    def _():
        hn = _rms(x_ref[...], g_ref[...])
        hn_scr[...] = hn.astype(BF16)
        logits = jnp.dot(hn.astype(BF16), rw_ref[...], preferred_element_type=F32) + rb_ref[...]
        gate_scr[...] = _top2_gates(logits)
        acc_scr[...] = jnp.zeros_like(acc_scr)

    hn = hn_scr[...]
    gate = gate_scr[...]
    gate_e = jnp.sum(jnp.where(_iota(gate.shape, 1) == e, gate, 0.0), axis=1, keepdims=True)
    act = _silu(jnp.dot(hn, wg_ref[...], preferred_element_type=F32)) * jnp.dot(hn, wu_ref[...], preferred_element_type=F32)
    acc_scr[...] += gate_e * jnp.dot(act.astype(BF16), wd_ref[...], preferred_element_type=F32)

    @pl.when((e == pl.num_programs(1) - 1) & (j == pl.num_programs(2) - 1))
    def _():
        o_ref[...] = _rms(x_ref[...] + acc_scr[...], gf_ref[...])


def _moe_dense(x, g, rw, rb, wg, wu, wd, gf, tm, tf):
    N = x.shape[0]
    tok = pl.BlockSpec((tm, D_MODEL), lambda i, e, j: (i, 0))
    full = lambda a: pl.BlockSpec(a.shape, lambda i, e, j: (0,) * a.ndim)
    return pl.pallas_call(
        _moe_dense_kernel,
        grid=(N // tm, N_EXPERTS, D_FF_EXPERT // tf),
        in_specs=[tok, full(g), full(rw), full(rb),
                  pl.BlockSpec((None, D_MODEL, tf), lambda i, e, j: (e, 0, j)),
                  pl.BlockSpec((None, D_MODEL, tf), lambda i, e, j: (e, 0, j)),
                  pl.BlockSpec((None, tf, D_MODEL), lambda i, e, j: (e, j, 0)), full(gf)],
        out_specs=tok,
        out_shape=jax.ShapeDtypeStruct((N, D_MODEL), F32),
        scratch_shapes=[pltpu.VMEM((tm, D_MODEL), BF16), pltpu.VMEM((tm, LANES), F32), pltpu.VMEM((tm, D_MODEL), F32)],
        compiler_params=_cparams(("parallel", "arbitrary", "arbitrary")),
        name="moe_dense",
    )(x, g, rw, rb, wg, wu, wd, gf)


def _outproj_kernel(x_ref, y_ref, wo_ref, o_ref):
    o_ref[...] = x_ref[...] + jnp.dot(y_ref[...], wo_ref[...], preferred_element_type=F32)


def _outproj(x, y, wo, tm):
    N = x.shape[0]
    tok = lambda width: pl.BlockSpec((tm, width), lambda i: (i, 0))
    return pl.pallas_call(
        _outproj_kernel,
        grid=(N // tm,),
        in_specs=[tok(D_MODEL), tok(y.shape[1]), pl.BlockSpec(wo.shape, lambda i: (0, 0))],
        out_specs=tok(D_MODEL),
        out_shape=jax.ShapeDtypeStruct((N, D_MODEL), F32),
        compiler_params=_cparams(("parallel",)),
        name="outproj",
    )(x, y, wo)


def _pad_router(router_w, router_b):
    rw = jnp.concatenate([router_w, jnp.zeros((D_MODEL, LANES - N_EXPERTS), router_w.dtype)], axis=1).astype(BF16)
    rb = jnp.concatenate([router_b.astype(F32), jnp.full((LANES - N_EXPERTS,), NEG_BIG, F32)]).reshape(1, LANES)
    return rw, rb


def _pad_w_in0(w_in_0):
    o = A_COLS + 3 * D_B + H_IDX * D_IDX
    wi = w_in_0[:, o:o + H_IDX]
    ki = w_in_0[:, o + H_IDX:o + H_IDX + D_IDX]
    pad = jnp.zeros((D_MODEL, LANES - D_IDX - H_IDX), w_in_0.dtype)
    return jnp.concatenate([w_in_0[:, :o], ki, wi, pad], axis=1).astype(BF16)


def _rwkv_params(w):
    row = lambda a: a.reshape(1, -1).astype(F32)
    zeros = jnp.zeros((DECAY_LORA, D_A), F32)
    return {
        "mu": row(w["a_mu"]), "w0": row(w["a_w0"]), "a0": row(w["a_a0"]),
        "w2": jnp.concatenate([w["a_w2"], zeros], axis=0).astype(BF16),
        "a2": jnp.concatenate([zeros, w["a_a2"]], axis=0).astype(BF16),
        "g2": w["a_g2"].astype(BF16),
        "kk": row(w["a_kk"]), "ka": row(w["a_ka"]), "rk": row(w["a_rk"]),
        "bd": _block_diag_ones(D_A), "lnw": row(w["a_ln_w"]), "lnb": row(w["a_ln_b"]),
    }


DSA_QB = 128


def _order_key(score):
    bits = lax.bitcast_convert_type(score, I32)
    key = jnp.where(bits < 0, bits ^ jnp.int32(0x7FFFFFFF), bits)
    return jnp.where(score == 0.0, 0, key)


def _fold_lanes(x):
    part = x[:, :LANES]
    for j in range(1, x.shape[1] // LANES):
        part = part + x[:, j * LANES:(j + 1) * LANES]
    return part


def _kth_largest_key(count, topk, rows):
    kf = float(topk)
    base = jnp.where(count(jnp.zeros((rows, 1), I32)) >= kf, 0, INT_MIN).astype(I32)

    def bit_body(i, base):
        cand = base + lax.shift_left(jnp.int32(1), 30 - i)
        return jnp.where(count(cand) >= kf, cand, base)

    return lax.fori_loop(0, 31, bit_body, base)


def _dsa_prompt_kernel(topk, kb_size, idx_bits, q_ref, kt_ref, v_ref, qi_ref, tail_ref, kit_ref, o_ref,
                       key_scr, thr_scr):
    QB, KB = DSA_QB, kb_size
    q_pos0 = pl.program_id(1) * QB
    nkb = lax.div(q_pos0 + QB - 1, KB) + 1
    lane128 = _iota((QB, LANES), 1)
    row = _iota((QB, KB), 0)
    lane = _iota((QB, KB), 1)
    upper = (lane128 // HEAD_DIM) == 1

    qi = qi_ref[...]
    parts = []
    for h in range(H_IDX):
        pair = qi[:, (h // 2) * LANES:(h // 2 + 1) * LANES]
        parts.append(jnp.where(upper == (h % 2 == 1), pair, jnp.zeros_like(pair)))
    lhs = jnp.concatenate(parts, axis=0)
    tail = tail_ref[...]
    wb = [jnp.broadcast_to(tail[:, D_IDX + h:D_IDX + h + 1] * IDX_SCALE, (QB, LANES)) for h in range(H_IDX)]

    def score_body(kb, carry):
        off = pl.multiple_of(kb * KB, KB)
        kit = kit_ref[:, pl.ds(off, KB)]
        res = jnp.dot(lhs, jnp.concatenate([kit, kit], axis=0), preferred_element_type=F32)
        score = jnp.zeros((QB, KB), F32)
        for h in range(H_IDX):
            wh = jnp.concatenate([wb[h]] * (KB // LANES), axis=1)
            score = score + jnp.maximum(res[h * QB:(h + 1) * QB], 0.0) * wh
        admissible = (off + lane) <= (q_pos0 + row)
        key_scr[:, pl.ds(off, KB)] = jnp.where(admissible, _order_key(score), INT_MIN)
        return carry

    lax.fori_loop(0, nkb, score_body, 0)

    def count_where(pred):
        def body(kb, acc):
            off = pl.multiple_of(kb * KB, KB)
            return acc + _fold_lanes(pred(key_scr[:, pl.ds(off, KB)], off + lane))
        acc = lax.fori_loop(0, nkb, body, jnp.zeros((QB, LANES), F32))
        return jnp.sum(acc, axis=1, keepdims=True)

    count_ge = lambda cand: count_where(lambda blk, idx: jnp.where(blk >= cand, 1.0, 0.0))
    thr = _kth_largest_key(count_ge, topk, QB)
    n_gt = count_where(lambda blk, idx: jnp.where(blk > thr, 1.0, 0.0))
    n_ge = count_ge(thr)
    need = float(topk) - n_gt
    thr_scr[:, :LANES] = jnp.broadcast_to(thr, (QB, LANES))
    thr_scr[:, LANES:] = jnp.full((QB, LANES), 2 ** 30, I32)

    @pl.when(jnp.max(n_ge - float(topk)) > 0.0)
    def _():
        def bit_body(i, j):
            cand = j + lax.shift_left(jnp.int32(1), idx_bits - 1 - i)
            ties = count_where(lambda blk, idx: jnp.where(blk == thr, jnp.where(idx <= cand, 1.0, 0.0), 0.0))
            return jnp.where(ties < need, cand, j)
        j = lax.fori_loop(0, idx_bits, bit_body, jnp.full((QB, 1), -1, I32))
        thr_scr[:, LANES:] = jnp.broadcast_to(j + 1, (QB, LANES))

    idx_thr = thr_scr[:, LANES:LANES + 1]
    tie_bias = jnp.where(thr == INT_MIN, NEG_BIG, 0.0)

    q = q_ref[...]
    qm = []
    for j in range(H_B // 2):
        pair = q[:, j * LANES:(j + 1) * LANES]
        qm.append(jnp.concatenate([jnp.where(upper, jnp.zeros_like(pair), pair),
                                   jnp.where(upper, pair, jnp.zeros_like(pair))], axis=0))
    upper_v = (_iota((KB, LANES), 1) // HEAD_DIM) == 1

    def attn_body(kb, carry):
        ms, ls, accs = carry
        off = pl.multiple_of(kb * KB, KB)
        blk = key_scr[:, pl.ds(off, KB)]
        bias = jnp.where(blk > thr, 0.0,
                         jnp.where(blk == thr, jnp.where((off + lane) <= idx_thr, tie_bias, NEG_BIG), NEG_BIG))
        new_ms, new_ls, new_accs = [], [], []
        for j in range(H_B // 2):
            s2 = jnp.dot(qm[j], kt_ref[j * LANES:(j + 1) * LANES, pl.ds(off, KB)], preferred_element_type=F32)
            vp = v_ref[pl.ds(off, KB), j * LANES:(j + 1) * LANES]
            v2 = jnp.concatenate([jnp.where(upper_v, jnp.zeros_like(vp), vp),
                                  jnp.where(upper_v, vp, jnp.zeros_like(vp))], axis=0)
            ps, alphas = [], []
            for u in range(2):
                h = 2 * j + u
                s = s2[u * QB:(u + 1) * QB] + bias
                m_new = jnp.maximum(ms[h], jnp.max(s, axis=1, keepdims=True))
                alpha = jnp.exp(ms[h] - m_new)
                p = jnp.exp(s - m_new)
                new_ms.append(m_new)
                new_ls.append(alpha * ls[h] + jnp.sum(p, axis=1, keepdims=True))
                ps.append(p.astype(BF16))
                alphas.append(alpha)
            pv = jnp.dot(jnp.concatenate(ps, axis=1), v2, preferred_element_type=F32)
            new_accs.append(accs[j] * jnp.where(upper, alphas[1], alphas[0]) + pv)
        return tuple(new_ms), tuple(new_ls), tuple(new_accs)

    init = (tuple(jnp.full((QB, 1), NEG_BIG, F32) for _ in range(H_B)),
            tuple(jnp.zeros((QB, 1), F32) for _ in range(H_B)),
            tuple(jnp.zeros((QB, LANES), F32) for _ in range(H_B // 2)))
    ms, ls, accs = lax.fori_loop(0, nkb, attn_body, init)
    outs = [accs[j] / jnp.where(upper, ls[2 * j + 1], ls[2 * j]) for j in range(H_B // 2)]
    o_ref[...] = jnp.concatenate(outs, axis=1).astype(BF16)


def _dsa_prompt(q, kt, v, qi, tail, kit):
    B, T, _ = q.shape
    topk = min(IDX_TOPK_MAX, T // 4)
    kb = min(512, T)
    idx_bits = max(1, int(np.ceil(np.log2(T))))
    blk = lambda width: pl.BlockSpec((None, DSA_QB, width), lambda b, i: (b, i, 0))
    whole = lambda r, c: pl.BlockSpec((None, r, c), lambda b, i: (b, 0, 0))
    return pl.pallas_call(
        functools.partial(_dsa_prompt_kernel, topk, kb, idx_bits),
        grid=(B, T // DSA_QB),
        in_specs=[blk(D_B), whole(D_B, T), whole(T, D_B), blk(D_B), blk(LANES), whole(D_IDX, T)],
        out_specs=blk(D_B),
        out_shape=jax.ShapeDtypeStruct((B, T, D_B), BF16),
        scratch_shapes=[pltpu.VMEM((DSA_QB, T), I32), pltpu.VMEM((DSA_QB, 2 * LANES), I32)],
        compiler_params=_cparams(("parallel", "arbitrary")),
        name="dsa_prompt",
    )(q, kt, v, qi, tail, kit)


def _tile(n, pref):
    return pref if n % pref == 0 else n


def kernel(x_prompt, x_sample, state_a_wkv, state_a_shift, cache_b_k, cache_b_v, cache_b_kidx, cache_c_k, cache_c_v, page_table, norm_mix, norm_ffn, norm_final, w_in_0, w_out_0, a_mu, a_w0, a_w2, a_a0, a_a2, a_g2, a_kk, a_ka, a_rk, a_ln_w, a_ln_b, ffn_wg, ffn_wu, ffn_wd, w_in_1, w_out_1, router_w, router_b, moe_wg, moe_wu, moe_wd):
    B, T, D = x_prompt.shape
    DB, S, _ = x_sample.shape
    assert S == 1 and D == D_MODEL
    past = page_table.shape[1] * PAGE_SIZE
    row = lambda a: a.reshape(1, -1).astype(F32)
    b16 = lambda a: a.astype(BF16)

    prm = _rwkv_params(dict(a_mu=a_mu, a_w0=a_w0, a_w2=a_w2, a_a0=a_a0, a_a2=a_a2, a_g2=a_g2, a_kk=a_kk, a_ka=a_ka,
                            a_rk=a_rk, a_ln_w=a_ln_w, a_ln_b=a_ln_b))
    w_in0 = _pad_w_in0(w_in_0)
    w_out0, w_in1, w_out1 = b16(w_out_0), b16(w_in_1), b16(w_out_1)
    f_wg, f_wu, f_wd = b16(ffn_wg), b16(ffn_wu), b16(ffn_wd)
    m_wg, m_wu, m_wd = b16(moe_wg), b16(moe_wu), b16(moe_wd)
    rw, rb = _pad_router(router_w, router_b)
    g_mix0, g_mix1 = row(norm_mix[0]), row(norm_mix[1])
    g_ffn0, g_ffn1, g_fin = row(norm_ffn[0]), row(norm_ffn[1]), row(norm_final)
    tf_ffn = D_FF // 2
    tf_moe = 512

    N = B * T
    cos_p, sin_p = _rope_tables(jnp.arange(T, dtype=I32))
    pa, q, kt, v, kf, vf, kif, qi, tail, kit = _inproj0(x_prompt, g_mix0, w_in0, cos_p, sin_p, _tile(T, 256))
    prep = _rwkv_prep(pa, jnp.zeros((B, A_COLS), F32), prm, _tile(T, 256), True)
    ya, p_a_wkv = _rwkv_scan(*prep, prm["lnw"], prm["lnb"], _tile(T, 128))
    yb = _dsa_prompt(q, kt, v, qi, tail, kit)
    h = _outproj_ffn(x_prompt.reshape(N, D), ya.reshape(N, D_A), yb.reshape(N, D_B), w_out0, g_ffn0, f_wg, f_wu, f_wd,
                     _tile(N, 512), tf_ffn)
    q1, k1, v1, k1f, v1f = _inproj1(h.reshape(B, T, D), g_mix1, w_in1, cos_p, sin_p, _tile(T, 256))
    outs, lses = [], []
    for window, dil in C_PATTERNS:
        o, lse = _dilated_branch(q1, k1, v1, window, dil)
        outs.append(o.reshape(N, D_C))
        lses.append(lse.reshape(N, LANES))
    h = _merge_outproj(h, outs, lses, w_out1, _tile(N, 512))
    y_prompt = _moe_dense(h, g_ffn1, rw, rb, m_wg, m_wu, m_wd, g_fin, _tile(N, 1024), tf_moe).reshape(B, T, D)
    keep = min(C_WINDOW_MAX, T)
    prompt_state = (p_a_wkv, pa[:, -1], kf.reshape(B, T, H_B, HEAD_DIM), vf.reshape(B, T, H_B, HEAD_DIM), kif,
                    k1f[:, -keep:].reshape(B, keep, H_C, HEAD_DIM), v1f[:, -keep:].reshape(B, keep, H_C, HEAD_DIM))

    cos_s, sin_s = _rope_tables(jnp.full((DB,), past, I32))
    xs = x_sample.reshape(1, DB, D)
    pa, q, kt, v, kf, vf, kif, qi, tail, kit = _inproj0(xs, g_mix0, w_in0, cos_s, sin_s, DB)
    prep = _rwkv_prep(pa, state_a_shift.astype(F32), prm, DB, False)
    ya, s_a_wkv = _rwkv_step(state_a_wkv.astype(F32), *(a.reshape(DB, D_A) for a in prep), prm["lnw"], prm["lnb"])
    pad8 = lambda a: jnp.concatenate([a, jnp.zeros_like(a)], axis=1)
    qi16 = pad8(qi.reshape(DB, H_IDX, D_IDX))
    wi16 = pad8((tail[0, :, D_IDX:D_IDX + H_IDX] * IDX_SCALE).reshape(DB, H_IDX, 1))
    ki16 = jnp.broadcast_to(kif.reshape(DB, 1, D_IDX), (DB, 16, D_IDX))
    bias, bias_new = _dsa_sample_select(page_table, qi16, wi16, ki16, cache_b_kidx)
    per_seq = lambda a: a.astype(F32).reshape(DB, 1, -1)
    yb = _dsa_sample_attend(page_table, per_seq(q), per_seq(kf), per_seq(vf), bias, bias_new, cache_b_k, cache_b_v)
    hs = _outproj_ffn(x_sample.reshape(DB, D), ya, yb.reshape(DB, D_B), w_out0, g_ffn0, f_wg, f_wu, f_wd, DB, tf_ffn)
    q1, k1, v1, k1f, v1f = _inproj1(hs.reshape(1, DB, D), g_mix1, w_in1, cos_s, sin_s, DB)
    yc = _dilated_sample(per_seq(q1), per_seq(k1f), per_seq(v1f), cache_c_k, cache_c_v)
    hs = _outproj(hs, yc.reshape(DB, D_C), w_out1, DB)
    y_sample = _moe_dense(hs, g_ffn1, rw, rb, m_wg, m_wu, m_wd, g_fin, DB, tf_moe).reshape(DB, 1, D)
    keep = min(C_WINDOW_MAX, cache_c_k.shape[1] + 1)
    s_c_k = jnp.concatenate([cache_c_k, k1f.reshape(DB, 1, H_C, HEAD_DIM)], axis=1)[:, -keep:]
    s_c_v = jnp.concatenate([cache_c_v, v1f.reshape(DB, 1, H_C, HEAD_DIM)], axis=1)[:, -keep:]
    sample_state = (s_a_wkv, pa[0], kf.reshape(DB, 1, H_B, HEAD_DIM), vf.reshape(DB, 1, H_B, HEAD_DIM),
                    kif.reshape(DB, 1, D_IDX), s_c_k, s_c_v)
    return (y_prompt, y_sample) + prompt_state + sample_state
```

```python
import functools

import numpy as np
import jax
import jax.numpy as jnp
from jax import lax
from jax.experimental import pallas as pl
from jax.experimental.pallas import tpu as pltpu

F32 = jnp.float32
BF16 = jnp.bfloat16
I32 = jnp.int32

D_MODEL = 1024
HEAD_DIM = 64
ROPE_THETA = 10000.0
NORM_EPS = 1e-6
PAGE_SIZE = 128

H_A = 8
D_A = H_A * HEAD_DIM
DECAY_LORA = 64
AAA_LORA = 64
GATE_LORA = 128
A_COLS = 3 * D_A + DECAY_LORA + AAA_LORA + GATE_LORA
GN_EPS = 64e-5

H_B = 8
D_B = H_B * HEAD_DIM
H_IDX = 8
D_IDX = 64
IDX_TOPK_MAX = 256
IDX_SCALE = (H_IDX ** -0.5) * (D_IDX ** -0.5)
B_COLS_PAD = 3 * D_B + H_IDX * D_IDX + 128

H_C = 16
D_C = H_C * HEAD_DIM
C_PATTERNS = ((128, 1), (512, 4), (2048, 16))
C_WINDOW_MAX = 2048
C_BLOCK = 128

D_FF = 2816
N_EXPERTS = 8
D_FF_EXPERT = 3584

LANES = 128
VMEM_LIMIT = 56 << 20
INT_MIN = -(2 ** 31)
NEG_BIG = -1e30
LOG2E = 1.4426950408889634


def _cparams(sem, vmem=VMEM_LIMIT):
    return pltpu.CompilerParams(dimension_semantics=sem, vmem_limit_bytes=vmem)


def _dot(a, b):
    return jnp.dot(a.astype(BF16), b.astype(BF16), preferred_element_type=F32)


def _dot_nt(a, b):
    return lax.dot_general(a.astype(BF16), b.astype(BF16), (((1,), (1,)), ((), ())), preferred_element_type=F32)


def _split(x):
    hi = x.astype(BF16)
    lo = (x - hi.astype(F32)).astype(BF16)
    return hi, lo


def _dot_split_lhs(a, b_exact):
    hi, lo = _split(a)
    return jnp.dot(hi, b_exact, preferred_element_type=F32) + jnp.dot(lo, b_exact, preferred_element_type=F32)


def _dot_split_rhs(a_exact, b):
    hi, lo = _split(b)
    return jnp.dot(a_exact, hi, preferred_element_type=F32) + jnp.dot(a_exact, lo, preferred_element_type=F32)


def _dot3(a, b):
    ah, al = _split(a)
    bh, bl = _split(b)
    return (jnp.dot(ah, bh, preferred_element_type=F32) + jnp.dot(ah, bl, preferred_element_type=F32)
            + jnp.dot(al, bh, preferred_element_type=F32))


def _iota(shape, axis):
    return lax.broadcasted_iota(I32, shape, axis)


def _rope_tables(pos):
    half = HEAD_DIM // 2
    inv_freq = jnp.power(ROPE_THETA, -jnp.arange(half, dtype=F32) / half)
    ang = pos.astype(F32)[:, None] * inv_freq[None, :]
    cos = jnp.cos(ang)
    sin = jnp.sin(ang)
    return jnp.tile(cos, (1, 4)), jnp.tile(jnp.concatenate([-sin, sin], axis=1), (1, 2))


def _rope(x, cos, sin):
    w = x.shape[-1]
    reps = w // LANES
    if reps > 1:
        cos = jnp.concatenate([cos] * reps, axis=1)
        sin = jnp.concatenate([sin] * reps, axis=1)
    first_half = (_iota(x.shape, 1) % HEAD_DIM) < (HEAD_DIM // 2)
    swapped = jnp.where(first_half, pltpu.roll(x, w - HEAD_DIM // 2, 1), pltpu.roll(x, HEAD_DIM // 2, 1))
    return x * cos + swapped * sin


def _rms(x, g):
    ms = jnp.mean(x * x, axis=-1, keepdims=True)
    return x * lax.rsqrt(ms + NORM_EPS) * g


def _inproj0_kernel(for_prompt, x_ref, g_ref, w_ref, cos_ref, sin_ref, pa_ref, kf_ref, vf_ref, kif_ref, *outs):
    h = _rms(x_ref[...], g_ref[...]).astype(BF16)
    p = jnp.dot(h, w_ref[...], preferred_element_type=F32)
    cos = cos_ref[...]
    sin = sin_ref[...]
    o = A_COLS
    pa_ref[...] = p[:, :o]
    q = _rope(p[:, o:o + D_B], cos, sin) * (HEAD_DIM ** -0.5)
    k = _rope(p[:, o + D_B:o + 2 * D_B], cos, sin)
    v = p[:, o + 2 * D_B:o + 3 * D_B]
    qi = _rope(p[:, o + 3 * D_B:o + 4 * D_B], cos, sin)
    tail = p[:, o + 4 * D_B:]
    ki = _rope(tail, cos, sin)[:, :D_IDX]
    kf_ref[...] = k
    vf_ref[...] = v
    kif_ref[...] = ki
    if for_prompt:
        qt_ref, k_ref, vt_ref, qit_ref, wit_ref, ki_ref = outs
        qt_ref[...] = (q * LOG2E).T.astype(BF16)
        k_ref[...] = k.astype(BF16)
        vt_ref[...] = v.T.astype(BF16)
        qit_ref[...] = qi.T.astype(BF16)
        wit_ref[...] = tail.T[D_IDX:D_IDX + H_IDX, :] * IDX_SCALE
        ki_ref[...] = ki.astype(BF16)
    else:
        q_ref, qi_ref, tail_ref = outs
        q_ref[...] = q.astype(BF16)
        qi_ref[...] = qi.astype(BF16)
        tail_ref[...] = tail


def _inproj0(x, g, w_pad, cos, sin, tm, for_prompt):
    B, T, _ = x.shape
    tok = lambda width: pl.BlockSpec((None, tm, width), lambda b, t: (b, t, 0))
    tr = lambda rows: pl.BlockSpec((None, rows, tm), lambda b, t: (b, 0, t))
    full = lambda a: pl.BlockSpec(a.shape, lambda b, t: (0,) * a.ndim)
    tab = pl.BlockSpec((tm, LANES), lambda b, t: (t, 0))
    sds = lambda shape, dt: jax.ShapeDtypeStruct(shape, dt)
    specs = [tok(A_COLS), tok(D_B), tok(D_B), tok(D_IDX)]
    shapes = [sds((B, T, A_COLS), F32), sds((B, T, D_B), F32), sds((B, T, D_B), F32), sds((B, T, D_IDX), F32)]
    if for_prompt:
        specs += [tr(D_B), tok(D_B), tr(D_B), tr(D_B), tr(H_IDX), tok(D_IDX)]
        shapes += [sds((B, D_B, T), BF16), sds((B, T, D_B), BF16), sds((B, D_B, T), BF16), sds((B, D_B, T), BF16),
                   sds((B, H_IDX, T), F32), sds((B, T, D_IDX), BF16)]
    else:
        specs += [tok(D_B), tok(D_B), tok(LANES)]
        shapes += [sds((B, T, D_B), BF16), sds((B, T, D_B), BF16), sds((B, T, LANES), F32)]
    return pl.pallas_call(
        functools.partial(_inproj0_kernel, for_prompt),
        grid=(B, T // tm),
        in_specs=[tok(D_MODEL), full(g), full(w_pad), tab, tab],
        out_specs=specs,
        out_shape=shapes,
        compiler_params=_cparams(("parallel", "arbitrary")),
        name="inproj0",
    )(x, g, w_pad, cos, sin)


def _seg_sum(x, bd):
    return _dot_split_lhs(x, bd)


def _rwkv_prep_kernel(seq_mode, p_ref, prev_ref, shift_ref, mu_ref, w0_ref, w2_ref, a0_ref, a2_ref, g2_ref,
                      kk_ref, ka_ref, rk_ref, bd_ref,
                      r_out, ld_out, k_out, v_out, kkn_out, ab_out, g_out, bonus_out):
    p = p_ref[...]
    if seq_mode:
        last = jnp.where(pl.program_id(1) == 0, shift_ref[...], prev_ref[7:8, :])
        prev = jnp.where(_iota(p.shape, 0) == 0, last, pltpu.roll(p, 1, 0))
    else:
        prev = prev_ref[...]
    xm = p + (prev - p) * mu_ref[...]
    r = xm[:, :D_A]
    k = xm[:, D_A:2 * D_A]
    v = xm[:, 2 * D_A:3 * D_A]
    wa = xm[:, 3 * D_A:3 * D_A + LANES]
    gl = xm[:, 3 * D_A + LANES:]
    z = -(w0_ref[...] + _dot(jnp.tanh(wa), w2_ref[...]))
    softplus = jnp.maximum(z, 0.0) + jnp.log(1.0 + jnp.exp(-jnp.abs(z)))
    ld_out[...] = -jnp.exp(-softplus - 0.5)
    a = jax.nn.sigmoid(a0_ref[...] + _dot(wa, a2_ref[...]))
    g_out[...] = _dot(jax.nn.sigmoid(gl), g2_ref[...])
    bd = bd_ref[...]
    kk = k * kk_ref[...]
    kkn = kk * lax.rsqrt(jnp.maximum(_seg_sum(kk * kk, bd), 1e-24))
    k2 = k * (1.0 + (a - 1.0) * ka_ref[...])
    r_out[...] = r
    k_out[...] = k2
    v_out[...] = v
    kkn_out[...] = kkn
    ab_out[...] = kkn * a
    bonus_out[...] = _seg_sum(r * k2 * rk_ref[...], bd) * v


def _block_diag_ones(n, seg=HEAD_DIM):
    i = np.arange(n)
    return jnp.asarray((i[:, None] // seg) == (i[None, :] // seg), BF16)


def _rwkv_prep(pa, shift_prev, prm, tm, seq_mode):
    B, T, _ = pa.shape
    nt = T // tm
    tok = lambda width: pl.BlockSpec((None, tm, width), lambda b, t: (b, t, 0))
    full = lambda a: pl.BlockSpec(a.shape, lambda b, t: (0,) * a.ndim)
    if seq_mode:
        prev_spec = pl.BlockSpec((None, 8, A_COLS), lambda b, t: (b, jnp.maximum(t * (tm // 8) - 1, 0), 0))
        prev_arr = pa
        shift_arr = shift_prev.reshape(B, 1, A_COLS)
        shift_spec = pl.BlockSpec((None, 1, A_COLS), lambda b, t: (b, 0, 0))
    else:
        prev_spec = tok(A_COLS)
        prev_arr = shift_prev.reshape(1, T, A_COLS)
        shift_arr = jnp.zeros((1, 1, A_COLS), F32)
        shift_spec = pl.BlockSpec((None, 1, A_COLS), lambda b, t: (0, 0, 0))
    params = [prm[n] for n in ("mu", "w0", "w2", "a0", "a2", "g2", "kk", "ka", "rk", "bd")]
    out = jax.ShapeDtypeStruct((B, T, D_A), F32)
    return pl.pallas_call(
        functools.partial(_rwkv_prep_kernel, seq_mode),
        grid=(B, nt),
        in_specs=[tok(A_COLS), prev_spec, shift_spec] + [full(a) for a in params],
        out_specs=[tok(D_A)] * 8,
        out_shape=[out] * 8,
        compiler_params=_cparams(("parallel", "arbitrary")),
        name="rwkv_prep",
    )(pa, prev_arr, shift_arr, *params)


RWKV_CHUNK = 64
RWKV_GROUP = 4
RWKV_W = RWKV_GROUP * HEAD_DIM


def _rwkv_chunk(r, ld, k, v, kkn, ab, h, tri, same_head, strict, incl, eye):
    cum = _dot_split_rhs(tri, ld)
    cum_end = cum[RWKV_CHUNK - 1:RWKV_CHUNK, :]
    e_in = jnp.exp(cum)
    e_ex = jnp.exp(cum - ld)
    e_neg = jnp.exp(-cum)
    e_rem = jnp.exp(cum_end - cum)

    def expand(x):
        return jnp.where(same_head, jnp.concatenate([x] * RWKV_GROUP, axis=0), 0.0).astype(BF16)

    a_t = expand(-(kkn * e_ex))
    r_t = expand(r * e_in)
    b_t = expand(ab * e_neg)
    k_t = expand(k * e_neg)
    v_e = expand(v)
    gram = _dot_nt(jnp.concatenate([a_t, r_t], axis=0), jnp.concatenate([b_t, k_t], axis=0))
    w = RWKV_W
    l_ab = jnp.where(strict, gram[:w, :w], 0.0)
    a_ak = jnp.where(strict, gram[:w, w:], 0.0)
    a_rb = jnp.where(incl, gram[w:, :w], 0.0)
    a_rk = jnp.where(incl, gram[w:, w:], 0.0)
    pinv = jnp.where(eye, 1.0, 0.0) + l_ab
    qpow = l_ab
    for _ in range(5):
        qpow = _dot(qpow, qpow)
        pinv = pinv + _dot(qpow, pinv)
    x0 = _dot(a_ak, v_e)
    wu = _dot(pinv, jnp.concatenate([a_t, x0.astype(BF16)], axis=1))
    u_e = _dot(wu[:, :w], h) + wu[:, w:]
    uv = jnp.concatenate([u_e.astype(BF16), v_e], axis=0)
    y_e = _dot(r_t, h) + _dot(jnp.concatenate([a_rb, a_rk], axis=1), uv)
    y = y_e[:RWKV_CHUNK]
    for i in range(1, RWKV_GROUP):
        y = y + y_e[i * RWKV_CHUNK:(i + 1) * RWKV_CHUNK]
    bk_t = jnp.concatenate([jnp.where(same_head, jnp.concatenate([ab * e_rem] * RWKV_GROUP, axis=0), 0.0).T,
                            jnp.where(same_head, jnp.concatenate([k * e_rem] * RWKV_GROUP, axis=0), 0.0).T], axis=1)
    g_col = jnp.exp(jnp.broadcast_to(cum_end, (8, w))).T[:, :1]
    h_new = h * g_col + _dot(bk_t, uv)
    return y, h_new


def _rwkv_scan_kernel(n_chunks, r_ref, ld_ref, k_ref, v_ref, kkn_ref, ab_ref, g_ref, bonus_ref, lnw_ref, lnb_ref,
                      y_ref, ht_ref, h_scr):
    @pl.when(pl.program_id(1) == 0)
    def _():
        h_scr[...] = jnp.zeros_like(h_scr)

    w = RWKV_W
    c = RWKV_CHUNK
    row = _iota((w, w), 0)
    col = _iota((w, w), 1)
    same_head = (row // c) == (col // HEAD_DIM)
    strict = (row % c) > (col % c)
    incl = (row % c) >= (col % c)
    eye = row == col
    tri = jnp.where(_iota((c, c), 0) >= _iota((c, c), 1), 1.0, 0.0).astype(BF16)
    seg_avg = jnp.where((row // HEAD_DIM) == (col // HEAD_DIM), 1.0 / HEAD_DIM, 0.0).astype(BF16)
    h = h_scr[...]
    for i in range(n_chunks):
        sl = slice(i * c, (i + 1) * c)
        y, h = _rwkv_chunk(r_ref[sl, :], ld_ref[sl, :], k_ref[sl, :], v_ref[sl, :], kkn_ref[sl, :], ab_ref[sl, :],
                           h, tri, same_head, strict, incl, eye)
        mean = _dot_split_lhs(y, seg_avg)
        yc = y - mean
        var = _dot_split_lhs(yc * yc, seg_avg)
        yn = yc * lax.rsqrt(var + GN_EPS) * lnw_ref[...] + lnb_ref[...]
        y_ref[sl, :] = ((yn + bonus_ref[sl, :]) * g_ref[sl, :]).astype(BF16)
    h_scr[...] = h

    @pl.when(pl.program_id(1) == pl.num_programs(1) - 1)
    def _():
        ht_ref[...] = h.T


def _rwkv_scan(r, ld, k, v, kkn, ab, g, bonus, lnw, lnb, tb):
    B, T, _ = r.shape
    ng = D_A // RWKV_W
    blk = pl.BlockSpec((None, tb, RWKV_W), lambda c, t: (c // ng, t, c % ng))
    par = pl.BlockSpec((1, RWKV_W), lambda c, t: (0, c % ng))
    y, ht = pl.pallas_call(
        functools.partial(_rwkv_scan_kernel, tb // RWKV_CHUNK),
        grid=(B * ng, T // tb),
        in_specs=[blk] * 8 + [par, par],
        out_specs=[blk, pl.BlockSpec((None, RWKV_W, RWKV_W), lambda c, t: (c, 0, 0))],
        out_shape=[jax.ShapeDtypeStruct((B, T, D_A), BF16), jax.ShapeDtypeStruct((B * ng, RWKV_W, RWKV_W), F32)],
        scratch_shapes=[pltpu.VMEM((RWKV_W, RWKV_W), F32)],
        compiler_params=_cparams(("parallel", "arbitrary")),
        name="rwkv_scan",
    )(r, ld, k, v, kkn, ab, g, bonus, lnw, lnb)
    ht = ht.reshape(B, ng, RWKV_GROUP, HEAD_DIM, RWKV_GROUP, HEAD_DIM)
    idx = jnp.arange(RWKV_GROUP)
    wkv = ht[:, :, idx, :, idx, :]
    return y, jnp.moveaxis(wkv, 0, 2).reshape(B, H_A, HEAD_DIM, HEAD_DIM)


def _rwkv_step_kernel(s_ref, r_ref, ld_ref, k_ref, v_ref, kkn_ref, ab_ref, g_ref, bonus_ref, lnw_ref, lnb_ref,
                      y_ref, s_out):
    rows = H_A * HEAD_DIM
    pad = 16
    rep = jnp.where((_iota((rows, pad), 0) // HEAD_DIM) == _iota((rows, pad), 1), 1.0, 0.0).astype(BF16)
    rep_t = jnp.where((_iota((pad, rows), 1) // HEAD_DIM) == _iota((pad, rows), 0), 1.0, 0.0).astype(BF16)
    zeros8 = jnp.zeros((pad - H_A, HEAD_DIM), F32)
    spread = lambda x8: _dot_split_rhs(rep, jnp.concatenate([x8, zeros8], axis=0))
    diag = (_iota((rows, HEAD_DIM), 0) % HEAD_DIM) == _iota((rows, HEAD_DIM), 1)
    s = s_ref[...].reshape(rows, HEAD_DIM)
    a_rep = -spread(kkn_ref[...])
    sa = jnp.sum(s * a_rep, axis=1, keepdims=True)
    v_col = jnp.sum(jnp.where(diag, spread(v_ref[...]), 0.0), axis=1, keepdims=True)
    s_new = s * jnp.exp(spread(ld_ref[...])) + sa * spread(ab_ref[...]) + v_col * spread(k_ref[...])
    s_out[...] = s_new.reshape(H_A, HEAD_DIM, HEAD_DIM)
    y_col = jnp.sum(s_new * spread(r_ref[...]), axis=1, keepdims=True)
    y = _dot_split_rhs(rep_t, jnp.where(diag, y_col, 0.0))[:H_A]
    mean = jnp.mean(y, axis=1, keepdims=True)
    yc = y - mean
    var = jnp.mean(yc * yc, axis=1, keepdims=True)
    yn = yc * lax.rsqrt(var + GN_EPS) * lnw_ref[...] + lnb_ref[...]
    y_ref[...] = ((yn + bonus_ref[...]) * g_ref[...]).astype(BF16)


def _rwkv_step(state, r, ld, k, v, kkn, ab, g, bonus, lnw, lnb):
    DB = state.shape[0]
    heads = lambda a: a.reshape(DB, H_A, HEAD_DIM)
    vec = pl.BlockSpec((None, H_A, HEAD_DIM), lambda b: (b, 0, 0))
    par = pl.BlockSpec((H_A, HEAD_DIM), lambda b: (0, 0))
    st = pl.BlockSpec((None, H_A, HEAD_DIM, HEAD_DIM), lambda b: (b, 0, 0, 0))
    y, s_new = pl.pallas_call(
        _rwkv_step_kernel,
        grid=(DB,),
        in_specs=[st] + [vec] * 8 + [par, par],
        out_specs=[vec, st],
        out_shape=[jax.ShapeDtypeStruct((DB, H_A, HEAD_DIM), BF16), jax.ShapeDtypeStruct(state.shape, F32)],
        compiler_params=_cparams(("parallel",)),
        name="rwkv_step",
    )(state, *(heads(a) for a in (r, ld, k, v, kkn, ab, g, bonus)), lnw.reshape(H_A, HEAD_DIM), lnb.reshape(H_A, HEAD_DIM))
    return y.reshape(DB, D_A), s_new


DSA_PG = 8


def _dsa_sample_select_kernel(topk, n_groups, idx_bits, pt_ref, qi_ref, w_ref, kin_ref, *rest):
    pages = rest[:DSA_PG]
    sel_ref, key_scr, rank_scr = rest[DSA_PG:]
    g = pl.program_id(1)
    qi8 = qi_ref[...]
    w8 = w_ref[...]
    allp = jnp.concatenate([p[...] for p in pages], axis=0)
    res = _dot_nt(qi8, allp)
    score = jnp.sum(jnp.maximum(res, 0.0) * w8, axis=0, keepdims=True)
    key = _order_key(score)
    for i in range(DSA_PG):
        key_scr[pl.ds(g * DSA_PG + i, 1), :] = key[:, i * PAGE_SIZE:(i + 1) * PAGE_SIZE]

    @pl.when(g == n_groups - 1)
    def _():
        n_pages = n_groups * DSA_PG
        s_new = jnp.sum(jnp.maximum(_dot_nt(qi8, kin_ref[...]), 0.0) * w8, axis=0, keepdims=True)[:, :1]
        key_new = _order_key(s_new)
        keys = key_scr[...]
        idx = _iota(keys.shape, 0) * PAGE_SIZE + _iota(keys.shape, 1)
        idx_new = n_pages * PAGE_SIZE
        total = lambda m: jnp.sum(jnp.sum(m, axis=1, keepdims=True), axis=0, keepdims=True)
        count_ge = lambda c: total(jnp.where(keys >= c, 1.0, 0.0)) + jnp.where(key_new >= c, 1.0, 0.0)
        thr = _kth_largest_key(count_ge, topk, (1, 1))
        n_gt = total(jnp.where(keys > thr, 1.0, 0.0)) + jnp.where(key_new > thr, 1.0, 0.0)
        need = float(topk) - n_gt

        def bit_body(i, j):
            cand = j + lax.shift_left(jnp.int32(1), idx_bits - 1 - i)
            ties = (total(jnp.where(keys == thr, jnp.where(idx <= cand, 1.0, 0.0), 0.0))
                    + jnp.where(key_new == thr, jnp.where(idx_new <= cand, 1.0, 0.0), 0.0))
            return jnp.where(ties < need, cand, j)

        idx_thr = lax.fori_loop(0, idx_bits, bit_body, jnp.full((1, 1), -1, I32)) + 1
        sel = lambda kk, ii: jnp.where(kk > thr, 1.0, jnp.where(kk == thr, jnp.where(ii <= idx_thr, 1.0, 0.0), 0.0))
        picked = sel(keys, idx)
        triu = jnp.where(_iota((PAGE_SIZE, PAGE_SIZE), 0) <= _iota((PAGE_SIZE, PAGE_SIZE), 1), 1.0, 0.0)
        within = _dot(picked, triu)
        row_tot = jnp.broadcast_to(within[:, PAGE_SIZE - 1:], (n_pages, PAGE_SIZE))
        below = jnp.where(_iota((n_pages, n_pages), 0) > _iota((n_pages, n_pages), 1), 1.0, 0.0)
        row_off = _dot(below, row_tot)
        rank_scr[...] = jnp.where(picked > 0.0, row_off + within - 1.0, -1.0)
        n_cached = row_off[n_pages - 1:, :1] + row_tot[n_pages - 1:, :1]
        rank_new = jnp.where(sel(key_new, idx_new) > 0.0, n_cached, -1.0)
        slot = _iota((topk, PAGE_SIZE), 0).astype(F32)
        lane = _iota((topk, PAGE_SIZE), 1)

        def collect(p, carry):
            page_acc, off_acc = carry
            hit = jnp.where(rank_scr[pl.ds(p, 1), :] == slot, 1.0, 0.0)
            return page_acc + hit * lax.convert_element_type(p, F32), off_acc + hit

        zero = jnp.zeros((topk, PAGE_SIZE), F32)
        page_acc, off_acc = lax.fori_loop(0, n_pages, collect, (zero, zero))
        page_idx = jnp.sum(page_acc, axis=1, keepdims=True) + jnp.where(slot[:, :1] == rank_new, float(n_pages), 0.0)
        off_idx = jnp.sum(off_acc * lane.astype(F32), axis=1, keepdims=True)
        sel_ref[...] = jnp.where(lane == 0, page_idx, jnp.where(lane == 1, off_idx, 0.0))


def _dsa_sample_select(page_table, qi, wi, ki_new, cache_kidx):
    DB, n_pages = page_table.shape
    n_groups = n_pages // DSA_PG
    L = n_pages * PAGE_SIZE + 1
    topk = min(IDX_TOPK_MAX, L // 4)
    idx_bits = int(np.ceil(np.log2(L))) + 1
    per_seq = lambda a: pl.BlockSpec((None,) + a.shape[1:], lambda b, g, pt: (b,) + (0,) * (a.ndim - 1))
    page = lambda i: pl.BlockSpec((None, PAGE_SIZE, D_IDX), lambda b, g, pt: (pt[b, g * DSA_PG + i], 0, 0))
    grid_spec = pltpu.PrefetchScalarGridSpec(
        num_scalar_prefetch=1,
        grid=(DB, n_groups),
        in_specs=[per_seq(qi), per_seq(wi), per_seq(ki_new)] + [page(i) for i in range(DSA_PG)],
        out_specs=pl.BlockSpec((None, topk, PAGE_SIZE), lambda b, g, pt: (b, 0, 0)),
        scratch_shapes=[pltpu.VMEM((n_pages, PAGE_SIZE), I32), pltpu.VMEM((n_pages, PAGE_SIZE), F32)],
    )
    return pl.pallas_call(
        functools.partial(_dsa_sample_select_kernel, topk, n_groups, idx_bits),
        grid_spec=grid_spec,
        out_shape=jax.ShapeDtypeStruct((DB, topk, PAGE_SIZE), F32),
        compiler_params=_cparams(("parallel", "arbitrary")),
        name="dsa_sample_select",
    )(page_table, qi, wi, ki_new, *([cache_kidx] * DSA_PG))


def _head_seg_matrix(n_heads):
    m = np.zeros((n_heads * HEAD_DIM, LANES), np.float32)
    for h in range(n_heads):
        m[h * HEAD_DIM:(h + 1) * HEAD_DIM, h] = 1.0
    return jnp.asarray(m, BF16)


def _row_attend(s, bias_col, vblk, seg_t, m, l, acc):
    s = s + bias_col
    m_new = jnp.maximum(m, jnp.max(s, axis=0, keepdims=True))
    alpha = jnp.exp(m - m_new)
    p = jnp.exp(s - m_new)
    l_new = alpha * l + jnp.sum(p, axis=0, keepdims=True)
    p_wide = _dot_split_lhs(p, seg_t)
    alpha_wide = _dot_split_lhs(jnp.broadcast_to(alpha, (8, LANES)), seg_t)[:1]
    acc_new = acc * alpha_wide + jnp.sum(p_wide * vblk, axis=0, keepdims=True)
    return m_new, l_new, acc_new


def _dsa_sample_attend_kernel(topk, n_pages, pt_ref, pg_ref, off_ref, q_ref, kn_ref, vn_ref, ck_ref, cv_ref, o_ref,
                              kbuf, vbuf, sems):
    b = pl.program_id(0)

    def row_copies(r):
        phys = pt_ref[b, jnp.minimum(pg_ref[b, r], n_pages - 1)]
        off = off_ref[b, r]
        return (pltpu.make_async_copy(ck_ref.at[phys, off], kbuf.at[r], sems.at[0]),
                pltpu.make_async_copy(cv_ref.at[phys, off], vbuf.at[r], sems.at[1]))

    def start(r, carry):
        for cp in row_copies(r):
            cp.start()
        return carry

    def wait(r, carry):
        for cp in row_copies(r):
            cp.wait()
        return carry

    lax.fori_loop(0, topk, start, 0)
    lax.fori_loop(0, topk, wait, 0)

    @pl.when(pg_ref[b, topk - 1] >= n_pages)
    def _():
        kbuf[topk - 1] = kn_ref[...]
        vbuf[topk - 1] = vn_ref[...]

    s = jnp.sum(kbuf[...] * q_ref[...][None], axis=2, keepdims=True)
    m = jnp.max(s, axis=0, keepdims=True)
    p = jnp.exp(s - m)
    l = jnp.sum(p, axis=0, keepdims=True)
    out = jnp.sum(p * vbuf[...], axis=0, keepdims=True) / l
    o_ref[...] = out[0].astype(BF16)


def _dsa_sample_attend(page_table, sel, q, k_new, v_new, cache_k, cache_v):
    DB, n_pages = page_table.shape
    topk = sel.shape[1]
    pages = sel[:, :, 0].astype(I32)
    offs = sel[:, :, 1].astype(I32)
    per_seq = pl.BlockSpec((None, H_B, HEAD_DIM), lambda b, *_: (b, 0, 0))
    grid_spec = pltpu.PrefetchScalarGridSpec(
        num_scalar_prefetch=3,
        grid=(DB,),
        in_specs=[per_seq, per_seq, per_seq, pl.BlockSpec(memory_space=pl.ANY), pl.BlockSpec(memory_space=pl.ANY)],
        out_specs=per_seq,
        scratch_shapes=[pltpu.VMEM((topk, H_B, HEAD_DIM), F32), pltpu.VMEM((topk, H_B, HEAD_DIM), F32),
                        pltpu.SemaphoreType.DMA((2,))],
    )
    return pl.pallas_call(
        functools.partial(_dsa_sample_attend_kernel, topk, n_pages),
        grid_spec=grid_spec,
        out_shape=jax.ShapeDtypeStruct((DB, H_B, HEAD_DIM), BF16),
        compiler_params=_cparams(("arbitrary",)),
        name="dsa_sample_attend",
    )(page_table, pages, offs, q, k_new, v_new, cache_k, cache_v)


DIL_CHUNK = 512


def _dilated_sample_kernel(w_len, q_ref, kn_ref, vn_ref, seg_ref, segt_ref, kc_ref, vc_ref, o_ref, m_scr, l_scr, acc_scr):
    c = pl.program_id(1)

    @pl.when(c == 0)
    def _():
        m_scr[...] = jnp.full(m_scr.shape, NEG_BIG, F32)
        l_scr[...] = jnp.zeros_like(l_scr)
        acc_scr[...] = jnp.zeros_like(acc_scr)

    q = q_ref[...]
    seg = seg_ref[...]
    seg_t = segt_ref[...]
    rows = kc_ref.shape[0]
    dist = w_len - (c * rows + _iota((rows, 1), 0))
    count = jnp.zeros((rows, 1), F32)
    for window, dil in C_PATTERNS:
        count = count + jnp.where(dist <= window, jnp.where(dist % dil == 0, 1.0, 0.0), 0.0)
    bias = jnp.where(count > 0.0, jnp.log(jnp.maximum(count, 1.0)), NEG_BIG)
    s = _dot_split_lhs(kc_ref[...].astype(BF16).astype(F32) * q, seg)
    m, l, acc = _row_attend(s, bias, vc_ref[...].astype(BF16).astype(F32), seg_t, m_scr[...], l_scr[...], acc_scr[...])
    m_scr[...], l_scr[...], acc_scr[...] = m, l, acc

    @pl.when(c == pl.num_programs(1) - 1)
    def _():
        kn = jnp.broadcast_to(kn_ref[...], (8, D_C))
        s2 = _dot_split_lhs(kn.astype(BF16).astype(F32) * q, seg)
        vn = jnp.where(_iota((8, D_C), 0) == 0, jnp.broadcast_to(vn_ref[...], (8, D_C)), 0.0).astype(BF16).astype(F32)
        bias2 = jnp.where(_iota((8, 1), 0) == 0, float(np.log(len(C_PATTERNS))), NEG_BIG)
        m2, l2, acc2 = _row_attend(s2, bias2, vn, seg_t, m, l, acc)
        inv = _dot_split_lhs(jnp.broadcast_to(1.0 / l2, (8, LANES)), seg_t)[:1]
        o_ref[...] = (acc2 * inv).astype(BF16)


def _dilated_sample(q, k_new, v_new, cache_k, cache_v):
    DB, w_len = cache_k.shape[:2]
    ck = cache_k.reshape(DB, w_len, D_C)
    cv = cache_v.reshape(DB, w_len, D_C)
    rows = min(DIL_CHUNK, w_len)
    seg = _head_seg_matrix(H_C)
    seg_t = seg.T
    per_seq = pl.BlockSpec((None, 1, D_C), lambda b, c: (b, 0, 0))
    full = lambda a: pl.BlockSpec(a.shape, lambda b, c: (0,) * a.ndim)
    chunk = pl.BlockSpec((None, rows, D_C), lambda b, c: (b, c, 0))
    return pl.pallas_call(
        functools.partial(_dilated_sample_kernel, w_len),
        grid=(DB, w_len // rows),
        in_specs=[per_seq, per_seq, per_seq, full(seg), full(seg_t), chunk, chunk],
        out_specs=per_seq,
        out_shape=jax.ShapeDtypeStruct((DB, 1, D_C), BF16),
        scratch_shapes=[pltpu.VMEM((1, LANES), F32), pltpu.VMEM((1, LANES), F32), pltpu.VMEM((1, D_C), F32)],
        compiler_params=_cparams(("parallel", "arbitrary")),
        name="dilated_sample",
    )(q, k_new, v_new, seg, seg_t, ck, cv)


def _silu(x):
    return x * jax.nn.sigmoid(x)


def _outproj_ffn_kernel(x_ref, ya_ref, yb_ref, wo_ref, g_ref, wg_ref, wu_ref, wd_ref, o_ref, h_scr, hn_scr, acc_scr):
    j = pl.program_id(1)

    @pl.when(j == 0)
    def _():
        y = jnp.concatenate([ya_ref[...], yb_ref[...]], axis=1)
        h = x_ref[...] + jnp.dot(y, wo_ref[...], preferred_element_type=F32)
        h_scr[...] = h
        hn_scr[...] = _rms(h, g_ref[...]).astype(BF16)
        acc_scr[...] = jnp.zeros_like(acc_scr)

    hn = hn_scr[...]
    act = _silu(jnp.dot(hn, wg_ref[...], preferred_element_type=F32)) * jnp.dot(hn, wu_ref[...], preferred_element_type=F32)
    acc_scr[...] += jnp.dot(act.astype(BF16), wd_ref[...], preferred_element_type=F32)

    @pl.when(j == pl.num_programs(1) - 1)
    def _():
        o_ref[...] = h_scr[...] + acc_scr[...]


def _outproj_ffn(x, ya, yb, wo, g, wg, wu, wd, tm, tf):
    N = x.shape[0]
    nf = D_FF // tf
    tok = lambda width: pl.BlockSpec((tm, width), lambda i, j: (i, 0))
    full = lambda a: pl.BlockSpec(a.shape, lambda i, j: (0,) * a.ndim)
    return pl.pallas_call(
        _outproj_ffn_kernel,
        grid=(N // tm, nf),
        in_specs=[tok(D_MODEL), tok(D_A), tok(D_B), full(wo), full(g),
                  pl.BlockSpec((D_MODEL, tf), lambda i, j: (0, j)), pl.BlockSpec((D_MODEL, tf), lambda i, j: (0, j)),
                  pl.BlockSpec((tf, D_MODEL), lambda i, j: (j, 0))],
        out_specs=tok(D_MODEL),
        out_shape=jax.ShapeDtypeStruct((N, D_MODEL), F32),
        scratch_shapes=[pltpu.VMEM((tm, D_MODEL), F32), pltpu.VMEM((tm, D_MODEL), BF16), pltpu.VMEM((tm, D_MODEL), F32)],
        compiler_params=_cparams(("parallel", "arbitrary")),
        name="outproj_ffn",
    )(x, ya, yb, wo, g, wg, wu, wd)


def _inproj1_kernel(x_ref, g_ref, w_ref, cos_ref, sin_ref, q_ref, k_ref, v_ref, kf_ref, vf_ref):
    h = _rms(x_ref[...], g_ref[...]).astype(BF16)
    p = jnp.dot(h, w_ref[...], preferred_element_type=F32)
    cos = cos_ref[...]
    sin = sin_ref[...]
    q_ref[...] = (_rope(p[:, :D_C], cos, sin) * (HEAD_DIM ** -0.5)).astype(BF16)
    k = _rope(p[:, D_C:2 * D_C], cos, sin)
    kf_ref[...] = k
    k_ref[...] = k.astype(BF16)
    v = p[:, 2 * D_C:]
    vf_ref[...] = v
    v_ref[...] = v.astype(BF16)


def _inproj1(x, g, w, cos, sin, tm):
    B, T, _ = x.shape
    tok = pl.BlockSpec((None, tm, D_C), lambda b, t: (b, t, 0))
    full = lambda a: pl.BlockSpec(a.shape, lambda b, t: (0,) * a.ndim)
    tab = pl.BlockSpec((tm, LANES), lambda b, t: (t, 0))
    bf = jax.ShapeDtypeStruct((B, T, D_C), BF16)
    ff = jax.ShapeDtypeStruct((B, T, D_C), F32)
    return pl.pallas_call(
        _inproj1_kernel,
        grid=(B, T // tm),
        in_specs=[tok, full(g), full(w), tab, tab],
        out_specs=[tok] * 5,
        out_shape=[bf, bf, bf, ff, ff],
        compiler_params=_cparams(("parallel", "arbitrary")),
        name="inproj1",
    )(x, g, w, cos, sin)


def _dilated_kernel(lookback, q_ref, kc_ref, kp_ref, vc_ref, vp_ref, o_ref, lse_ref):
    QB = C_BLOCK
    c = pl.program_id(2)
    lane128 = _iota((QB, LANES), 1)
    upper = (lane128 // HEAD_DIM) == 1
    upper_v = (_iota((2 * QB, LANES), 1) // HEAD_DIM) == 1
    qi = _iota((QB, 2 * QB), 0)
    kj = _iota((QB, 2 * QB), 1)
    dist = qi + QB - kj
    ok = (dist >= 0) & (dist <= lookback) & ((kj >= QB) | (c > 0))
    bias = jnp.where(ok, 0.0, NEG_BIG)
    q = q_ref[...]
    lse_blk = jnp.zeros((QB, LANES), F32)
    outs = []
    for j in range(H_C // 2):
        sl = slice(j * LANES, (j + 1) * LANES)
        pair = q[:, sl]
        q2 = jnp.concatenate([jnp.where(upper, jnp.zeros_like(pair), pair),
                              jnp.where(upper, pair, jnp.zeros_like(pair))], axis=0)
        k2 = jnp.concatenate([kp_ref[:, sl], kc_ref[:, sl]], axis=0)
        v2 = jnp.concatenate([vp_ref[:, sl], vc_ref[:, sl]], axis=0)
        s2 = _dot_nt(q2, k2)
        vv = jnp.concatenate([jnp.where(upper_v, jnp.zeros_like(v2), v2),
                              jnp.where(upper_v, v2, jnp.zeros_like(v2))], axis=0)
        ps, ls = [], []
        for u in range(2):
            s = s2[u * QB:(u + 1) * QB] + bias
            m = jnp.max(s, axis=1, keepdims=True)
            p = jnp.exp(s - m)
            l = jnp.sum(p, axis=1, keepdims=True)
            ps.append(p.astype(BF16))
            ls.append(l)
            lse_blk = lse_blk + jnp.where(lane128 == 2 * j + u, m + jnp.log(l), 0.0)
        pv = jnp.dot(jnp.concatenate(ps, axis=1), vv, preferred_element_type=F32)
        outs.append(pv / jnp.where(upper, ls[1], ls[0]))
    o_ref[...] = jnp.concatenate(outs, axis=1).astype(BF16)
    lse_ref[...] = lse_blk


def _dilated_branch(q, k, v, window, dil):
    B, T, _ = q.shape
    n = T // dil
    view = lambda a: a.reshape(B, n, dil * D_C)
    cur = pl.BlockSpec((None, C_BLOCK, D_C), lambda b, r, c: (b, c, r))
    prev = pl.BlockSpec((None, C_BLOCK, D_C), lambda b, r, c: (b, jnp.maximum(c - 1, 0), r))
    o, lse = pl.pallas_call(
        functools.partial(_dilated_kernel, window // dil),
        grid=(B, dil, n // C_BLOCK),
        in_specs=[cur, cur, prev, cur, prev],
        out_specs=[cur, pl.BlockSpec((None, C_BLOCK, LANES), lambda b, r, c: (b, c, r))],
        out_shape=[jax.ShapeDtypeStruct((B, n, dil * D_C), BF16), jax.ShapeDtypeStruct((B, n, dil * LANES), F32)],
        compiler_params=_cparams(("parallel", "parallel", "arbitrary")),
        name="dilated_w%d_d%d" % (window, dil),
    )(view(q), view(k), view(k), view(v), view(v))
    return o.reshape(B, T, D_C), lse.reshape(B, T, LANES)


def _merge_outproj_kernel(x_ref, o1_ref, o2_ref, o3_ref, l1_ref, l2_ref, l3_ref, ex_ref, wo_ref, out_ref):
    lses = [l1_ref[...], l2_ref[...], l3_ref[...]]
    m = jnp.maximum(jnp.maximum(lses[0], lses[1]), lses[2])
    es = [jnp.exp(l - m) for l in lses]
    inv = 1.0 / (es[0] + es[1] + es[2])
    y = jnp.zeros(x_ref.shape, F32)
    for e, o_ref in zip(es, (o1_ref, o2_ref, o3_ref)):
        y = y + _dot_split_lhs(e * inv, ex_ref[...]) * o_ref[...].astype(F32)
    out_ref[...] = x_ref[...] + jnp.dot(y.astype(BF16), wo_ref[...], preferred_element_type=F32)


def _head_expand_matrix():
    e = np.zeros((LANES, D_C), np.float32)
    for h in range(H_C):
        e[h, h * HEAD_DIM:(h + 1) * HEAD_DIM] = 1.0
    return jnp.asarray(e, BF16)


def _merge_outproj(x, os_, lses, wo, tm):
    N = x.shape[0]
    tok = lambda width: pl.BlockSpec((tm, width), lambda i: (i, 0))
    full = lambda a: pl.BlockSpec(a.shape, lambda i: (0,) * a.ndim)
    ex = _head_expand_matrix()
    return pl.pallas_call(
        _merge_outproj_kernel,
        grid=(N // tm,),
        in_specs=[tok(D_MODEL)] + [tok(D_C)] * 3 + [tok(LANES)] * 3 + [full(ex), full(wo)],
        out_specs=tok(D_MODEL),
        out_shape=jax.ShapeDtypeStruct((N, D_MODEL), F32),
        compiler_params=_cparams(("parallel",)),
        name="merge_outproj",
    )(x, *os_, *lses, ex, wo)


def _top2_gates(logits):
    lane = _iota(logits.shape, 1)
    m1 = jnp.max(logits, axis=1, keepdims=True)
    i1 = jnp.min(jnp.where(logits == m1, lane, LANES), axis=1, keepdims=True)
    rest = jnp.where(lane == i1, -jnp.inf, logits)
    m2 = jnp.max(rest, axis=1, keepdims=True)
    i2 = jnp.min(jnp.where(rest == m2, lane, LANES), axis=1, keepdims=True)
    e2 = jnp.exp(m2 - m1)
    g1 = 1.0 / (1.0 + e2)
    return jnp.where(lane == i1, g1, 0.0) + jnp.where(lane == i2, e2 * g1, 0.0)


def _moe_dense_kernel(x_ref, g_ref, rw_ref, rb_ref, wg_ref, wu_ref, wd_ref, gf_ref, o_ref, hn_scr, gate_scr, acc_scr):
    e = pl.program_id(1)
    j = pl.program_id(2)

    @pl.when((e == 0) & (j == 0))
    def _():
        hn = _rms(x_ref[...], g_ref[...])
        hn_scr[...] = hn.astype(BF16)
        logits = jnp.dot(hn.astype(BF16), rw_ref[...], preferred_element_type=F32) + rb_ref[...]
        gate_scr[...] = _top2_gates(logits)
        acc_scr[...] = jnp.zeros_like(acc_scr)

    hn = hn_scr[...]
    gate = gate_scr[...]
    gate_e = jnp.sum(jnp.where(_iota(gate.shape, 1) == e, gate, 0.0), axis=1, keepdims=True)
    act = _silu(jnp.dot(hn, wg_ref[...], preferred_element_type=F32)) * jnp.dot(hn, wu_ref[...], preferred_element_type=F32)
    acc_scr[...] += gate_e * jnp.dot(act.astype(BF16), wd_ref[...], preferred_element_type=F32)

    @pl.when((e == pl.num_programs(1) - 1) & (j == pl.num_programs(2) - 1))
    def _():
        o_ref[...] = _rms(x_ref[...] + acc_scr[...], gf_ref[...])


def _moe_dense(x, g, rw, rb, wg, wu, wd, gf, tm, tf):
    N = x.shape[0]
    tok = pl.BlockSpec((tm, D_MODEL), lambda i, e, j: (i, 0))
    full = lambda a: pl.BlockSpec(a.shape, lambda i, e, j: (0,) * a.ndim)
    return pl.pallas_call(
        _moe_dense_kernel,
        grid=(N // tm, N_EXPERTS, D_FF_EXPERT // tf),
        in_specs=[tok, full(g), full(rw), full(rb),
                  pl.BlockSpec((None, D_MODEL, tf), lambda i, e, j: (e, 0, j)),
                  pl.BlockSpec((None, D_MODEL, tf), lambda i, e, j: (e, 0, j)),
                  pl.BlockSpec((None, tf, D_MODEL), lambda i, e, j: (e, j, 0)), full(gf)],
        out_specs=tok,
        out_shape=jax.ShapeDtypeStruct((N, D_MODEL), F32),
        scratch_shapes=[pltpu.VMEM((tm, D_MODEL), BF16), pltpu.VMEM((tm, LANES), F32), pltpu.VMEM((tm, D_MODEL), F32)],
        compiler_params=_cparams(("parallel", "arbitrary", "arbitrary")),
        name="moe_dense",
    )(x, g, rw, rb, wg, wu, wd, gf)


MOE_TR = 1024
MOE_RC = 128
MOE_CAP = 2 * MOE_TR + N_EXPERTS * MOE_RC
MOE_PAD = 16


def _moe_route_kernel(x_ref, g_ref, rw_ref, rb_ref, hn_ref, gate_ref, pos_ref, rankt_ref, cnt_ref):
    TR = x_ref.shape[0]
    hn = _rms(x_ref[...], g_ref[...]).astype(BF16)
    hn_ref[...] = hn
    gate = _top2_gates(jnp.dot(hn, rw_ref[...], preferred_element_type=F32) + rb_ref[...])
    gate_ref[...] = gate
    sel = jnp.where(gate.T[:MOE_PAD] > 0.0, 1.0, 0.0)
    triu = jnp.where(_iota((LANES, LANES), 0) <= _iota((LANES, LANES), 1), 1.0, 0.0).astype(BF16)
    carry = jnp.zeros((MOE_PAD, 1), F32)
    ranks = []
    for c in range(TR // LANES):
        blk = sel[:, c * LANES:(c + 1) * LANES]
        pref = jnp.dot(blk.astype(BF16), triu, preferred_element_type=F32)
        ranks.append(jnp.where(blk > 0.0, carry + pref - 1.0, -1.0))
        carry = carry + pref[:, LANES - 1:]
    rank_t = jnp.concatenate(ranks, axis=1)
    rankt_ref[...] = rank_t[:N_EXPERTS]
    padded = jnp.ceil(carry * (1.0 / MOE_RC)) * MOE_RC
    lower = jnp.where(_iota((MOE_PAD, MOE_PAD), 0) > _iota((MOE_PAD, MOE_PAD), 1), 1.0, 0.0).astype(BF16)
    offs = jnp.dot(lower, jnp.broadcast_to(padded, (MOE_PAD, LANES)).astype(BF16), preferred_element_type=F32)
    cnt_ref[0:N_EXPERTS, :] = jnp.broadcast_to(carry[:N_EXPERTS], (N_EXPERTS, LANES))
    cnt_ref[N_EXPERTS:, :] = offs[:N_EXPERTS]
    pos_t = jnp.where(rank_t >= 0.0, rank_t + offs[:, :1], -1.0)
    p1 = jnp.max(pos_t, axis=0, keepdims=True)
    p2 = jnp.max(jnp.where(pos_t == p1, -1.0, pos_t), axis=0, keepdims=True)
    rows = _iota((LANES, TR), 0)
    pos_ref[...] = jnp.where(rows == 0, p1, jnp.where(rows == 1, p2, -1.0)).T


def _moe_route(x, g, rw, rb):
    N = x.shape[0]
    nt = N // MOE_TR
    tok = lambda width: pl.BlockSpec((MOE_TR, width), lambda i: (i, 0))
    full = lambda a: pl.BlockSpec(a.shape, lambda i: (0,) * a.ndim)
    return pl.pallas_call(
        _moe_route_kernel,
        grid=(nt,),
        in_specs=[tok(D_MODEL), full(g), full(rw), full(rb)],
        out_specs=[tok(D_MODEL), tok(LANES), tok(LANES), pl.BlockSpec((N_EXPERTS, MOE_TR), lambda i: (0, i)),
                   pl.BlockSpec((None, 2 * N_EXPERTS, LANES), lambda i: (i, 0, 0))],
        out_shape=[jax.ShapeDtypeStruct((N, D_MODEL), BF16), jax.ShapeDtypeStruct((N, LANES), F32),
                   jax.ShapeDtypeStruct((N, LANES), F32), jax.ShapeDtypeStruct((N_EXPERTS, N), F32),
                   jax.ShapeDtypeStruct((nt, 2 * N_EXPERTS, LANES), F32)],
        compiler_params=_cparams(("parallel",)),
        name="moe_route",
    )(x, g, rw, rb)


def _moe_routed_kernel(meta_ref, x_ref, hn_ref, gate_ref, pos_ref, rankt_ref, wg_ref, wu_ref, wd_ref, gf_ref, o_ref,
                       xc_scr, y_scr, grow_scr):
    i = pl.program_id(0)
    e = pl.program_id(1)
    j = pl.program_id(2)
    TR, RC = MOE_TR, MOE_RC
    n_chunks = lax.div(meta_ref[i, e] + (RC - 1), RC)
    off = meta_ref[i, N_EXPERTS + e]

    @pl.when((e == 0) & (j == 0))
    def _():
        y_scr[...] = jnp.zeros_like(y_scr)
        grow_scr[...] = jnp.zeros_like(grow_scr)

    @pl.when(j == 0)
    def _():
        rank_row = rankt_ref[pl.ds(e, 1), :]
        gate = gate_ref[...]
        gate_e = jnp.where(_iota(gate.shape, 1) == e, gate, 0.0)

        def gather(c, carry):
            rows = pl.multiple_of(off + c * RC, RC)
            want = (c * RC + _iota((RC, 1), 0)).astype(F32)
            onehot = jnp.where(rank_row == want, 1.0, 0.0).astype(BF16)
            xc_scr[pl.ds(rows, RC), :] = jnp.dot(onehot, hn_ref[...], preferred_element_type=F32).astype(BF16)
            grow_scr[pl.ds(rows, RC), :] = _dot_split_rhs(onehot, gate_e)
            return carry

        lax.fori_loop(0, n_chunks, gather, 0)

    def expert(c, carry):
        rows = pl.multiple_of(off + c * RC, RC)
        xc = xc_scr[pl.ds(rows, RC), :]
        act = _silu(jnp.dot(xc, wg_ref[...], preferred_element_type=F32)) * jnp.dot(xc, wu_ref[...], preferred_element_type=F32)
        y_scr[pl.ds(rows, RC), :] += jnp.dot(act.astype(BF16), wd_ref[...], preferred_element_type=F32)
        return carry

    lax.fori_loop(0, n_chunks, expert, 0)

    @pl.when((e == pl.num_programs(1) - 1) & (j == pl.num_programs(2) - 1))
    def _():
        step = 256
        for r in range(MOE_CAP // step):
            sl = slice(r * step, (r + 1) * step)
            gr = jnp.sum(grow_scr[sl, :], axis=1, keepdims=True)
            xc_scr[sl, :] = (y_scr[sl, :] * gr).astype(BF16)
        lane = _iota((step, MOE_CAP), 1).astype(F32)
        for r in range(TR // step):
            sl = slice(r * step, (r + 1) * step)
            pos = pos_ref[sl, :]
            scatter = jnp.where(lane == pos[:, 0:1], 1.0, jnp.where(lane == pos[:, 1:2], 1.0, 0.0)).astype(BF16)
            y = jnp.dot(scatter, xc_scr[...], preferred_element_type=F32)
            o_ref[sl, :] = _rms(x_ref[sl, :] + y, gf_ref[...])


def _moe_routed(x, g, rw, rb, wg, wu, wd, gf, tf):
    N = x.shape[0]
    nt = N // MOE_TR
    hn, gate, pos, rank_t, meta = _moe_route(x, g, rw, rb)
    meta = meta[:, :, 0].astype(I32)
    tok = lambda width: pl.BlockSpec((MOE_TR, width), lambda i, e, j, m: (i, 0))
    full = lambda a: pl.BlockSpec(a.shape, lambda i, e, j, m: (0,) * a.ndim)
    grid_spec = pltpu.PrefetchScalarGridSpec(
        num_scalar_prefetch=1,
        grid=(nt, N_EXPERTS, D_FF_EXPERT // tf),
        in_specs=[tok(D_MODEL), tok(D_MODEL), tok(LANES), tok(LANES),
                  pl.BlockSpec((N_EXPERTS, MOE_TR), lambda i, e, j, m: (0, i)),
                  pl.BlockSpec((None, D_MODEL, tf), lambda i, e, j, m: (e, 0, j)),
                  pl.BlockSpec((None, D_MODEL, tf), lambda i, e, j, m: (e, 0, j)),
                  pl.BlockSpec((None, tf, D_MODEL), lambda i, e, j, m: (e, j, 0)), full(gf)],
        out_specs=tok(D_MODEL),
        scratch_shapes=[pltpu.VMEM((MOE_CAP, D_MODEL), BF16), pltpu.VMEM((MOE_CAP, D_MODEL), F32),
                        pltpu.VMEM((MOE_CAP, LANES), F32)],
    )
    return pl.pallas_call(
        _moe_routed_kernel,
        grid_spec=grid_spec,
        out_shape=jax.ShapeDtypeStruct((N, D_MODEL), F32),
        compiler_params=_cparams(("parallel", "arbitrary", "arbitrary")),
        name="moe_routed",
    )(meta, x, hn, gate, pos, rank_t, wg, wu, wd, gf)


def _outproj_kernel(x_ref, y_ref, wo_ref, o_ref):
    o_ref[...] = x_ref[...] + jnp.dot(y_ref[...], wo_ref[...], preferred_element_type=F32)


def _outproj(x, y, wo, tm):
    N = x.shape[0]
    tok = lambda width: pl.BlockSpec((tm, width), lambda i: (i, 0))
    return pl.pallas_call(
        _outproj_kernel,
        grid=(N // tm,),
        in_specs=[tok(D_MODEL), tok(y.shape[1]), pl.BlockSpec(wo.shape, lambda i: (0, 0))],
        out_specs=tok(D_MODEL),
        out_shape=jax.ShapeDtypeStruct((N, D_MODEL), F32),
        compiler_params=_cparams(("parallel",)),
        name="outproj",
    )(x, y, wo)


def _pad_router(router_w, router_b):
    rw = jnp.concatenate([router_w, jnp.zeros((D_MODEL, LANES - N_EXPERTS), router_w.dtype)], axis=1).astype(BF16)
    rb = jnp.concatenate([router_b.astype(F32), jnp.full((LANES - N_EXPERTS,), NEG_BIG, F32)]).reshape(1, LANES)
    return rw, rb


def _pad_w_in0(w_in_0):
    o = A_COLS + 3 * D_B + H_IDX * D_IDX
    wi = w_in_0[:, o:o + H_IDX]
    ki = w_in_0[:, o + H_IDX:o + H_IDX + D_IDX]
    pad = jnp.zeros((D_MODEL, LANES - D_IDX - H_IDX), w_in_0.dtype)
    return jnp.concatenate([w_in_0[:, :o], ki, wi, pad], axis=1).astype(BF16)


def _rwkv_params(w):
    row = lambda a: a.reshape(1, -1).astype(F32)
    zeros = jnp.zeros((DECAY_LORA, D_A), F32)
    return {
        "mu": row(w["a_mu"]), "w0": row(w["a_w0"]), "a0": row(w["a_a0"]),
        "w2": jnp.concatenate([w["a_w2"], zeros], axis=0).astype(BF16),
        "a2": jnp.concatenate([zeros, w["a_a2"]], axis=0).astype(BF16),
        "g2": w["a_g2"].astype(BF16),
        "kk": row(w["a_kk"]), "ka": row(w["a_ka"]), "rk": row(w["a_rk"]),
        "bd": _block_diag_ones(D_A), "lnw": row(w["a_ln_w"]), "lnb": row(w["a_ln_b"]),
    }


DSA_QB = 128


def _order_key(score):
    bits = lax.bitcast_convert_type(score, I32)
    key = jnp.where(bits < 0, bits ^ jnp.int32(0x7FFFFFFF), bits)
    return jnp.where(score == 0.0, 0, key)


def _fold_lanes(x):
    part = x[:, :LANES]
    for j in range(1, x.shape[1] // LANES):
        part = part + x[:, j * LANES:(j + 1) * LANES]
    return part


def _fold_rows(x, rows=64):
    rows = min(rows, x.shape[0])
    part = x[:rows]
    for j in range(1, x.shape[0] // rows):
        part = part + x[j * rows:(j + 1) * rows]
    return part


def _kth_largest_key(count, topk, shape):
    kf = float(topk)
    base = jnp.where(count(jnp.zeros(shape, I32)) >= kf, 0, INT_MIN).astype(I32)

    def bit_body(i, base):
        cand = base + lax.shift_left(jnp.int32(1), 30 - i)
        return jnp.where(count(cand) >= kf, cand, base)

    return lax.fori_loop(0, 31, bit_body, base)


def _dsa_prompt_kernel(topk, KB, KA, idx_bits, qt_ref, k_ref, vt_ref, qit_ref, wit_ref, ki_ref, o_ref,
                       key_scr, thr_scr, s_scr, bias_scr, p_scr, m_scr, l_scr, acc_scr):
    QB = DSA_QB
    q_pos0 = pl.program_id(1) * QB
    nkb = lax.div(q_pos0 + QB - 1, KB) + 1
    nka = lax.div(q_pos0 + QB - 1, KA) + 1
    key_row = _iota((KB, QB), 0)
    q_lane = _iota((KB, QB), 1)

    qit = qit_ref[...]
    q_cat = jnp.concatenate([qit[h * D_IDX:(h + 1) * D_IDX] for h in range(H_IDX)], axis=1)
    wit = wit_ref[...]
    w_cat = jnp.concatenate([wit[h:h + 1] for h in range(H_IDX)], axis=1)

    def score_body(kb, carry):
        off = pl.multiple_of(kb * KB, KB)
        res = jnp.dot(ki_ref[pl.ds(off, KB), :], q_cat, preferred_element_type=F32)
        weighted = jnp.maximum(res, 0.0) * w_cat
        admissible = (off + key_row) <= (q_pos0 + q_lane)
        key_scr[pl.ds(off, KB), :] = jnp.where(admissible, _order_key(_fold_lanes(weighted)), INT_MIN)
        return carry

    lax.fori_loop(0, nkb, score_body, 0)

    def count_where(pred):
        def body(kb, acc):
            off = pl.multiple_of(kb * KB, KB)
            return acc + _fold_rows(pred(key_scr[pl.ds(off, KB), :], off + key_row))
        acc = lax.fori_loop(0, nkb, body, jnp.zeros((min(64, KB), QB), F32))
        return jnp.sum(acc, axis=0, keepdims=True)

    count_ge = lambda cand: count_where(lambda blk, idx: jnp.where(blk >= cand, 1.0, 0.0))
    thr = _kth_largest_key(count_ge, topk, (1, QB))
    n_gt = count_where(lambda blk, idx: jnp.where(blk > thr, 1.0, 0.0))
    n_ge = count_ge(thr)
    need = float(topk) - n_gt
    thr_scr[...] = jnp.full((8, QB), 2 ** 30, I32)

    @pl.when(jnp.max(n_ge - float(topk)) > 0.0)
    def _():
        def bit_body(i, j):
            cand = j + lax.shift_left(jnp.int32(1), idx_bits - 1 - i)
            ties = count_where(lambda blk, idx: jnp.where(blk == thr, jnp.where(idx <= cand, 1.0, 0.0), 0.0))
            return jnp.where(ties < need, cand, j)
        j = lax.fori_loop(0, idx_bits, bit_body, jnp.full((1, QB), -1, I32))
        thr_scr[...] = jnp.broadcast_to(j + 1, (8, QB))

    idx_thr = thr_scr[0:1, :]
    tie_bias = jnp.where(thr == INT_MIN, NEG_BIG, 0.0)

    qt = qt_ref[...]
    feat = _iota((LANES, QB), 0)
    qtm = []
    for h in range(H_B):
        pair = qt[(h // 2) * LANES:(h // 2 + 1) * LANES]
        qtm.append(jnp.where((feat // HEAD_DIM) == (h % 2), pair, jnp.zeros_like(pair)))
    q_pairs = [jnp.concatenate([qtm[2 * j], qtm[2 * j + 1]], axis=1) for j in range(H_B // 2)]
    key_row_a = _iota((KA, QB), 0)

    def scores_into(slot, ka):
        off = pl.multiple_of(ka * KA, KA)
        for j in range(H_B // 2):
            s2 = jnp.dot(k_ref[pl.ds(off, KA), j * LANES:(j + 1) * LANES], q_pairs[j], preferred_element_type=F32)
            s_scr[slot, 2 * j] = s2[:, :QB]
            s_scr[slot, 2 * j + 1] = s2[:, QB:]

    scores_into(0, 0)

    m_scr[...] = jnp.full(m_scr.shape, NEG_BIG, F32)
    l_scr[...] = jnp.zeros_like(l_scr)
    acc_scr[...] = jnp.zeros_like(acc_scr)
    CH = 64
    n_ch = KA // CH

    def attend_block(slot, ka):
        live = ka < nka
        ka = jnp.minimum(ka, nka - 1)
        off = pl.multiple_of(ka * KA, KA)
        blk = key_scr[pl.ds(off, KA), :]
        hit = jnp.where(live, 0.0, NEG_BIG)
        bias_scr[...] = jnp.where(blk > thr, hit, jnp.where(
            blk == thr, jnp.where((off + key_row_a) <= idx_thr, tie_bias + hit, NEG_BIG), NEG_BIG))
        for h in range(H_B):
            top = None
            for c in range(n_ch):
                rows = slice(c * CH, (c + 1) * CH)
                piece = s_scr[slot, h, rows, :] + bias_scr[rows, :]
                s_scr[slot, h, rows, :] = piece
                top = piece if top is None else jnp.maximum(top, piece)
            m_old = m_scr[h:h + 1, :]
            m_new = jnp.maximum(m_old, jnp.max(top, axis=0, keepdims=True))
            alpha = jnp.exp2(m_old - m_new)
            total = None
            for c in range(n_ch):
                rows = slice(c * CH, (c + 1) * CH)
                e = jnp.exp2(s_scr[slot, h, rows, :] - m_new)
                p_scr[h, rows, :] = e.astype(BF16)
                total = e if total is None else total + e
            m_scr[h:h + 1, :] = m_new
            l_scr[h:h + 1, :] = alpha * l_scr[h:h + 1, :] + jnp.sum(total, axis=0, keepdims=True)
            pv = jnp.dot(vt_ref[h * HEAD_DIM:(h + 1) * HEAD_DIM, pl.ds(off, KA)], p_scr[h],
                         preferred_element_type=F32)
            acc_scr[h] = acc_scr[h] * alpha + pv

    def attn_body(i, carry):
        scores_into(1, jnp.minimum(2 * i + 1, nka - 1))
        attend_block(0, 2 * i)
        scores_into(0, jnp.minimum(2 * i + 2, nka - 1))
        attend_block(1, 2 * i + 1)
        return carry

    lax.fori_loop(0, lax.div(nka + 1, 2), attn_body, 0)
    out_t = jnp.concatenate([acc_scr[h] / l_scr[h:h + 1, :] for h in range(H_B)], axis=0)
    o_ref[...] = out_t.T.astype(BF16)


def _dsa_prompt(qt, k, vt, qit, wit, ki):
    B, T, _ = k.shape
    topk = min(IDX_TOPK_MAX, T // 4)
    kb = min(512, T)
    ka = min(512, T)
    idx_bits = max(1, int(np.ceil(np.log2(T))))
    qcols = lambda rows: pl.BlockSpec((None, rows, DSA_QB), lambda b, i: (b, 0, i))
    whole = lambda r, c: pl.BlockSpec((None, r, c), lambda b, i: (b, 0, 0))
    return pl.pallas_call(
        functools.partial(_dsa_prompt_kernel, topk, kb, ka, idx_bits),
        grid=(B, T // DSA_QB),
        in_specs=[qcols(D_B), whole(T, D_B), whole(D_B, T), qcols(D_B), qcols(H_IDX), whole(T, D_IDX)],
        out_specs=pl.BlockSpec((None, DSA_QB, D_B), lambda b, i: (b, i, 0)),
        out_shape=jax.ShapeDtypeStruct((B, T, D_B), BF16),
        scratch_shapes=[pltpu.VMEM((T, DSA_QB), I32), pltpu.VMEM((8, DSA_QB), I32),
                        pltpu.VMEM((2, H_B, ka, DSA_QB), F32), pltpu.VMEM((ka, DSA_QB), F32),
                        pltpu.VMEM((H_B, ka, DSA_QB), BF16), pltpu.VMEM((H_B, DSA_QB), F32),
                        pltpu.VMEM((H_B, DSA_QB), F32), pltpu.VMEM((H_B, HEAD_DIM, DSA_QB), F32)],
        compiler_params=_cparams(("parallel", "arbitrary")),
        name="dsa_prompt",
    )(qt, k, vt, qit, wit, ki)


def _tile(n, pref):
    return pref if n % pref == 0 else n


def kernel(x_prompt, x_sample, state_a_wkv, state_a_shift, cache_b_k, cache_b_v, cache_b_kidx, cache_c_k, cache_c_v, page_table, norm_mix, norm_ffn, norm_final, w_in_0, w_out_0, a_mu, a_w0, a_w2, a_a0, a_a2, a_g2, a_kk, a_ka, a_rk, a_ln_w, a_ln_b, ffn_wg, ffn_wu, ffn_wd, w_in_1, w_out_1, router_w, router_b, moe_wg, moe_wu, moe_wd):
    B, T, D = x_prompt.shape
    DB, S, _ = x_sample.shape
    assert S == 1 and D == D_MODEL
    past = page_table.shape[1] * PAGE_SIZE
    row = lambda a: a.reshape(1, -1).astype(F32)
    b16 = lambda a: a.astype(BF16)

    prm = _rwkv_params(dict(a_mu=a_mu, a_w0=a_w0, a_w2=a_w2, a_a0=a_a0, a_a2=a_a2, a_g2=a_g2, a_kk=a_kk, a_ka=a_ka,
                            a_rk=a_rk, a_ln_w=a_ln_w, a_ln_b=a_ln_b))
    w_in0 = _pad_w_in0(w_in_0)
    w_out0, w_in1, w_out1 = b16(w_out_0), b16(w_in_1), b16(w_out_1)
    f_wg, f_wu, f_wd = b16(ffn_wg), b16(ffn_wu), b16(ffn_wd)
    m_wg, m_wu, m_wd = b16(moe_wg), b16(moe_wu), b16(moe_wd)
    rw, rb = _pad_router(router_w, router_b)
    g_mix0, g_mix1 = row(norm_mix[0]), row(norm_mix[1])
    g_ffn0, g_ffn1, g_fin = row(norm_ffn[0]), row(norm_ffn[1]), row(norm_final)
    tf_ffn = D_FF // 2
    tf_moe = 512

    N = B * T
    cos_p, sin_p = _rope_tables(jnp.arange(T, dtype=I32))
    pa, kf, vf, kif, qt, kb, vt, qit, wit, kib = _inproj0(x_prompt, g_mix0, w_in0, cos_p, sin_p, _tile(T, 256), True)
    prep = _rwkv_prep(pa, jnp.zeros((B, A_COLS), F32), prm, _tile(T, 256), True)
    ya, p_a_wkv = _rwkv_scan(*prep, prm["lnw"], prm["lnb"], _tile(T, 256))
    yb = _dsa_prompt(qt, kb, vt, qit, wit, kib)
    h = _outproj_ffn(x_prompt.reshape(N, D), ya.reshape(N, D_A), yb.reshape(N, D_B), w_out0, g_ffn0, f_wg, f_wu, f_wd,
                     _tile(N, 512), tf_ffn)
    q1, k1, v1, k1f, v1f = _inproj1(h.reshape(B, T, D), g_mix1, w_in1, cos_p, sin_p, _tile(T, 256))
    outs, lses = [], []
    for window, dil in C_PATTERNS:
        o, lse = _dilated_branch(q1, k1, v1, window, dil)
        outs.append(o.reshape(N, D_C))
        lses.append(lse.reshape(N, LANES))
    h = _merge_outproj(h, outs, lses, w_out1, _tile(N, 512))
    if N % MOE_TR == 0:
        y_prompt = _moe_routed(h, g_ffn1, rw, rb, m_wg, m_wu, m_wd, g_fin, tf_moe).reshape(B, T, D)
    else:
        y_prompt = _moe_dense(h, g_ffn1, rw, rb, m_wg, m_wu, m_wd, g_fin, N, tf_moe).reshape(B, T, D)
    keep = min(C_WINDOW_MAX, T)
    prompt_state = (p_a_wkv, pa[:, -1], kf.reshape(B, T, H_B, HEAD_DIM), vf.reshape(B, T, H_B, HEAD_DIM), kif,
                    k1f[:, -keep:].reshape(B, keep, H_C, HEAD_DIM), v1f[:, -keep:].reshape(B, keep, H_C, HEAD_DIM))

    cos_s, sin_s = _rope_tables(jnp.full((DB,), past, I32))
    xs = x_sample.reshape(1, DB, D)
    pa, kf, vf, kif, q, qi, tail = _inproj0(xs, g_mix0, w_in0, cos_s, sin_s, DB, False)
    prep = _rwkv_prep(pa, state_a_shift.astype(F32), prm, DB, False)
    ya, s_a_wkv = _rwkv_step(state_a_wkv.astype(F32), *(a.reshape(DB, D_A) for a in prep), prm["lnw"], prm["lnb"])
    pad8 = lambda a: jnp.concatenate([a, jnp.zeros_like(a)], axis=1)
    qi16 = pad8(qi.reshape(DB, H_IDX, D_IDX))
    wi16 = pad8((tail[0, :, D_IDX:D_IDX + H_IDX] * IDX_SCALE).reshape(DB, H_IDX, 1))
    ki16 = jnp.broadcast_to(kif.reshape(DB, 1, D_IDX), (DB, 16, D_IDX))
    sel = _dsa_sample_select(page_table, qi16, wi16, ki16, cache_b_kidx)
    per_seq = lambda a: a.astype(F32).reshape(DB, 1, -1)
    heads = lambda a: a.astype(F32).reshape(DB, H_B, HEAD_DIM)
    yb = _dsa_sample_attend(page_table, sel, heads(q), heads(kf), heads(vf), cache_b_k.astype(F32), cache_b_v.astype(F32))
    hs = _outproj_ffn(x_sample.reshape(DB, D), ya, yb.reshape(DB, D_B), w_out0, g_ffn0, f_wg, f_wu, f_wd, DB, tf_ffn)
    q1, k1, v1, k1f, v1f = _inproj1(hs.reshape(1, DB, D), g_mix1, w_in1, cos_s, sin_s, DB)
    yc = _dilated_sample(per_seq(q1), per_seq(k1f), per_seq(v1f), cache_c_k, cache_c_v)
    hs = _outproj(hs, yc.reshape(DB, D_C), w_out1, DB)
    y_sample = _moe_dense(hs, g_ffn1, rw, rb, m_wg, m_wu, m_wd, g_fin, DB, tf_moe).reshape(DB, 1, D)
    keep = min(C_WINDOW_MAX, cache_c_k.shape[1] + 1)
    s_c_k = jnp.concatenate([cache_c_k, k1f.reshape(DB, 1, H_C, HEAD_DIM)], axis=1)[:, -keep:]
    s_c_v = jnp.concatenate([cache_c_v, v1f.reshape(DB, 1, H_C, HEAD_DIM)], axis=1)[:, -keep:]
    sample_state = (s_a_wkv, pa[0], kf.reshape(DB, 1, H_B, HEAD_DIM), vf.reshape(DB, 1, H_B, HEAD_DIM),
                    kif.reshape(DB, 1, D_IDX), s_c_k, s_c_v)
    return (y_prompt, y_sample) + prompt_state + sample_state
```

```python
import functools

import numpy as np
import jax
import jax.numpy as jnp
from jax import lax
from jax.experimental import pallas as pl
from jax.experimental.pallas import tpu as pltpu

F32 = jnp.float32
BF16 = jnp.bfloat16
I32 = jnp.int32

D_MODEL = 1024
HEAD_DIM = 64
ROPE_THETA = 10000.0
NORM_EPS = 1e-6
PAGE_SIZE = 128

H_A = 8
D_A = H_A * HEAD_DIM
DECAY_LORA = 64
AAA_LORA = 64
GATE_LORA = 128
A_COLS = 3 * D_A + DECAY_LORA + AAA_LORA + GATE_LORA
GN_EPS = 64e-5

H_B = 8
D_B = H_B * HEAD_DIM
H_IDX = 8
D_IDX = 64
IDX_TOPK_MAX = 256
IDX_SCALE = (H_IDX ** -0.5) * (D_IDX ** -0.5)
B_COLS_PAD = 3 * D_B + H_IDX * D_IDX + 128

H_C = 16
D_C = H_C * HEAD_DIM
C_PATTERNS = ((128, 1), (512, 4), (2048, 16))
C_WINDOW_MAX = 2048
C_BLOCK = 128

D_FF = 2816
N_EXPERTS = 8
D_FF_EXPERT = 3584

LANES = 128
VMEM_LIMIT = 56 << 20
INT_MIN = -(2 ** 31)
NEG_BIG = -1e30
LOG2E = 1.4426950408889634


def _cparams(sem, vmem=VMEM_LIMIT):
    return pltpu.CompilerParams(dimension_semantics=sem, vmem_limit_bytes=vmem)


def _dot(a, b):
    return jnp.dot(a.astype(BF16), b.astype(BF16), preferred_element_type=F32)


def _dot_nt(a, b):
    return lax.dot_general(a.astype(BF16), b.astype(BF16), (((1,), (1,)), ((), ())), preferred_element_type=F32)


def _split(x):
    hi = x.astype(BF16)
    lo = (x - hi.astype(F32)).astype(BF16)
    return hi, lo


def _dot_split_lhs(a, b_exact):
    hi, lo = _split(a)
    return jnp.dot(hi, b_exact, preferred_element_type=F32) + jnp.dot(lo, b_exact, preferred_element_type=F32)


def _dot_split_rhs(a_exact, b):
    hi, lo = _split(b)
    return jnp.dot(a_exact, hi, preferred_element_type=F32) + jnp.dot(a_exact, lo, preferred_element_type=F32)


def _dot3(a, b):
    ah, al = _split(a)
    bh, bl = _split(b)
    return (jnp.dot(ah, bh, preferred_element_type=F32) + jnp.dot(ah, bl, preferred_element_type=F32)
            + jnp.dot(al, bh, preferred_element_type=F32))


def _iota(shape, axis):
    return lax.broadcasted_iota(I32, shape, axis)


def _rope_tables(pos):
    half = HEAD_DIM // 2
    inv_freq = jnp.power(ROPE_THETA, -jnp.arange(half, dtype=F32) / half)
    ang = pos.astype(F32)[:, None] * inv_freq[None, :]
    cos = jnp.cos(ang)
    sin = jnp.sin(ang)
    return jnp.tile(cos, (1, 4)), jnp.tile(jnp.concatenate([-sin, sin], axis=1), (1, 2))


def _rope(x, cos, sin):
    w = x.shape[-1]
    reps = w // LANES
    if reps > 1:
        cos = jnp.concatenate([cos] * reps, axis=1)
        sin = jnp.concatenate([sin] * reps, axis=1)
    first_half = (_iota(x.shape, 1) % HEAD_DIM) < (HEAD_DIM // 2)
    swapped = jnp.where(first_half, pltpu.roll(x, w - HEAD_DIM // 2, 1), pltpu.roll(x, HEAD_DIM // 2, 1))
    return x * cos + swapped * sin


def _rms(x, g):
    ms = jnp.mean(x * x, axis=-1, keepdims=True)
    return x * lax.rsqrt(ms + NORM_EPS) * g


def _inproj0_kernel(for_prompt, x_ref, g_ref, w_ref, cos_ref, sin_ref, pa_ref, kf_ref, vf_ref, kif_ref, *outs):
    h = _rms(x_ref[...], g_ref[...]).astype(BF16)
    p = jnp.dot(h, w_ref[...], preferred_element_type=F32)
    cos = cos_ref[...]
    sin = sin_ref[...]
    o = A_COLS
    pa_ref[...] = p[:, :o]
    q = _rope(p[:, o:o + D_B], cos, sin) * (HEAD_DIM ** -0.5)
    k = _rope(p[:, o + D_B:o + 2 * D_B], cos, sin)
    v = p[:, o + 2 * D_B:o + 3 * D_B]
    qi = _rope(p[:, o + 3 * D_B:o + 4 * D_B], cos, sin)
    tail = p[:, o + 4 * D_B:]
    ki = _rope(tail, cos, sin)[:, :D_IDX]
    if for_prompt:
        qt_ref, k_ref, vt_ref, qit_ref, wit_ref, ki_ref = outs
        v_t = v.T
        kf_ref[...] = k.T
        vf_ref[...] = v_t
        kif_ref[...] = ki.T
        qt_ref[...] = (q * LOG2E).T.astype(BF16)
        k_ref[...] = k.astype(BF16)
        vt_ref[...] = v_t.astype(BF16)
        qit_ref[...] = qi.T.astype(BF16)
        wit_ref[...] = tail.T[D_IDX:D_IDX + H_IDX, :] * IDX_SCALE
        ki_ref[...] = ki.astype(BF16)
    else:
        q_ref, qi_ref, tail_ref = outs
        kf_ref[...] = k
        vf_ref[...] = v
        kif_ref[...] = ki
        q_ref[...] = q.astype(BF16)
        qi_ref[...] = qi.astype(BF16)
        tail_ref[...] = tail


def _inproj0(x, g, w_pad, cos, sin, tm, for_prompt):
    B, T, _ = x.shape
    tok = lambda width: pl.BlockSpec((None, tm, width), lambda b, t: (b, t, 0))
    tr = lambda rows: pl.BlockSpec((None, rows, tm), lambda b, t: (b, 0, t))
    full = lambda a: pl.BlockSpec(a.shape, lambda b, t: (0,) * a.ndim)
    tab = pl.BlockSpec((tm, LANES), lambda b, t: (t, 0))
    sds = lambda shape, dt: jax.ShapeDtypeStruct(shape, dt)
    if for_prompt:
        specs = [tok(A_COLS), tr(D_B), tr(D_B), tr(D_IDX), tr(D_B), tok(D_B), tr(D_B), tr(D_B), tr(H_IDX), tok(D_IDX)]
        shapes = [sds((B, T, A_COLS), F32), sds((B, D_B, T), F32), sds((B, D_B, T), F32), sds((B, D_IDX, T), F32),
                  sds((B, D_B, T), BF16), sds((B, T, D_B), BF16), sds((B, D_B, T), BF16), sds((B, D_B, T), BF16),
                  sds((B, H_IDX, T), F32), sds((B, T, D_IDX), BF16)]
    else:
        specs = [tok(A_COLS), tok(D_B), tok(D_B), tok(D_IDX), tok(D_B), tok(D_B), tok(LANES)]
        shapes = [sds((B, T, A_COLS), F32), sds((B, T, D_B), F32), sds((B, T, D_B), F32), sds((B, T, D_IDX), F32),
                  sds((B, T, D_B), BF16), sds((B, T, D_B), BF16), sds((B, T, LANES), F32)]
    return pl.pallas_call(
        functools.partial(_inproj0_kernel, for_prompt),
        grid=(B, T // tm),
        in_specs=[tok(D_MODEL), full(g), full(w_pad), tab, tab],
        out_specs=specs,
        out_shape=shapes,
        compiler_params=_cparams(("parallel", "arbitrary")),
        name="inproj0",
    )(x, g, w_pad, cos, sin)


def _seg_sum(x, bd):
    return _dot_split_lhs(x, bd)


def _rwkv_prep_kernel(seq_mode, p_ref, prev_ref, shift_ref, mu_ref, w0_ref, w2_ref, a0_ref, a2_ref, g2_ref,
                      kk_ref, ka_ref, rk_ref, bd_ref,
                      r_out, ld_out, k_out, v_out, kkn_out, ab_out, g_out, bonus_out):
    p = p_ref[...]
    if seq_mode:
        last = jnp.where(pl.program_id(1) == 0, shift_ref[...], prev_ref[7:8, :])
        prev = jnp.where(_iota(p.shape, 0) == 0, last, pltpu.roll(p, 1, 0))
    else:
        prev = prev_ref[...]
    xm = p + (prev - p) * mu_ref[...]
    r = xm[:, :D_A]
    k = xm[:, D_A:2 * D_A]
    v = xm[:, 2 * D_A:3 * D_A]
    wa = xm[:, 3 * D_A:3 * D_A + LANES]
    gl = xm[:, 3 * D_A + LANES:]
    z = -(w0_ref[...] + _dot(jnp.tanh(wa), w2_ref[...]))
    softplus = jnp.maximum(z, 0.0) + jnp.log(1.0 + jnp.exp(-jnp.abs(z)))
    ld_out[...] = -jnp.exp(-softplus - 0.5)
    a = jax.nn.sigmoid(a0_ref[...] + _dot(wa, a2_ref[...]))
    g_out[...] = _dot(jax.nn.sigmoid(gl), g2_ref[...])
    bd = bd_ref[...]
    kk = k * kk_ref[...]
    kkn = kk * lax.rsqrt(jnp.maximum(_seg_sum(kk * kk, bd), 1e-24))
    k2 = k * (1.0 + (a - 1.0) * ka_ref[...])
    r_out[...] = r
    k_out[...] = k2
    v_out[...] = v
    kkn_out[...] = kkn
    ab_out[...] = kkn * a
    bonus_out[...] = _seg_sum(r * k2 * rk_ref[...], bd) * v


def _block_diag_ones(n, seg=HEAD_DIM):
    i = np.arange(n)
    return jnp.asarray((i[:, None] // seg) == (i[None, :] // seg), BF16)


def _rwkv_prep(pa, shift_prev, prm, tm, seq_mode):
    B, T, _ = pa.shape
    nt = T // tm
    tok = lambda width: pl.BlockSpec((None, tm, width), lambda b, t: (b, t, 0))
    full = lambda a: pl.BlockSpec(a.shape, lambda b, t: (0,) * a.ndim)
    if seq_mode:
        prev_spec = pl.BlockSpec((None, 8, A_COLS), lambda b, t: (b, jnp.maximum(t * (tm // 8) - 1, 0), 0))
        prev_arr = pa
        shift_arr = shift_prev.reshape(B, 1, A_COLS)
        shift_spec = pl.BlockSpec((None, 1, A_COLS), lambda b, t: (b, 0, 0))
    else:
        prev_spec = tok(A_COLS)
        prev_arr = shift_prev.reshape(1, T, A_COLS)
        shift_arr = jnp.zeros((1, 1, A_COLS), F32)
        shift_spec = pl.BlockSpec((None, 1, A_COLS), lambda b, t: (0, 0, 0))
    params = [prm[n] for n in ("mu", "w0", "w2", "a0", "a2", "g2", "kk", "ka", "rk", "bd")]
    out = jax.ShapeDtypeStruct((B, T, D_A), F32)
    return pl.pallas_call(
        functools.partial(_rwkv_prep_kernel, seq_mode),
        grid=(B, nt),
        in_specs=[tok(A_COLS), prev_spec, shift_spec] + [full(a) for a in params],
        out_specs=[tok(D_A)] * 8,
        out_shape=[out] * 8,
        compiler_params=_cparams(("parallel", "arbitrary")),
        name="rwkv_prep",
    )(pa, prev_arr, shift_arr, *params)


RWKV_CHUNK = 64
RWKV_GROUP = 4
RWKV_W = RWKV_GROUP * HEAD_DIM


def _rwkv_chunk(r, ld, k, v, kkn, ab, h, tri, same_head, strict, incl, eye):
    cum = _dot_split_rhs(tri, ld)
    cum_end = cum[RWKV_CHUNK - 1:RWKV_CHUNK, :]
    e_in = jnp.exp(cum)
    e_ex = jnp.exp(cum - ld)
    e_neg = jnp.exp(-cum)
    e_rem = jnp.exp(cum_end - cum)

    def expand(x):
        return jnp.where(same_head, jnp.concatenate([x] * RWKV_GROUP, axis=0), 0.0).astype(BF16)

    a_t = expand(-(kkn * e_ex))
    r_t = expand(r * e_in)
    b_t = expand(ab * e_neg)
    k_t = expand(k * e_neg)
    v_e = expand(v)
    gram = _dot_nt(jnp.concatenate([a_t, r_t], axis=0), jnp.concatenate([b_t, k_t], axis=0))
    w = RWKV_W
    l_ab = jnp.where(strict, gram[:w, :w], 0.0)
    a_ak = jnp.where(strict, gram[:w, w:], 0.0)
    a_rb = jnp.where(incl, gram[w:, :w], 0.0)
    a_rk = jnp.where(incl, gram[w:, w:], 0.0)
    pinv = jnp.where(eye, 1.0, 0.0) + l_ab
    qpow = l_ab
    for _ in range(5):
        qpow = _dot(qpow, qpow)
        pinv = pinv + _dot(qpow, pinv)
    x0 = _dot(a_ak, v_e)
    wu = _dot(pinv, jnp.concatenate([a_t, x0.astype(BF16)], axis=1))
    u_e = _dot(wu[:, :w], h) + wu[:, w:]
    uv = jnp.concatenate([u_e.astype(BF16), v_e], axis=0)
    y_e = _dot(r_t, h) + _dot(jnp.concatenate([a_rb, a_rk], axis=1), uv)
    y = y_e[:RWKV_CHUNK]
    for i in range(1, RWKV_GROUP):
        y = y + y_e[i * RWKV_CHUNK:(i + 1) * RWKV_CHUNK]
    bk_t = jnp.concatenate([jnp.where(same_head, jnp.concatenate([ab * e_rem] * RWKV_GROUP, axis=0), 0.0).T,
                            jnp.where(same_head, jnp.concatenate([k * e_rem] * RWKV_GROUP, axis=0), 0.0).T], axis=1)
    g_col = jnp.exp(jnp.broadcast_to(cum_end, (8, w))).T[:, :1]
    h_new = h * g_col + _dot(bk_t, uv)
    return y, h_new


def _rwkv_scan_kernel(n_chunks, r_ref, ld_ref, k_ref, v_ref, kkn_ref, ab_ref, g_ref, bonus_ref, lnw_ref, lnb_ref,
                      y_ref, ht_ref, h_scr):
    @pl.when(pl.program_id(1) == 0)
    def _():
        h_scr[...] = jnp.zeros_like(h_scr)

    w = RWKV_W
    c = RWKV_CHUNK
    row = _iota((w, w), 0)
    col = _iota((w, w), 1)
    same_head = (row // c) == (col // HEAD_DIM)
    strict = (row % c) > (col % c)
    incl = (row % c) >= (col % c)
    eye = row == col
    tri = jnp.where(_iota((c, c), 0) >= _iota((c, c), 1), 1.0, 0.0).astype(BF16)
    seg_avg = jnp.where((row // HEAD_DIM) == (col // HEAD_DIM), 1.0 / HEAD_DIM, 0.0).astype(BF16)
    h = h_scr[...]
    for i in range(n_chunks):
        sl = slice(i * c, (i + 1) * c)
        y, h = _rwkv_chunk(r_ref[sl, :], ld_ref[sl, :], k_ref[sl, :], v_ref[sl, :], kkn_ref[sl, :], ab_ref[sl, :],
                           h, tri, same_head, strict, incl, eye)
        mean = _dot_split_lhs(y, seg_avg)
        yc = y - mean
        var = _dot_split_lhs(yc * yc, seg_avg)
        yn = yc * lax.rsqrt(var + GN_EPS) * lnw_ref[...] + lnb_ref[...]
        y_ref[sl, :] = ((yn + bonus_ref[sl, :]) * g_ref[sl, :]).astype(BF16)
    h_scr[...] = h

    @pl.when(pl.program_id(1) == pl.num_programs(1) - 1)
    def _():
        ht_ref[...] = h.T


def _rwkv_scan(r, ld, k, v, kkn, ab, g, bonus, lnw, lnb, tb):
    B, T, _ = r.shape
    ng = D_A // RWKV_W
    blk = pl.BlockSpec((None, tb, RWKV_W), lambda c, t: (c // ng, t, c % ng))
    par = pl.BlockSpec((1, RWKV_W), lambda c, t: (0, c % ng))
    y, ht = pl.pallas_call(
        functools.partial(_rwkv_scan_kernel, tb // RWKV_CHUNK),
        grid=(B * ng, T // tb),
        in_specs=[blk] * 8 + [par, par],
        out_specs=[blk, pl.BlockSpec((None, RWKV_W, RWKV_W), lambda c, t: (c, 0, 0))],
        out_shape=[jax.ShapeDtypeStruct((B, T, D_A), BF16), jax.ShapeDtypeStruct((B * ng, RWKV_W, RWKV_W), F32)],
        scratch_shapes=[pltpu.VMEM((RWKV_W, RWKV_W), F32)],
        compiler_params=_cparams(("parallel", "arbitrary")),
        name="rwkv_scan",
    )(r, ld, k, v, kkn, ab, g, bonus, lnw, lnb)
    ht = ht.reshape(B, ng, RWKV_GROUP, HEAD_DIM, RWKV_GROUP, HEAD_DIM)
    idx = jnp.arange(RWKV_GROUP)
    wkv = ht[:, :, idx, :, idx, :]
    return y, jnp.moveaxis(wkv, 0, 2).reshape(B, H_A, HEAD_DIM, HEAD_DIM)


def _rwkv_step_kernel(s_ref, r_ref, ld_ref, k_ref, v_ref, kkn_ref, ab_ref, g_ref, bonus_ref, lnw_ref, lnb_ref,
                      y_ref, s_out):
    rows = H_A * HEAD_DIM
    pad = 16
    rep = jnp.where((_iota((rows, pad), 0) // HEAD_DIM) == _iota((rows, pad), 1), 1.0, 0.0).astype(BF16)
    rep_t = jnp.where((_iota((pad, rows), 1) // HEAD_DIM) == _iota((pad, rows), 0), 1.0, 0.0).astype(BF16)
    zeros8 = jnp.zeros((pad - H_A, HEAD_DIM), F32)
    spread = lambda x8: _dot_split_rhs(rep, jnp.concatenate([x8, zeros8], axis=0))
    diag = (_iota((rows, HEAD_DIM), 0) % HEAD_DIM) == _iota((rows, HEAD_DIM), 1)
    s = s_ref[...].reshape(rows, HEAD_DIM)
    a_rep = -spread(kkn_ref[...])
    sa = jnp.sum(s * a_rep, axis=1, keepdims=True)
    v_col = jnp.sum(jnp.where(diag, spread(v_ref[...]), 0.0), axis=1, keepdims=True)
    s_new = s * jnp.exp(spread(ld_ref[...])) + sa * spread(ab_ref[...]) + v_col * spread(k_ref[...])
    s_out[...] = s_new.reshape(H_A, HEAD_DIM, HEAD_DIM)
    y_col = jnp.sum(s_new * spread(r_ref[...]), axis=1, keepdims=True)
    y = _dot_split_rhs(rep_t, jnp.where(diag, y_col, 0.0))[:H_A]
    mean = jnp.mean(y, axis=1, keepdims=True)
    yc = y - mean
    var = jnp.mean(yc * yc, axis=1, keepdims=True)
    yn = yc * lax.rsqrt(var + GN_EPS) * lnw_ref[...] + lnb_ref[...]
    y_ref[...] = ((yn + bonus_ref[...]) * g_ref[...]).astype(BF16)


def _rwkv_step(state, r, ld, k, v, kkn, ab, g, bonus, lnw, lnb):
    DB = state.shape[0]
    heads = lambda a: a.reshape(DB, H_A, HEAD_DIM)
    vec = pl.BlockSpec((None, H_A, HEAD_DIM), lambda b: (b, 0, 0))
    par = pl.BlockSpec((H_A, HEAD_DIM), lambda b: (0, 0))
    st = pl.BlockSpec((None, H_A, HEAD_DIM, HEAD_DIM), lambda b: (b, 0, 0, 0))
    y, s_new = pl.pallas_call(
        _rwkv_step_kernel,
        grid=(DB,),
        in_specs=[st] + [vec] * 8 + [par, par],
        out_specs=[vec, st],
        out_shape=[jax.ShapeDtypeStruct((DB, H_A, HEAD_DIM), BF16), jax.ShapeDtypeStruct(state.shape, F32)],
        compiler_params=_cparams(("parallel",)),
        name="rwkv_step",
    )(state, *(heads(a) for a in (r, ld, k, v, kkn, ab, g, bonus)), lnw.reshape(H_A, HEAD_DIM), lnb.reshape(H_A, HEAD_DIM))
    return y.reshape(DB, D_A), s_new


DSA_PG = 8


def _dsa_sample_select_kernel(topk, n_groups, idx_bits, pt_ref, qi_ref, w_ref, kin_ref, *rest):
    pages = rest[:DSA_PG]
    bias_ref, bias_new_ref, key_scr = rest[DSA_PG:]
    g = pl.program_id(1)
    qi8 = qi_ref[...]
    w8 = w_ref[...]
    allp = jnp.concatenate([p[...] for p in pages], axis=1)
    res = _dot(qi8, allp)
    score = jnp.sum(jnp.maximum(res, 0.0) * w8, axis=0, keepdims=True)
    key = _order_key(score)
    for i in range(DSA_PG):
        key_scr[pl.ds(g * DSA_PG + i, 1), :] = key[:, i * PAGE_SIZE:(i + 1) * PAGE_SIZE]

    @pl.when(g == n_groups - 1)
    def _():
        n_pages = n_groups * DSA_PG
        s_new = jnp.sum(jnp.maximum(_dot_nt(qi8, kin_ref[...]), 0.0) * w8, axis=0, keepdims=True)[:, :1]
        key_new = _order_key(s_new)
        keys = key_scr[...]
        idx = _iota(keys.shape, 0) * PAGE_SIZE + _iota(keys.shape, 1)
        idx_new = n_pages * PAGE_SIZE
        total = lambda m: jnp.sum(jnp.sum(m, axis=1, keepdims=True), axis=0, keepdims=True)
        count_ge = lambda c: total(jnp.where(keys >= c, 1.0, 0.0)) + jnp.where(key_new >= c, 1.0, 0.0)
        thr = _kth_largest_key(count_ge, topk, (1, 1))
        n_gt = total(jnp.where(keys > thr, 1.0, 0.0)) + jnp.where(key_new > thr, 1.0, 0.0)
        need = float(topk) - n_gt

        def bit_body(i, j):
            cand = j + lax.shift_left(jnp.int32(1), idx_bits - 1 - i)
            ties = (total(jnp.where(keys == thr, jnp.where(idx <= cand, 1.0, 0.0), 0.0))
                    + jnp.where(key_new == thr, jnp.where(idx_new <= cand, 1.0, 0.0), 0.0))
            return jnp.where(ties < need, cand, j)

        idx_thr = lax.fori_loop(0, idx_bits, bit_body, jnp.full((1, 1), -1, I32)) + 1
        sel = lambda kk, ii: jnp.where(kk > thr, 0.0, jnp.where(kk == thr, jnp.where(ii <= idx_thr, 0.0, NEG_BIG), NEG_BIG))
        bias_ref[...] = sel(keys, idx)
        bias_new_ref[...] = jnp.where(_iota(bias_new_ref.shape, 1) == 0, sel(key_new, idx_new), NEG_BIG)


def _dsa_sample_select(page_table, qi, wi, ki_new, kidx_t):
    DB, n_pages = page_table.shape
    n_groups = n_pages // DSA_PG
    L = n_pages * PAGE_SIZE + 1
    topk = min(IDX_TOPK_MAX, L // 4)
    idx_bits = int(np.ceil(np.log2(L))) + 1
    per_seq = lambda a: pl.BlockSpec((None,) + a.shape[1:], lambda b, g, pt: (b,) + (0,) * (a.ndim - 1))
    page = lambda i: pl.BlockSpec((None, D_IDX, PAGE_SIZE), lambda b, g, pt: (pt[b, g * DSA_PG + i], 0, 0))
    grid_spec = pltpu.PrefetchScalarGridSpec(
        num_scalar_prefetch=1,
        grid=(DB, n_groups),
        in_specs=[per_seq(qi), per_seq(wi), per_seq(ki_new)] + [page(i) for i in range(DSA_PG)],
        out_specs=[pl.BlockSpec((None, n_pages, PAGE_SIZE), lambda b, g, pt: (b, 0, 0)),
                   pl.BlockSpec((None, 8, LANES), lambda b, g, pt: (b, 0, 0))],
        scratch_shapes=[pltpu.VMEM((n_pages, PAGE_SIZE), I32)],
    )
    return pl.pallas_call(
        functools.partial(_dsa_sample_select_kernel, topk, n_groups, idx_bits),
        grid_spec=grid_spec,
        out_shape=[jax.ShapeDtypeStruct((DB, n_pages, PAGE_SIZE), F32), jax.ShapeDtypeStruct((DB, 8, LANES), F32)],
        compiler_params=_cparams(("parallel", "arbitrary")),
        name="dsa_sample_select",
    )(page_table, qi, wi, ki_new, *([kidx_t] * DSA_PG))


def _keys_on_lanes(cache):
    return jnp.moveaxis(cache, -3, -1)


def _on_lanes(x):
    return jnp.broadcast_to(x.astype(F32)[..., None], x.shape + (LANES,))


def _lane0(x):
    return jnp.pad(x.astype(F32)[..., None], ((0, 0), (0, 0), (0, 0), (0, LANES - 1)))


def _lane_attend_scratch(n_heads, width):
    return [pltpu.VMEM((n_heads, LANES), F32), pltpu.VMEM((n_heads, LANES), F32),
            pltpu.VMEM((n_heads, HEAD_DIM, LANES), F32), pltpu.VMEM((n_heads, width), F32),
            pltpu.VMEM((n_heads, LANES), F32)]


def _lane_attend_init(m_scr, l_scr, acc_scr, p_scr, a_scr):
    m_scr[...] = jnp.full(m_scr.shape, NEG_BIG, F32)
    l_scr[...] = jnp.zeros_like(l_scr)
    acc_scr[...] = jnp.zeros_like(acc_scr)


def _lane_attend(kt_ref, vt_ref, qb_ref, bias, m_scr, l_scr, acc_scr, p_scr, a_scr):
    n_heads, _, width = kt_ref.shape
    groups = width // LANES
    for h in range(n_heads):
        q = qb_ref[h]
        for g in range(groups):
            lanes = slice(g * LANES, (g + 1) * LANES)
            p_scr[h:h + 1, lanes] = jnp.sum(kt_ref[h, :, lanes] * q, axis=0, keepdims=True)
    s = p_scr[:, :width] + bias
    m_old = m_scr[...]
    m_new = jnp.maximum(m_old, jnp.max(s, axis=1, keepdims=True))
    alpha = jnp.exp(m_old - m_new)
    p = jnp.exp(s - (jnp.concatenate([m_new] * groups, axis=1) if groups > 1 else m_new))
    l_scr[...] = alpha * l_scr[...] + jnp.sum(p, axis=1, keepdims=True)
    m_scr[...] = m_new
    p_scr[:, :width] = p
    a_scr[...] = alpha
    for h in range(n_heads):
        acc = acc_scr[h] * a_scr[h:h + 1, :]
        for g in range(groups):
            lanes = slice(g * LANES, (g + 1) * LANES)
            acc = acc + p_scr[h:h + 1, lanes] * vt_ref[h, :, lanes]
        acc_scr[h] = acc


def _lane_attend_finish(o_ref, m_scr, l_scr, acc_scr, p_scr, a_scr):
    a_scr[...] = 1.0 / l_scr[...]
    for h in range(o_ref.shape[0]):
        o_ref[h] = jnp.sum(acc_scr[h] * a_scr[h:h + 1, :], axis=1, keepdims=True)


def _dsa_sample_attend_kernel(n_groups, pt_ref, qb_ref, kn_ref, vn_ref, bias_ref, bias_new_ref, *rest):
    kpages = rest[:DSA_PG]
    vpages = rest[DSA_PG:2 * DSA_PG]
    o_ref = rest[2 * DSA_PG]
    state = rest[2 * DSA_PG + 1:]
    g = pl.program_id(1)

    @pl.when(g == 0)
    def _():
        _lane_attend_init(*state)

    for i in range(DSA_PG):
        _lane_attend(kpages[i], vpages[i], qb_ref, bias_ref[i:i + 1, :], *state)

    @pl.when(g == n_groups - 1)
    def _():
        _lane_attend(kn_ref, vn_ref, qb_ref, bias_new_ref[0:1, :], *state)
        _lane_attend_finish(o_ref, *state)


def _dsa_sample_attend(page_table, q, k_new, v_new, bias, bias_new, cache_k, cache_v):
    DB, n_pages = page_table.shape
    n_groups = n_pages // DSA_PG
    per_seq = pl.BlockSpec((None, H_B, HEAD_DIM, LANES), lambda b, g, pt: (b, 0, 0, 0))
    page = lambda i: pl.BlockSpec((None, H_B, HEAD_DIM, PAGE_SIZE), lambda b, g, pt: (pt[b, g * DSA_PG + i], 0, 0, 0))
    grid_spec = pltpu.PrefetchScalarGridSpec(
        num_scalar_prefetch=1,
        grid=(DB, n_groups),
        in_specs=[per_seq, per_seq, per_seq,
                  pl.BlockSpec((None, DSA_PG, PAGE_SIZE), lambda b, g, pt: (b, g, 0)),
                  pl.BlockSpec((None, 8, LANES), lambda b, g, pt: (b, 0, 0))] + [page(i) for i in range(DSA_PG)] * 2,
        out_specs=pl.BlockSpec((None, H_B, HEAD_DIM, 1), lambda b, g, pt: (b, 0, 0, 0)),
        scratch_shapes=_lane_attend_scratch(H_B, PAGE_SIZE),
    )
    ck, cv = _keys_on_lanes(cache_k), _keys_on_lanes(cache_v)
    return pl.pallas_call(
        functools.partial(_dsa_sample_attend_kernel, n_groups),
        grid_spec=grid_spec,
        out_shape=jax.ShapeDtypeStruct((DB, H_B, HEAD_DIM, 1), F32),
        compiler_params=_cparams(("parallel", "arbitrary")),
        name="dsa_sample_attend",
    )(page_table, _on_lanes(q), _lane0(k_new), _lane0(v_new), bias, bias_new, *([ck] * DSA_PG), *([cv] * DSA_PG))


DIL_CHUNK = 512


def _dilated_sample_kernel(w_len, qb_ref, kn_ref, vn_ref, kc_ref, vc_ref, o_ref, *state):
    c = pl.program_id(1)

    @pl.when(c == 0)
    def _():
        _lane_attend_init(*state)

    width = kc_ref.shape[2]
    dist = w_len - (c * width + _iota((1, width), 1))
    count = jnp.zeros((1, width), F32)
    for window, dil in C_PATTERNS:
        count = count + jnp.where(dist <= window, jnp.where(dist % dil == 0, 1.0, 0.0), 0.0)
    bias = jnp.where(count > 0.0, jnp.log(jnp.maximum(count, 1.0)), NEG_BIG)
    _lane_attend(kc_ref, vc_ref, qb_ref, bias, *state)

    @pl.when(c == pl.num_programs(1) - 1)
    def _():
        bias_new = jnp.where(_iota((1, LANES), 1) == 0, float(np.log(len(C_PATTERNS))), NEG_BIG)
        _lane_attend(kn_ref, vn_ref, qb_ref, bias_new, *state)
        _lane_attend_finish(o_ref, *state)


def _dilated_sample(q, k_new, v_new, cache_k, cache_v):
    DB, w_len = cache_k.shape[:2]
    width = min(DIL_CHUNK, w_len)
    per_seq = pl.BlockSpec((None, H_C, HEAD_DIM, LANES), lambda b, c: (b, 0, 0, 0))
    chunk = pl.BlockSpec((None, H_C, HEAD_DIM, width), lambda b, c: (b, 0, 0, c))
    return pl.pallas_call(
        functools.partial(_dilated_sample_kernel, w_len),
        grid=(DB, w_len // width),
        in_specs=[per_seq, per_seq, per_seq, chunk, chunk],
        out_specs=pl.BlockSpec((None, H_C, HEAD_DIM, 1), lambda b, c: (b, 0, 0, 0)),
        out_shape=jax.ShapeDtypeStruct((DB, H_C, HEAD_DIM, 1), F32),
        scratch_shapes=_lane_attend_scratch(H_C, width),
        compiler_params=_cparams(("parallel", "arbitrary")),
        name="dilated_sample",
    )(_on_lanes(q), _lane0(k_new), _lane0(v_new), _keys_on_lanes(cache_k), _keys_on_lanes(cache_v))


def _silu(x):
    return x * jax.nn.sigmoid(x)


def _outproj_ffn_kernel(x_ref, ya_ref, yb_ref, wo_ref, g_ref, wg_ref, wu_ref, wd_ref, o_ref, h_scr, hn_scr, acc_scr):
    j = pl.program_id(1)

    @pl.when(j == 0)
    def _():
        y = jnp.concatenate([ya_ref[...], yb_ref[...]], axis=1)
        h = x_ref[...] + jnp.dot(y, wo_ref[...], preferred_element_type=F32)
        h_scr[...] = h
        hn_scr[...] = _rms(h, g_ref[...]).astype(BF16)
        acc_scr[...] = jnp.zeros_like(acc_scr)

    hn = hn_scr[...]
    act = _silu(jnp.dot(hn, wg_ref[...], preferred_element_type=F32)) * jnp.dot(hn, wu_ref[...], preferred_element_type=F32)
    acc_scr[...] += jnp.dot(act.astype(BF16), wd_ref[...], preferred_element_type=F32)

    @pl.when(j == pl.num_programs(1) - 1)
    def _():
        o_ref[...] = h_scr[...] + acc_scr[...]


def _outproj_ffn(x, ya, yb, wo, g, wg, wu, wd, tm, tf):
    N = x.shape[0]
    nf = D_FF // tf
    tok = lambda width: pl.BlockSpec((tm, width), lambda i, j: (i, 0))
    full = lambda a: pl.BlockSpec(a.shape, lambda i, j: (0,) * a.ndim)
    return pl.pallas_call(
        _outproj_ffn_kernel,
        grid=(N // tm, nf),
        in_specs=[tok(D_MODEL), tok(D_A), tok(D_B), full(wo), full(g),
                  pl.BlockSpec((D_MODEL, tf), lambda i, j: (0, j)), pl.BlockSpec((D_MODEL, tf), lambda i, j: (0, j)),
                  pl.BlockSpec((tf, D_MODEL), lambda i, j: (j, 0))],
        out_specs=tok(D_MODEL),
        out_shape=jax.ShapeDtypeStruct((N, D_MODEL), F32),
        scratch_shapes=[pltpu.VMEM((tm, D_MODEL), F32), pltpu.VMEM((tm, D_MODEL), BF16), pltpu.VMEM((tm, D_MODEL), F32)],
        compiler_params=_cparams(("parallel", "arbitrary")),
        name="outproj_ffn",
    )(x, ya, yb, wo, g, wg, wu, wd)


def _inproj1_kernel(keep_tiles, x_ref, g_ref, w_ref, cos_ref, sin_ref, q_ref, k_ref, v_ref, kf_ref, vf_ref, *streams):
    h = _rms(x_ref[...], g_ref[...]).astype(BF16)
    p = jnp.dot(h, w_ref[...], preferred_element_type=F32)
    cos = cos_ref[...]
    sin = sin_ref[...]
    q = _rope(p[:, :D_C], cos, sin) * (HEAD_DIM ** -0.5)
    k = _rope(p[:, D_C:2 * D_C], cos, sin)
    v = p[:, 2 * D_C:]
    q_ref[...] = q.astype(BF16)
    k_ref[...] = k.astype(BF16)
    v_ref[...] = v.astype(BF16)
    if keep_tiles is None:
        kf_ref[...] = k
        vf_ref[...] = v
    else:
        scr = streams[-1]
        n_groups, tm, _ = scr.shape
        n_dil = len(C_PATTERNS) - 1
        for which, val in enumerate((q, k, v)):
            for g in range(n_groups):
                scr[g] = val[:, g * LANES:(g + 1) * LANES]
            for di, (_, dil) in enumerate(C_PATTERNS[1:]):
                out = streams[which * n_dil + di]
                for r in range(dil):
                    for g in range(n_groups):
                        out[:, r * D_C + g * LANES:r * D_C + (g + 1) * LANES] = (
                            scr[g, pl.ds(r, tm // dil, stride=dil), :].astype(BF16))

        @pl.when(pl.program_id(1) >= pl.num_programs(1) - keep_tiles)
        def _():
            kf_ref[...] = k.T
            vf_ref[...] = v.T


def _inproj1(x, g, w, cos, sin, tm, keep=None):
    B, T, _ = x.shape
    tok = pl.BlockSpec((None, tm, D_C), lambda b, t: (b, t, 0))
    full = lambda a: pl.BlockSpec(a.shape, lambda b, t: (0,) * a.ndim)
    tab = pl.BlockSpec((tm, LANES), lambda b, t: (t, 0))
    bf = jax.ShapeDtypeStruct((B, T, D_C), BF16)
    extra_specs, extra_shapes, scratch = [], [], []
    if keep is None:
        keep_tiles, f_spec, ff = None, tok, jax.ShapeDtypeStruct((B, T, D_C), F32)
    else:
        keep_tiles = keep // tm
        first = T // tm - keep_tiles
        f_spec = pl.BlockSpec((None, D_C, tm), lambda b, t: (b, 0, jnp.maximum(t - first, 0)))
        ff = jax.ShapeDtypeStruct((B, D_C, keep), F32)
        for _ in range(3):
            for _, dil in C_PATTERNS[1:]:
                extra_specs.append(pl.BlockSpec((None, tm // dil, dil * D_C), lambda b, t: (b, t, 0)))
                extra_shapes.append(jax.ShapeDtypeStruct((B, T // dil, dil * D_C), BF16))
        scratch = [pltpu.VMEM((D_C // LANES, tm, LANES), F32)]
    return pl.pallas_call(
        functools.partial(_inproj1_kernel, keep_tiles),
        grid=(B, T // tm),
        in_specs=[tok, full(g), full(w), tab, tab],
        out_specs=[tok, tok, tok, f_spec, f_spec] + extra_specs,
        out_shape=[bf, bf, bf, ff, ff] + extra_shapes,
        scratch_shapes=scratch,
        compiler_params=_cparams(("parallel", "arbitrary")),
        name="inproj1",
    )(x, g, w, cos, sin)


def _dilated_kernel(lookback, q_ref, kc_ref, kp_ref, vc_ref, vp_ref, o_ref, lse_ref):
    QB = C_BLOCK
    c = pl.program_id(2)
    lane128 = _iota((QB, LANES), 1)
    upper = (lane128 // HEAD_DIM) == 1
    upper_v = (_iota((2 * QB, LANES), 1) // HEAD_DIM) == 1
    qi = _iota((QB, 2 * QB), 0)
    kj = _iota((QB, 2 * QB), 1)
    dist = qi + QB - kj
    ok = (dist >= 0) & (dist <= lookback) & ((kj >= QB) | (c > 0))
    bias = jnp.where(ok, 0.0, NEG_BIG)
    q = q_ref[...]
    lse_blk = jnp.zeros((QB, LANES), F32)
    outs = []
    for j in range(H_C // 2):
        sl = slice(j * LANES, (j + 1) * LANES)
        pair = q[:, sl]
        q2 = jnp.concatenate([jnp.where(upper, jnp.zeros_like(pair), pair),
                              jnp.where(upper, pair, jnp.zeros_like(pair))], axis=0)
        k2 = jnp.concatenate([kp_ref[:, sl], kc_ref[:, sl]], axis=0)
        v2 = jnp.concatenate([vp_ref[:, sl], vc_ref[:, sl]], axis=0)
        s2 = _dot_nt(q2, k2)
        vv = jnp.concatenate([jnp.where(upper_v, jnp.zeros_like(v2), v2),
                              jnp.where(upper_v, v2, jnp.zeros_like(v2))], axis=0)
        ps, ls = [], []
        for u in range(2):
            s = s2[u * QB:(u + 1) * QB] + bias
            m = jnp.max(s, axis=1, keepdims=True)
            p = jnp.exp(s - m)
            l = jnp.sum(p, axis=1, keepdims=True)
            ps.append(p.astype(BF16))
            ls.append(l)
            lse_blk = lse_blk + jnp.where(lane128 == 2 * j + u, m + jnp.log(l), 0.0)
        pv = jnp.dot(jnp.concatenate(ps, axis=1), vv, preferred_element_type=F32)
        outs.append(pv / jnp.where(upper, ls[1], ls[0]))
    o_ref[...] = jnp.concatenate(outs, axis=1).astype(BF16)
    lse_ref[...] = lse_blk


def _dilated_branch(q, k, v, window, dil):
    B, n, _ = q.shape
    view = lambda a: a
    cur = pl.BlockSpec((None, C_BLOCK, D_C), lambda b, r, c: (b, c, r))
    prev = pl.BlockSpec((None, C_BLOCK, D_C), lambda b, r, c: (b, jnp.maximum(c - 1, 0), r))
    o, lse = pl.pallas_call(
        functools.partial(_dilated_kernel, window // dil),
        grid=(B, dil, n // C_BLOCK),
        in_specs=[cur, cur, prev, cur, prev],
        out_specs=[cur, pl.BlockSpec((None, C_BLOCK, LANES), lambda b, r, c: (b, c, r))],
        out_shape=[jax.ShapeDtypeStruct((B, n, dil * D_C), BF16), jax.ShapeDtypeStruct((B, n, dil * LANES), F32)],
        compiler_params=_cparams(("parallel", "parallel", "arbitrary")),
        name="dilated_w%d_d%d" % (window, dil),
    )(view(q), view(k), view(k), view(v), view(v))
    return o, lse


def _merge_outproj_kernel(x_ref, o1_ref, o2_ref, o3_ref, l1_ref, l2_ref, l3_ref, ex_ref, wo_ref, out_ref,
                          o_scr, l_scr):
    tm = x_ref.shape[0]

    def in_token_order(o_ref, l_ref, slot, dil):
        if dil == 1:
            return o_ref[...].astype(F32), l_ref[...]
        n_groups = D_C // LANES
        for r in range(dil):
            rows = pl.ds(r, tm // dil, stride=dil)
            for g in range(n_groups):
                o_scr[slot, g, rows, :] = o_ref[:, r * D_C + g * LANES:r * D_C + (g + 1) * LANES].astype(F32)
            l_scr[slot, rows, :] = l_ref[:, r * LANES:(r + 1) * LANES]
        return jnp.concatenate([o_scr[slot, g] for g in range(n_groups)], axis=1), l_scr[slot]

    branches = [in_token_order(o_ref, l_ref, i - 1, dil) for i, (o_ref, l_ref, (_, dil)) in
                enumerate(zip((o1_ref, o2_ref, o3_ref), (l1_ref, l2_ref, l3_ref), C_PATTERNS))]
    lses = [l for _, l in branches]
    m = jnp.maximum(jnp.maximum(lses[0], lses[1]), lses[2])
    es = [jnp.exp(l - m) for l in lses]
    inv = 1.0 / (es[0] + es[1] + es[2])
    y = jnp.zeros(x_ref.shape, F32)
    for e, (o, _) in zip(es, branches):
        y = y + _dot_split_lhs(e * inv, ex_ref[...]) * o
    out_ref[...] = x_ref[...] + jnp.dot(y.astype(BF16), wo_ref[...], preferred_element_type=F32)


def _head_expand_matrix():
    e = np.zeros((LANES, D_C), np.float32)
    for h in range(H_C):
        e[h, h * HEAD_DIM:(h + 1) * HEAD_DIM] = 1.0
    return jnp.asarray(e, BF16)


def _merge_outproj(x, os_, lses, wo, tm):
    N = x.shape[0]
    tok = lambda width: pl.BlockSpec((tm, width), lambda i: (i, 0))
    full = lambda a: pl.BlockSpec(a.shape, lambda i: (0,) * a.ndim)
    stream = lambda width, dil: pl.BlockSpec((tm // dil, dil * width), lambda i: (i, 0))
    ex = _head_expand_matrix()
    dils = [dil for _, dil in C_PATTERNS]
    return pl.pallas_call(
        _merge_outproj_kernel,
        grid=(N // tm,),
        in_specs=([tok(D_MODEL)] + [stream(D_C, d) for d in dils] + [stream(LANES, d) for d in dils]
                  + [full(ex), full(wo)]),
        out_specs=tok(D_MODEL),
        out_shape=jax.ShapeDtypeStruct((N, D_MODEL), F32),
        scratch_shapes=[pltpu.VMEM((2, D_C // LANES, tm, LANES), F32), pltpu.VMEM((2, tm, LANES), F32)],
        compiler_params=_cparams(("parallel",)),
        name="merge_outproj",
    )(x, *os_, *lses, ex, wo)


def _top2_gates(logits):
    lane = _iota(logits.shape, 1)
    m1 = jnp.max(logits, axis=1, keepdims=True)
    i1 = jnp.min(jnp.where(logits == m1, lane, LANES), axis=1, keepdims=True)
    rest = jnp.where(lane == i1, -jnp.inf, logits)
    m2 = jnp.max(rest, axis=1, keepdims=True)
    i2 = jnp.min(jnp.where(rest == m2, lane, LANES), axis=1, keepdims=True)
    e2 = jnp.exp(m2 - m1)
    g1 = 1.0 / (1.0 + e2)
    return jnp.where(lane == i1, g1, 0.0) + jnp.where(lane == i2, e2 * g1, 0.0)


def _moe_dense_kernel(x_ref, g_ref, rw_ref, rb_ref, wg_ref, wu_ref, wd_ref, gf_ref, o_ref, hn_scr, gate_scr, acc_scr):
    e = pl.program_id(1)
    j = pl.program_id(2)

    @pl.when((e == 0) & (j == 0))
    def _():
        hn = _rms(x_ref[...], g_ref[...])
        hn_scr[...] = hn.astype(BF16)
        logits = jnp.dot(hn.astype(BF16), rw_ref[...], preferred_element_type=F32) + rb_ref[...]
        gate_scr[...] = _top2_gates(logits)
        acc_scr[...] = jnp.zeros_like(acc_scr)

    hn = hn_scr[...]
    gate = gate_scr[...]
    gate_e = jnp.sum(jnp.where(_iota(gate.shape, 1) == e, gate, 0.0), axis=1, keepdims=True)
    act = _silu(jnp.dot(hn, wg_ref[...], preferred_element_type=F32)) * jnp.dot(hn, wu_ref[...], preferred_element_type=F32)
    acc_scr[...] += gate_e * jnp.dot(act.astype(BF16), wd_ref[...], preferred_element_type=F32)

    @pl.when((e == pl.num_programs(1) - 1) & (j == pl.num_programs(2) - 1))
    def _():
        o_ref[...] = _rms(x_ref[...] + acc_scr[...], gf_ref[...])


def _moe_dense(x, g, rw, rb, wg, wu, wd, gf, tm, tf):
    N = x.shape[0]
    tok = pl.BlockSpec((tm, D_MODEL), lambda i, e, j: (i, 0))
    full = lambda a: pl.BlockSpec(a.shape, lambda i, e, j: (0,) * a.ndim)
    return pl.pallas_call(
        _moe_dense_kernel,
        grid=(N // tm, N_EXPERTS, D_FF_EXPERT // tf),
        in_specs=[tok, full(g), full(rw), full(rb),
                  pl.BlockSpec((None, D_MODEL, tf), lambda i, e, j: (e, 0, j)),
                  pl.BlockSpec((None, D_MODEL, tf), lambda i, e, j: (e, 0, j)),
                  pl.BlockSpec((None, tf, D_MODEL), lambda i, e, j: (e, j, 0)), full(gf)],
        out_specs=tok,
        out_shape=jax.ShapeDtypeStruct((N, D_MODEL), F32),
        scratch_shapes=[pltpu.VMEM((tm, D_MODEL), BF16), pltpu.VMEM((tm, LANES), F32), pltpu.VMEM((tm, D_MODEL), F32)],
        compiler_params=_cparams(("parallel", "arbitrary", "arbitrary")),
        name="moe_dense",
    )(x, g, rw, rb, wg, wu, wd, gf)


MOE_TR = 1024
MOE_RC = 128
MOE_CAP = 2 * MOE_TR + N_EXPERTS * MOE_RC
MOE_PAD = 16


def _moe_route_kernel(x_ref, g_ref, rw_ref, rb_ref, hn_ref, gate_ref, pos_ref, rankt_ref, cnt_ref):
    TR = x_ref.shape[0]
    hn = _rms(x_ref[...], g_ref[...]).astype(BF16)
    hn_ref[...] = hn
    gate = _top2_gates(jnp.dot(hn, rw_ref[...], preferred_element_type=F32) + rb_ref[...])
    gate_ref[...] = gate
    sel = jnp.where(gate.T[:MOE_PAD] > 0.0, 1.0, 0.0)
    triu = jnp.where(_iota((LANES, LANES), 0) <= _iota((LANES, LANES), 1), 1.0, 0.0).astype(BF16)
    carry = jnp.zeros((MOE_PAD, 1), F32)
    ranks = []
    for c in range(TR // LANES):
        blk = sel[:, c * LANES:(c + 1) * LANES]
        pref = jnp.dot(blk.astype(BF16), triu, preferred_element_type=F32)
        ranks.append(jnp.where(blk > 0.0, carry + pref - 1.0, -1.0))
        carry = carry + pref[:, LANES - 1:]
    rank_t = jnp.concatenate(ranks, axis=1)
    rankt_ref[...] = rank_t[:N_EXPERTS]
    padded = jnp.ceil(carry * (1.0 / MOE_RC)) * MOE_RC
    lower = jnp.where(_iota((MOE_PAD, MOE_PAD), 0) > _iota((MOE_PAD, MOE_PAD), 1), 1.0, 0.0).astype(BF16)
    offs = jnp.dot(lower, jnp.broadcast_to(padded, (MOE_PAD, LANES)).astype(BF16), preferred_element_type=F32)
    cnt_ref[0:N_EXPERTS, :] = jnp.broadcast_to(carry[:N_EXPERTS], (N_EXPERTS, LANES))
    cnt_ref[N_EXPERTS:, :] = offs[:N_EXPERTS]
    pos_t = jnp.where(rank_t >= 0.0, rank_t + offs[:, :1], -1.0)
    p1 = jnp.max(pos_t, axis=0, keepdims=True)
    p2 = jnp.max(jnp.where(pos_t == p1, -1.0, pos_t), axis=0, keepdims=True)
    rows = _iota((LANES, TR), 0)
    pos_ref[...] = jnp.where(rows == 0, p1, jnp.where(rows == 1, p2, -1.0)).T


def _moe_route(x, g, rw, rb):
    N = x.shape[0]
    nt = N // MOE_TR
    tok = lambda width: pl.BlockSpec((MOE_TR, width), lambda i: (i, 0))
    full = lambda a: pl.BlockSpec(a.shape, lambda i: (0,) * a.ndim)
    return pl.pallas_call(
        _moe_route_kernel,
        grid=(nt,),
        in_specs=[tok(D_MODEL), full(g), full(rw), full(rb)],
        out_specs=[tok(D_MODEL), tok(LANES), tok(LANES), pl.BlockSpec((N_EXPERTS, MOE_TR), lambda i: (0, i)),
                   pl.BlockSpec((None, 2 * N_EXPERTS, LANES), lambda i: (i, 0, 0))],
        out_shape=[jax.ShapeDtypeStruct((N, D_MODEL), BF16), jax.ShapeDtypeStruct((N, LANES), F32),
                   jax.ShapeDtypeStruct((N, LANES), F32), jax.ShapeDtypeStruct((N_EXPERTS, N), F32),
                   jax.ShapeDtypeStruct((nt, 2 * N_EXPERTS, LANES), F32)],
        compiler_params=_cparams(("parallel",)),
        name="moe_route",
    )(x, g, rw, rb)


def _moe_routed_kernel(meta_ref, x_ref, hn_ref, gate_ref, pos_ref, rankt_ref, wg_ref, wu_ref, wd_ref, gf_ref, o_ref,
                       xc_scr, y_scr, grow_scr):
    i = pl.program_id(0)
    e = pl.program_id(1)
    j = pl.program_id(2)
    TR, RC = MOE_TR, MOE_RC
    n_chunks = lax.div(meta_ref[i, e] + (RC - 1), RC)
    off = meta_ref[i, N_EXPERTS + e]

    @pl.when((e == 0) & (j == 0))
    def _():
        y_scr[...] = jnp.zeros_like(y_scr)
        grow_scr[...] = jnp.zeros_like(grow_scr)

    @pl.when(j == 0)
    def _():
        rank_row = rankt_ref[pl.ds(e, 1), :]
        gate = gate_ref[...]
        gate_e = jnp.where(_iota(gate.shape, 1) == e, gate, 0.0)

        def gather(c, carry):
            rows = pl.multiple_of(off + c * RC, RC)
            want = (c * RC + _iota((RC, 1), 0)).astype(F32)
            onehot = jnp.where(rank_row == want, 1.0, 0.0).astype(BF16)
            xc_scr[pl.ds(rows, RC), :] = jnp.dot(onehot, hn_ref[...], preferred_element_type=F32).astype(BF16)
            grow_scr[pl.ds(rows, RC), :] = _dot_split_rhs(onehot, gate_e)
            return carry

        lax.fori_loop(0, n_chunks, gather, 0)

    def expert(c, carry):
        rows = pl.multiple_of(off + c * RC, RC)
        xc = xc_scr[pl.ds(rows, RC), :]
        act = _silu(jnp.dot(xc, wg_ref[...], preferred_element_type=F32)) * jnp.dot(xc, wu_ref[...], preferred_element_type=F32)
        y_scr[pl.ds(rows, RC), :] += jnp.dot(act.astype(BF16), wd_ref[...], preferred_element_type=F32)
        return carry

    lax.fori_loop(0, n_chunks, expert, 0)

    @pl.when((e == pl.num_programs(1) - 1) & (j == pl.num_programs(2) - 1))
    def _():
        step = 256
        for r in range(MOE_CAP // step):
            sl = slice(r * step, (r + 1) * step)
            gr = jnp.sum(grow_scr[sl, :], axis=1, keepdims=True)
            xc_scr[sl, :] = (y_scr[sl, :] * gr).astype(BF16)
        lane = _iota((step, MOE_CAP), 1).astype(F32)
        for r in range(TR // step):
            sl = slice(r * step, (r + 1) * step)
            pos = pos_ref[sl, :]
            scatter = jnp.where(lane == pos[:, 0:1], 1.0, jnp.where(lane == pos[:, 1:2], 1.0, 0.0)).astype(BF16)
            y = jnp.dot(scatter, xc_scr[...], preferred_element_type=F32)
            o_ref[sl, :] = _rms(x_ref[sl, :] + y, gf_ref[...])


def _moe_routed(x, g, rw, rb, wg, wu, wd, gf, tf):
    N = x.shape[0]
    nt = N // MOE_TR
    hn, gate, pos, rank_t, meta = _moe_route(x, g, rw, rb)
    meta = meta[:, :, 0].astype(I32)
    tok = lambda width: pl.BlockSpec((MOE_TR, width), lambda i, e, j, m: (i, 0))
    full = lambda a: pl.BlockSpec(a.shape, lambda i, e, j, m: (0,) * a.ndim)
    grid_spec = pltpu.PrefetchScalarGridSpec(
        num_scalar_prefetch=1,
        grid=(nt, N_EXPERTS, D_FF_EXPERT // tf),
        in_specs=[tok(D_MODEL), tok(D_MODEL), tok(LANES), tok(LANES),
                  pl.BlockSpec((N_EXPERTS, MOE_TR), lambda i, e, j, m: (0, i)),
                  pl.BlockSpec((None, D_MODEL, tf), lambda i, e, j, m: (e, 0, j)),
                  pl.BlockSpec((None, D_MODEL, tf), lambda i, e, j, m: (e, 0, j)),
                  pl.BlockSpec((None, tf, D_MODEL), lambda i, e, j, m: (e, j, 0)), full(gf)],
        out_specs=tok(D_MODEL),
        scratch_shapes=[pltpu.VMEM((MOE_CAP, D_MODEL), BF16), pltpu.VMEM((MOE_CAP, D_MODEL), F32),
                        pltpu.VMEM((MOE_CAP, LANES), F32)],
    )
    return pl.pallas_call(
        _moe_routed_kernel,
        grid_spec=grid_spec,
        out_shape=jax.ShapeDtypeStruct((N, D_MODEL), F32),
        compiler_params=_cparams(("parallel", "arbitrary", "arbitrary")),
        name="moe_routed",
    )(meta, x, hn, gate, pos, rank_t, wg, wu, wd, gf)


def _outproj_kernel(x_ref, y_ref, wo_ref, o_ref):
    o_ref[...] = x_ref[...] + jnp.dot(y_ref[...], wo_ref[...], preferred_element_type=F32)


def _outproj(x, y, wo, tm):
    N = x.shape[0]
    tok = lambda width: pl.BlockSpec((tm, width), lambda i: (i, 0))
    return pl.pallas_call(
        _outproj_kernel,
        grid=(N // tm,),
        in_specs=[tok(D_MODEL), tok(y.shape[1]), pl.BlockSpec(wo.shape, lambda i: (0, 0))],
        out_specs=tok(D_MODEL),
        out_shape=jax.ShapeDtypeStruct((N, D_MODEL), F32),
        compiler_params=_cparams(("parallel",)),
        name="outproj",
    )(x, y, wo)


def _pad_router(router_w, router_b):
    rw = jnp.concatenate([router_w, jnp.zeros((D_MODEL, LANES - N_EXPERTS), router_w.dtype)], axis=1).astype(BF16)
    rb = jnp.concatenate([router_b.astype(F32), jnp.full((LANES - N_EXPERTS,), NEG_BIG, F32)]).reshape(1, LANES)
    return rw, rb


def _pad_w_in0(w_in_0):
    o = A_COLS + 3 * D_B + H_IDX * D_IDX
    wi = w_in_0[:, o:o + H_IDX]
    ki = w_in_0[:, o + H_IDX:o + H_IDX + D_IDX]
    pad = jnp.zeros((D_MODEL, LANES - D_IDX - H_IDX), w_in_0.dtype)
    return jnp.concatenate([w_in_0[:, :o], ki, wi, pad], axis=1).astype(BF16)


def _rwkv_params(w):
    row = lambda a: a.reshape(1, -1).astype(F32)
    zeros = jnp.zeros((DECAY_LORA, D_A), F32)
    return {
        "mu": row(w["a_mu"]), "w0": row(w["a_w0"]), "a0": row(w["a_a0"]),
        "w2": jnp.concatenate([w["a_w2"], zeros], axis=0).astype(BF16),
        "a2": jnp.concatenate([zeros, w["a_a2"]], axis=0).astype(BF16),
        "g2": w["a_g2"].astype(BF16),
        "kk": row(w["a_kk"]), "ka": row(w["a_ka"]), "rk": row(w["a_rk"]),
        "bd": _block_diag_ones(D_A), "lnw": row(w["a_ln_w"]), "lnb": row(w["a_ln_b"]),
    }


DSA_QB = 128


def _order_key(score):
    bits = lax.bitcast_convert_type(score, I32)
    key = jnp.where(bits < 0, bits ^ jnp.int32(0x7FFFFFFF), bits)
    return jnp.where(score == 0.0, 0, key)


def _fold_lanes(x):
    part = x[:, :LANES]
    for j in range(1, x.shape[1] // LANES):
        part = part + x[:, j * LANES:(j + 1) * LANES]
    return part


def _fold_rows(x, rows=64):
    rows = min(rows, x.shape[0])
    part = x[:rows]
    for j in range(1, x.shape[0] // rows):
        part = part + x[j * rows:(j + 1) * rows]
    return part


def _kth_largest_key(count, topk, shape):
    kf = float(topk)
    base = jnp.where(count(jnp.zeros(shape, I32)) >= kf, 0, INT_MIN).astype(I32)

    def bit_body(i, base):
        cand = base + lax.shift_left(jnp.int32(1), 30 - i)
        return jnp.where(count(cand) >= kf, cand, base)

    return lax.fori_loop(0, 31, bit_body, base)


def _dsa_prompt_kernel(topk, KB, KA, idx_bits, qt_ref, k_ref, vt_ref, qit_ref, wit_ref, ki_ref, o_ref,
                       key_scr, thr_scr, s_scr, bias_scr, p_scr, m_scr, l_scr, acc_scr):
    QB = DSA_QB
    q_pos0 = pl.program_id(1) * QB
    nkb = lax.div(q_pos0 + QB - 1, KB) + 1
    nka = lax.div(q_pos0 + QB - 1, KA) + 1
    key_row = _iota((KB, QB), 0)
    q_lane = _iota((KB, QB), 1)

    qit = qit_ref[...]
    q_cat = jnp.concatenate([qit[h * D_IDX:(h + 1) * D_IDX] for h in range(H_IDX)], axis=1)
    wit = wit_ref[...]
    w_cat = jnp.concatenate([wit[h:h + 1] for h in range(H_IDX)], axis=1)

    def score_body(kb, carry):
        off = pl.multiple_of(kb * KB, KB)
        res = jnp.dot(ki_ref[pl.ds(off, KB), :], q_cat, preferred_element_type=F32)
        weighted = jnp.maximum(res, 0.0) * w_cat
        admissible = (off + key_row) <= (q_pos0 + q_lane)
        key_scr[pl.ds(off, KB), :] = jnp.where(admissible, _order_key(_fold_lanes(weighted)), INT_MIN)
        return carry

    lax.fori_loop(0, nkb, score_body, 0)

    def count_where(pred):
        def body(kb, acc):
            off = pl.multiple_of(kb * KB, KB)
            return acc + _fold_rows(pred(key_scr[pl.ds(off, KB), :], off + key_row))
        acc = lax.fori_loop(0, nkb, body, jnp.zeros((min(64, KB), QB), F32))
        return jnp.sum(acc, axis=0, keepdims=True)

    count_ge = lambda cand: count_where(lambda blk, idx: jnp.where(blk >= cand, 1.0, 0.0))
    thr = _kth_largest_key(count_ge, topk, (1, QB))
    n_gt = count_where(lambda blk, idx: jnp.where(blk > thr, 1.0, 0.0))
    n_ge = count_ge(thr)
    need = float(topk) - n_gt
    thr_scr[...] = jnp.full((8, QB), 2 ** 30, I32)

    @pl.when(jnp.max(n_ge - float(topk)) > 0.0)
    def _():
        def bit_body(i, j):
            cand = j + lax.shift_left(jnp.int32(1), idx_bits - 1 - i)
            ties = count_where(lambda blk, idx: jnp.where(blk == thr, jnp.where(idx <= cand, 1.0, 0.0), 0.0))
            return jnp.where(ties < need, cand, j)
        j = lax.fori_loop(0, idx_bits, bit_body, jnp.full((1, QB), -1, I32))
        thr_scr[...] = jnp.broadcast_to(j + 1, (8, QB))

    idx_thr = thr_scr[0:1, :]
    tie_bias = jnp.where(thr == INT_MIN, NEG_BIG, 0.0)

    qt = qt_ref[...]
    feat = _iota((LANES, QB), 0)
    qtm = []
    for h in range(H_B):
        pair = qt[(h // 2) * LANES:(h // 2 + 1) * LANES]
        qtm.append(jnp.where((feat // HEAD_DIM) == (h % 2), pair, jnp.zeros_like(pair)))
    q_pairs = [jnp.concatenate([qtm[2 * j], qtm[2 * j + 1]], axis=1) for j in range(H_B // 2)]
    key_row_a = _iota((KA, QB), 0)

    def scores_into(slot, ka):
        off = pl.multiple_of(ka * KA, KA)
        for j in range(H_B // 2):
            s2 = jnp.dot(k_ref[pl.ds(off, KA), j * LANES:(j + 1) * LANES], q_pairs[j], preferred_element_type=F32)
            s_scr[slot, 2 * j] = s2[:, :QB]
            s_scr[slot, 2 * j + 1] = s2[:, QB:]

    scores_into(0, 0)

    m_scr[...] = jnp.full(m_scr.shape, NEG_BIG, F32)
    l_scr[...] = jnp.zeros_like(l_scr)
    acc_scr[...] = jnp.zeros_like(acc_scr)
    CH = 64
    n_ch = KA // CH

    def attend_block(slot, ka):
        live = ka < nka
        ka = jnp.minimum(ka, nka - 1)
        off = pl.multiple_of(ka * KA, KA)
        blk = key_scr[pl.ds(off, KA), :]
        hit = jnp.where(live, 0.0, NEG_BIG)
        bias_scr[...] = jnp.where(blk > thr, hit, jnp.where(
            blk == thr, jnp.where((off + key_row_a) <= idx_thr, tie_bias + hit, NEG_BIG), NEG_BIG))
        for h in range(H_B):
            top = None
            for c in range(n_ch):
                rows = slice(c * CH, (c + 1) * CH)
                piece = s_scr[slot, h, rows, :] + bias_scr[rows, :]
                s_scr[slot, h, rows, :] = piece
                top = piece if top is None else jnp.maximum(top, piece)
            m_old = m_scr[h:h + 1, :]
            m_new = jnp.maximum(m_old, jnp.max(top, axis=0, keepdims=True))
            alpha = jnp.exp2(m_old - m_new)
            total = None
            for c in range(n_ch):
                rows = slice(c * CH, (c + 1) * CH)
                e = jnp.exp2(s_scr[slot, h, rows, :] - m_new)
                p_scr[h, rows, :] = e.astype(BF16)
                total = e if total is None else total + e
            m_scr[h:h + 1, :] = m_new
            l_scr[h:h + 1, :] = alpha * l_scr[h:h + 1, :] + jnp.sum(total, axis=0, keepdims=True)
            pv = jnp.dot(vt_ref[h * HEAD_DIM:(h + 1) * HEAD_DIM, pl.ds(off, KA)], p_scr[h],
                         preferred_element_type=F32)
            acc_scr[h] = acc_scr[h] * alpha + pv

    def attn_body(i, carry):
        scores_into(1, jnp.minimum(2 * i + 1, nka - 1))
        attend_block(0, 2 * i)
        scores_into(0, jnp.minimum(2 * i + 2, nka - 1))
        attend_block(1, 2 * i + 1)
        return carry

    lax.fori_loop(0, lax.div(nka + 1, 2), attn_body, 0)
    out_t = jnp.concatenate([acc_scr[h] / l_scr[h:h + 1, :] for h in range(H_B)], axis=0)
    o_ref[...] = out_t.T.astype(BF16)


def _dsa_prompt(qt, k, vt, qit, wit, ki):
    B, T, _ = k.shape
    topk = min(IDX_TOPK_MAX, T // 4)
    kb = min(512, T)
    ka = min(512, T)
    idx_bits = max(1, int(np.ceil(np.log2(T))))
    qcols = lambda rows: pl.BlockSpec((None, rows, DSA_QB), lambda b, i: (b, 0, i))
    whole = lambda r, c: pl.BlockSpec((None, r, c), lambda b, i: (b, 0, 0))
    return pl.pallas_call(
        functools.partial(_dsa_prompt_kernel, topk, kb, ka, idx_bits),
        grid=(B, T // DSA_QB),
        in_specs=[qcols(D_B), whole(T, D_B), whole(D_B, T), qcols(D_B), qcols(H_IDX), whole(T, D_IDX)],
        out_specs=pl.BlockSpec((None, DSA_QB, D_B), lambda b, i: (b, i, 0)),
        out_shape=jax.ShapeDtypeStruct((B, T, D_B), BF16),
        scratch_shapes=[pltpu.VMEM((T, DSA_QB), I32), pltpu.VMEM((8, DSA_QB), I32),
                        pltpu.VMEM((2, H_B, ka, DSA_QB), F32), pltpu.VMEM((ka, DSA_QB), F32),
                        pltpu.VMEM((H_B, ka, DSA_QB), BF16), pltpu.VMEM((H_B, DSA_QB), F32),
                        pltpu.VMEM((H_B, DSA_QB), F32), pltpu.VMEM((H_B, HEAD_DIM, DSA_QB), F32)],
        compiler_params=_cparams(("parallel", "arbitrary")),
        name="dsa_prompt",
    )(qt, k, vt, qit, wit, ki)


def _tile(n, pref):
    return pref if n % pref == 0 else n


def kernel(x_prompt, x_sample, state_a_wkv, state_a_shift, cache_b_k, cache_b_v, cache_b_kidx, cache_c_k, cache_c_v, page_table, norm_mix, norm_ffn, norm_final, w_in_0, w_out_0, a_mu, a_w0, a_w2, a_a0, a_a2, a_g2, a_kk, a_ka, a_rk, a_ln_w, a_ln_b, ffn_wg, ffn_wu, ffn_wd, w_in_1, w_out_1, router_w, router_b, moe_wg, moe_wu, moe_wd):
    B, T, D = x_prompt.shape
    DB, S, _ = x_sample.shape
    assert S == 1 and D == D_MODEL
    past = page_table.shape[1] * PAGE_SIZE
    row = lambda a: a.reshape(1, -1).astype(F32)
    b16 = lambda a: a.astype(BF16)

    prm = _rwkv_params(dict(a_mu=a_mu, a_w0=a_w0, a_w2=a_w2, a_a0=a_a0, a_a2=a_a2, a_g2=a_g2, a_kk=a_kk, a_ka=a_ka,
                            a_rk=a_rk, a_ln_w=a_ln_w, a_ln_b=a_ln_b))
    w_in0 = _pad_w_in0(w_in_0)
    w_out0, w_in1, w_out1 = b16(w_out_0), b16(w_in_1), b16(w_out_1)
    f_wg, f_wu, f_wd = b16(ffn_wg), b16(ffn_wu), b16(ffn_wd)
    m_wg, m_wu, m_wd = b16(moe_wg), b16(moe_wu), b16(moe_wd)
    rw, rb = _pad_router(router_w, router_b)
    g_mix0, g_mix1 = row(norm_mix[0]), row(norm_mix[1])
    g_ffn0, g_ffn1, g_fin = row(norm_ffn[0]), row(norm_ffn[1]), row(norm_final)
    tf_ffn = D_FF // 2
    tf_moe = 512

    N = B * T
    cos_p, sin_p = _rope_tables(jnp.arange(T, dtype=I32))
    pa, kf, vf, kif, qt, kb, vt, qit, wit, kib = _inproj0(x_prompt, g_mix0, w_in0, cos_p, sin_p, _tile(T, 256), True)
    prep = _rwkv_prep(pa, jnp.zeros((B, A_COLS), F32), prm, _tile(T, 256), True)
    ya, p_a_wkv = _rwkv_scan(*prep, prm["lnw"], prm["lnb"], _tile(T, 256))
    yb = _dsa_prompt(qt, kb, vt, qit, wit, kib)
    h = _outproj_ffn(x_prompt.reshape(N, D), ya.reshape(N, D_A), yb.reshape(N, D_B), w_out0, g_ffn0, f_wg, f_wu, f_wd,
                     _tile(N, 512), tf_ffn)
    keep = min(C_WINDOW_MAX, T)
    q1, k1, v1, k1f, v1f, *streams = _inproj1(h.reshape(B, T, D), g_mix1, w_in1, cos_p, sin_p, _tile(T, 256), keep)
    n_dil = len(C_PATTERNS) - 1
    outs, lses = [], []
    for i, (window, dil) in enumerate(C_PATTERNS):
        qkv = (q1, k1, v1) if i == 0 else tuple(streams[which * n_dil + i - 1] for which in range(3))
        o, lse = _dilated_branch(*qkv, window, dil)
        outs.append(o.reshape(N // dil, dil * D_C))
        lses.append(lse.reshape(N // dil, dil * LANES))
    h = _merge_outproj(h, outs, lses, w_out1, _tile(N, 512))
    if N % MOE_TR == 0:
        y_prompt = _moe_routed(h, g_ffn1, rw, rb, m_wg, m_wu, m_wd, g_fin, tf_moe).reshape(B, T, D)
    else:
        y_prompt = _moe_dense(h, g_ffn1, rw, rb, m_wg, m_wu, m_wd, g_fin, N, tf_moe).reshape(B, T, D)
    rows = lambda a, n: jnp.moveaxis(a.reshape(B, n, HEAD_DIM, a.shape[-1]), -1, 1)
    prompt_state = (p_a_wkv, pa[:, -1], rows(kf, H_B), rows(vf, H_B), jnp.swapaxes(kif, 1, 2),
                    rows(k1f, H_C), rows(v1f, H_C))

    cos_s, sin_s = _rope_tables(jnp.full((DB,), past, I32))
    xs = x_sample.reshape(1, DB, D)
    pa, kf, vf, kif, q, qi, tail = _inproj0(xs, g_mix0, w_in0, cos_s, sin_s, DB, False)
    prep = _rwkv_prep(pa, state_a_shift.astype(F32), prm, DB, False)
    ya, s_a_wkv = _rwkv_step(state_a_wkv.astype(F32), *(a.reshape(DB, D_A) for a in prep), prm["lnw"], prm["lnb"])
    pad8 = lambda a: jnp.concatenate([a, jnp.zeros_like(a)], axis=1)
    qi16 = pad8(qi.reshape(DB, H_IDX, D_IDX))
    wi16 = pad8((tail[0, :, D_IDX:D_IDX + H_IDX] * IDX_SCALE).reshape(DB, H_IDX, 1))
    ki16 = jnp.broadcast_to(kif.reshape(DB, 1, D_IDX), (DB, 16, D_IDX))
    bias, bias_new = _dsa_sample_select(page_table, qi16, wi16, ki16, jnp.swapaxes(cache_b_kidx.astype(F32), 1, 2))
    heads = lambda a, n: a.astype(F32).reshape(DB, n, HEAD_DIM)
    yb = _dsa_sample_attend(page_table, heads(q, H_B), heads(kf, H_B), heads(vf, H_B), bias, bias_new,
                            cache_b_k.astype(F32), cache_b_v.astype(F32))
    hs = _outproj_ffn(x_sample.reshape(DB, D), ya, yb.reshape(DB, D_B).astype(BF16), w_out0, g_ffn0, f_wg, f_wu, f_wd,
                      DB, tf_ffn)
    q1, k1, v1, k1f, v1f = _inproj1(hs.reshape(1, DB, D), g_mix1, w_in1, cos_s, sin_s, DB)
    yc = _dilated_sample(heads(q1, H_C), heads(k1f, H_C), heads(v1f, H_C), cache_c_k.astype(F32), cache_c_v.astype(F32))
    hs = _outproj(hs, yc.reshape(DB, D_C).astype(BF16), w_out1, DB)
    y_sample = _moe_dense(hs, g_ffn1, rw, rb, m_wg, m_wu, m_wd, g_fin, DB, tf_moe).reshape(DB, 1, D)
    keep = min(C_WINDOW_MAX, cache_c_k.shape[1] + 1)
    s_c_k = jnp.concatenate([cache_c_k, k1f.reshape(DB, 1, H_C, HEAD_DIM)], axis=1)[:, -keep:]
    s_c_v = jnp.concatenate([cache_c_v, v1f.reshape(DB, 1, H_C, HEAD_DIM)], axis=1)[:, -keep:]
    sample_state = (s_a_wkv, pa[0], kf.reshape(DB, 1, H_B, HEAD_DIM), vf.reshape(DB, 1, H_B, HEAD_DIM),
                    kif.reshape(DB, 1, D_IDX), s_c_k, s_c_v)
    return (y_prompt, y_sample) + prompt_state + sample_state
```

```python
import functools

import numpy as np
import jax
import jax.numpy as jnp
from jax import lax
from jax.experimental import pallas as pl
from jax.experimental.pallas import tpu as pltpu

F32 = jnp.float32
BF16 = jnp.bfloat16
I32 = jnp.int32
I16 = jnp.int16

D_MODEL = 1024
HEAD_DIM = 64
ROPE_THETA = 10000.0
NORM_EPS = 1e-6
PAGE_SIZE = 128

H_A = 8
D_A = H_A * HEAD_DIM
DECAY_LORA = 64
AAA_LORA = 64
GATE_LORA = 128
A_COLS = 3 * D_A + DECAY_LORA + AAA_LORA + GATE_LORA
GN_EPS = 64e-5

H_B = 8
D_B = H_B * HEAD_DIM
H_IDX = 8
D_IDX = 64
IDX_TOPK_MAX = 256
IDX_SCALE = (H_IDX ** -0.5) * (D_IDX ** -0.5)
B_COLS_PAD = 3 * D_B + H_IDX * D_IDX + 128

H_C = 16
D_C = H_C * HEAD_DIM
C_PATTERNS = ((128, 1), (512, 4), (2048, 16))
C_WINDOW_MAX = 2048
C_BLOCK = 128

D_FF = 2816
N_EXPERTS = 8
D_FF_EXPERT = 3584

LANES = 128
VMEM_LIMIT = 56 << 20
INT_MIN = -(2 ** 31)
NEG_BIG = -1e30
LOG2E = 1.4426950408889634


def _cparams(sem, vmem=VMEM_LIMIT):
    return pltpu.CompilerParams(dimension_semantics=sem, vmem_limit_bytes=vmem)


def _dot(a, b):
    return jnp.dot(a.astype(BF16), b.astype(BF16), preferred_element_type=F32)


def _dot_nt(a, b):
    return lax.dot_general(a.astype(BF16), b.astype(BF16), (((1,), (1,)), ((), ())), preferred_element_type=F32)


def _split(x):
    hi = x.astype(BF16)
    lo = (x - hi.astype(F32)).astype(BF16)
    return hi, lo


def _dot_split_lhs(a, b_exact):
    hi, lo = _split(a)
    return jnp.dot(hi, b_exact, preferred_element_type=F32) + jnp.dot(lo, b_exact, preferred_element_type=F32)


def _dot_split_rhs(a_exact, b):
    hi, lo = _split(b)
    return jnp.dot(a_exact, hi, preferred_element_type=F32) + jnp.dot(a_exact, lo, preferred_element_type=F32)


def _dot3(a, b):
    ah, al = _split(a)
    bh, bl = _split(b)
    return (jnp.dot(ah, bh, preferred_element_type=F32) + jnp.dot(ah, bl, preferred_element_type=F32)
            + jnp.dot(al, bh, preferred_element_type=F32))


def _iota(shape, axis):
    return lax.broadcasted_iota(I32, shape, axis)


def _rope_tables(pos):
    half = HEAD_DIM // 2
    inv_freq = jnp.power(ROPE_THETA, -jnp.arange(half, dtype=F32) / half)
    ang = pos.astype(F32)[:, None] * inv_freq[None, :]
    cos = jnp.cos(ang)
    sin = jnp.sin(ang)
    return jnp.tile(cos, (1, 4)), jnp.tile(jnp.concatenate([-sin, sin], axis=1), (1, 2))


def _rope(x, cos, sin):
    w = x.shape[-1]
    reps = w // LANES
    if reps > 1:
        cos = jnp.concatenate([cos] * reps, axis=1)
        sin = jnp.concatenate([sin] * reps, axis=1)
    first_half = (_iota(x.shape, 1) % HEAD_DIM) < (HEAD_DIM // 2)
    swapped = jnp.where(first_half, pltpu.roll(x, w - HEAD_DIM // 2, 1), pltpu.roll(x, HEAD_DIM // 2, 1))
    return x * cos + swapped * sin


def _rms(x, g):
    ms = jnp.mean(x * x, axis=-1, keepdims=True)
    return x * lax.rsqrt(ms + NORM_EPS) * g


def _inproj0_kernel(for_prompt, x_ref, g_ref, w_ref, cos_ref, sin_ref, pa_ref, kf_ref, vf_ref, kif_ref, *outs):
    h = _rms(x_ref[...], g_ref[...]).astype(BF16)
    p = jnp.dot(h, w_ref[...], preferred_element_type=F32)
    cos = cos_ref[...]
    sin = sin_ref[...]
    o = A_COLS
    pa_ref[...] = p[:, :o]
    q = _rope(p[:, o:o + D_B], cos, sin) * (HEAD_DIM ** -0.5)
    k = _rope(p[:, o + D_B:o + 2 * D_B], cos, sin)
    v = p[:, o + 2 * D_B:o + 3 * D_B]
    qi = _rope(p[:, o + 3 * D_B:o + 4 * D_B], cos, sin)
    tail = p[:, o + 4 * D_B:]
    ki = _rope(tail, cos, sin)[:, :D_IDX]
    if for_prompt:
        qt_ref, k_ref, vt_ref, qit_ref, wit_ref, ki_ref = outs
        v_t = v.T
        kf_ref[...] = k.T
        vf_ref[...] = v_t
        kif_ref[...] = ki.T
        qt_ref[...] = (q * LOG2E).T.astype(BF16)
        k_ref[...] = k.astype(BF16)
        vt_ref[...] = v_t.astype(BF16)
        qit_ref[...] = qi.T.astype(BF16)
        wit_ref[...] = tail.T[D_IDX:D_IDX + H_IDX, :] * IDX_SCALE
        ki_ref[...] = ki.astype(BF16)
    else:
        q_ref, qi_ref, tail_ref = outs
        kf_ref[...] = k
        vf_ref[...] = v
        kif_ref[...] = ki
        q_ref[...] = q.astype(BF16)
        qi_ref[...] = qi.astype(BF16)
        tail_ref[...] = tail


def _inproj0(x, g, w_pad, cos, sin, tm, for_prompt):
    B, T, _ = x.shape
    tok = lambda width: pl.BlockSpec((None, tm, width), lambda b, t: (b, t, 0))
    tr = lambda rows: pl.BlockSpec((None, rows, tm), lambda b, t: (b, 0, t))
    full = lambda a: pl.BlockSpec(a.shape, lambda b, t: (0,) * a.ndim)
    tab = pl.BlockSpec((tm, LANES), lambda b, t: (t, 0))
    sds = lambda shape, dt: jax.ShapeDtypeStruct(shape, dt)
    if for_prompt:
        specs = [tok(A_COLS), tr(D_B), tr(D_B), tr(D_IDX), tr(D_B), tok(D_B), tr(D_B), tr(D_B), tr(H_IDX), tok(D_IDX)]
        shapes = [sds((B, T, A_COLS), F32), sds((B, D_B, T), F32), sds((B, D_B, T), F32), sds((B, D_IDX, T), F32),
                  sds((B, D_B, T), BF16), sds((B, T, D_B), BF16), sds((B, D_B, T), BF16), sds((B, D_B, T), BF16),
                  sds((B, H_IDX, T), F32), sds((B, T, D_IDX), BF16)]
    else:
        specs = [tok(A_COLS), tok(D_B), tok(D_B), tok(D_IDX), tok(D_B), tok(D_B), tok(LANES)]
        shapes = [sds((B, T, A_COLS), F32), sds((B, T, D_B), F32), sds((B, T, D_B), F32), sds((B, T, D_IDX), F32),
                  sds((B, T, D_B), BF16), sds((B, T, D_B), BF16), sds((B, T, LANES), F32)]
    return pl.pallas_call(
        functools.partial(_inproj0_kernel, for_prompt),
        grid=(B, T // tm),
        in_specs=[tok(D_MODEL), full(g), full(w_pad), tab, tab],
        out_specs=specs,
        out_shape=shapes,
        compiler_params=_cparams(("parallel", "arbitrary")),
        name="inproj0",
    )(x, g, w_pad, cos, sin)


def _seg_sum(x, bd):
    return _dot_split_lhs(x, bd)


def _rwkv_prep_kernel(seq_mode, p_ref, prev_ref, shift_ref, mu_ref, w0_ref, w2_ref, a0_ref, a2_ref, g2_ref,
                      kk_ref, ka_ref, rk_ref, bd_ref,
                      r_out, ld_out, k_out, v_out, kkn_out, ab_out, g_out, bonus_out):
    p = p_ref[...]
    if seq_mode:
        last = jnp.where(pl.program_id(1) == 0, shift_ref[...], prev_ref[7:8, :])
        prev = jnp.where(_iota(p.shape, 0) == 0, last, pltpu.roll(p, 1, 0))
    else:
        prev = prev_ref[...]
    xm = p + (prev - p) * mu_ref[...]
    r = xm[:, :D_A]
    k = xm[:, D_A:2 * D_A]
    v = xm[:, 2 * D_A:3 * D_A]
    wa = xm[:, 3 * D_A:3 * D_A + LANES]
    gl = xm[:, 3 * D_A + LANES:]
    z = -(w0_ref[...] + _dot(jnp.tanh(wa), w2_ref[...]))
    softplus = jnp.maximum(z, 0.0) + jnp.log(1.0 + jnp.exp(-jnp.abs(z)))
    ld_out[...] = -jnp.exp(-softplus - 0.5)
    a = jax.nn.sigmoid(a0_ref[...] + _dot(wa, a2_ref[...]))
    g_out[...] = _dot(jax.nn.sigmoid(gl), g2_ref[...])
    bd = bd_ref[...]
    kk = k * kk_ref[...]
    kkn = kk * lax.rsqrt(jnp.maximum(_seg_sum(kk * kk, bd), 1e-24))
    k2 = k * (1.0 + (a - 1.0) * ka_ref[...])
    r_out[...] = r
    k_out[...] = k2
    v_out[...] = v
    kkn_out[...] = kkn
    ab_out[...] = kkn * a
    bonus_out[...] = _seg_sum(r * k2 * rk_ref[...], bd) * v


def _block_diag_ones(n, seg=HEAD_DIM):
    i = np.arange(n)
    return jnp.asarray((i[:, None] // seg) == (i[None, :] // seg), BF16)


def _rwkv_prep(pa, shift_prev, prm, tm, seq_mode):
    B, T, _ = pa.shape
    nt = T // tm
    tok = lambda width: pl.BlockSpec((None, tm, width), lambda b, t: (b, t, 0))
    full = lambda a: pl.BlockSpec(a.shape, lambda b, t: (0,) * a.ndim)
    if seq_mode:
        prev_spec = pl.BlockSpec((None, 8, A_COLS), lambda b, t: (b, jnp.maximum(t * (tm // 8) - 1, 0), 0))
        prev_arr = pa
        shift_arr = shift_prev.reshape(B, 1, A_COLS)
        shift_spec = pl.BlockSpec((None, 1, A_COLS), lambda b, t: (b, 0, 0))
    else:
        prev_spec = tok(A_COLS)
        prev_arr = shift_prev.reshape(1, T, A_COLS)
        shift_arr = jnp.zeros((1, 1, A_COLS), F32)
        shift_spec = pl.BlockSpec((None, 1, A_COLS), lambda b, t: (0, 0, 0))
    params = [prm[n] for n in ("mu", "w0", "w2", "a0", "a2", "g2", "kk", "ka", "rk", "bd")]
    out = jax.ShapeDtypeStruct((B, T, D_A), F32)
    return pl.pallas_call(
        functools.partial(_rwkv_prep_kernel, seq_mode),
        grid=(B, nt),
        in_specs=[tok(A_COLS), prev_spec, shift_spec] + [full(a) for a in params],
        out_specs=[tok(D_A)] * 8,
        out_shape=[out] * 8,
        compiler_params=_cparams(("parallel", "arbitrary")),
        name="rwkv_prep",
    )(pa, prev_arr, shift_arr, *params)


RWKV_CHUNK = 64
RWKV_GROUP = 4
RWKV_W = RWKV_GROUP * HEAD_DIM


def _rwkv_chunk(r, ld, k, v, kkn, ab, h, tri, same_head, strict, incl, eye):
    each = lambda f, *xs: [f(*a) for a in zip(*xs)]
    w = RWKV_W

    def expand(x):
        return jnp.where(same_head, jnp.concatenate([x] * RWKV_GROUP, axis=0), 0.0).astype(BF16)

    cum = each(lambda x: _dot_split_rhs(tri, x), ld)
    cum_end = each(lambda c: c[RWKV_CHUNK - 1:RWKV_CHUNK, :], cum)
    a_t = each(lambda x, c, l: expand(-(x * jnp.exp(c - l))), kkn, cum, ld)
    r_t = each(lambda x, c: expand(x * jnp.exp(c)), r, cum)
    e_neg = each(lambda c: jnp.exp(-c), cum)
    b_t = each(lambda x, e: expand(x * e), ab, e_neg)
    k_t = each(lambda x, e: expand(x * e), k, e_neg)
    v_e = each(expand, v)
    gram = each(lambda a, rr, b, kk: _dot_nt(jnp.concatenate([a, rr], axis=0), jnp.concatenate([b, kk], axis=0)),
                a_t, r_t, b_t, k_t)
    l_ab = each(lambda g: jnp.where(strict, g[:w, :w], 0.0), gram)
    a_ak = each(lambda g: jnp.where(strict, g[:w, w:], 0.0), gram)
    a_r = each(lambda g: jnp.concatenate([jnp.where(incl, g[w:, :w], 0.0), jnp.where(incl, g[w:, w:], 0.0)], axis=1), gram)
    pinv = each(lambda l: jnp.where(eye, 1.0, 0.0) + l, l_ab)
    qpow = l_ab
    for _ in range(5):
        qpow = each(lambda q: _dot(q, q), qpow)
        pinv = each(lambda p, q: p + _dot(q, p), pinv, qpow)
    x0 = each(_dot, a_ak, v_e)
    wu = each(lambda p, a, x: _dot(p, jnp.concatenate([a, x.astype(BF16)], axis=1)), pinv, a_t, x0)
    u_e = each(lambda m, hh: _dot(m[:, :w], hh) + m[:, w:], wu, h)
    uv = each(lambda u, vv: jnp.concatenate([u.astype(BF16), vv], axis=0), u_e, v_e)
    y_e = each(lambda rr, hh, a, x: _dot(rr, hh) + _dot(a, x), r_t, h, a_r, uv)
    y = each(lambda ye: sum(ye[i * RWKV_CHUNK:(i + 1) * RWKV_CHUNK] for i in range(1, RWKV_GROUP)) + ye[:RWKV_CHUNK], y_e)
    e_rem = each(lambda ce, c: jnp.exp(ce - c), cum_end, cum)
    bk_t = each(lambda b, kk, e: jnp.concatenate(
        [jnp.where(same_head, jnp.concatenate([b * e] * RWKV_GROUP, axis=0), 0.0).T,
         jnp.where(same_head, jnp.concatenate([kk * e] * RWKV_GROUP, axis=0), 0.0).T], axis=1), ab, k, e_rem)
    g_col = each(lambda ce: jnp.exp(jnp.broadcast_to(ce, (8, w))).T[:, :1], cum_end)
    h_new = each(lambda hh, g, b, x: hh * g + _dot(b, x), h, g_col, bk_t, uv)
    return y, h_new


def _rwkv_scan_kernel(n_chunks, r_ref, ld_ref, k_ref, v_ref, kkn_ref, ab_ref, g_ref, bonus_ref, lnw_ref, lnb_ref,
                      y_ref, ht_ref, h_scr):
    @pl.when(pl.program_id(1) == 0)
    def _():
        h_scr[...] = jnp.zeros_like(h_scr)

    w = RWKV_W
    c = RWKV_CHUNK
    row = _iota((w, w), 0)
    col = _iota((w, w), 1)
    same_head = (row // c) == (col // HEAD_DIM)
    strict = (row % c) > (col % c)
    incl = (row % c) >= (col % c)
    eye = row == col
    tri = jnp.where(_iota((c, c), 0) >= _iota((c, c), 1), 1.0, 0.0).astype(BF16)
    seg_avg = jnp.where((row // HEAD_DIM) == (col // HEAD_DIM), 1.0 / HEAD_DIM, 0.0).astype(BF16)
    n_seq = r_ref.shape[0]
    chains = [(s, slice(g * w, (g + 1) * w)) for s in range(n_seq) for g in range(D_A // w)]
    hs = [h_scr[j] for j in range(len(chains))]
    for i in range(n_chunks):
        sl = slice(i * c, (i + 1) * c)
        pick = lambda ref: [ref[s, sl, ln] for s, ln in chains]
        ys, hs = _rwkv_chunk(pick(r_ref), pick(ld_ref), pick(k_ref), pick(v_ref), pick(kkn_ref), pick(ab_ref), hs,
                             tri, same_head, strict, incl, eye)
        for y, (s, ln) in zip(ys, chains):
            mean = _dot_split_lhs(y, seg_avg)
            yc = y - mean
            var = _dot_split_lhs(yc * yc, seg_avg)
            yn = yc * lax.rsqrt(var + GN_EPS) * lnw_ref[:, ln] + lnb_ref[:, ln]
            y_ref[s, sl, ln] = ((yn + bonus_ref[s, sl, ln]) * g_ref[s, sl, ln]).astype(BF16)
    for j in range(len(chains)):
        h_scr[j] = hs[j]

    @pl.when(pl.program_id(1) == pl.num_programs(1) - 1)
    def _():
        for j in range(len(chains)):
            ht_ref[j] = hs[j].T


def _rwkv_scan(r, ld, k, v, kkn, ab, g, bonus, lnw, lnb, tb):
    B, T, _ = r.shape
    ng = D_A // RWKV_W
    nb = 2 if B % 2 == 0 else 1
    blk = pl.BlockSpec((nb, tb, D_A), lambda b, t: (b, t, 0))
    par = pl.BlockSpec((1, D_A), lambda b, t: (0, 0))
    y, ht = pl.pallas_call(
        functools.partial(_rwkv_scan_kernel, tb // RWKV_CHUNK),
        grid=(B // nb, T // tb),
        in_specs=[blk] * 8 + [par, par],
        out_specs=[blk, pl.BlockSpec((nb * ng, RWKV_W, RWKV_W), lambda b, t: (b, 0, 0))],
        out_shape=[jax.ShapeDtypeStruct((B, T, D_A), BF16), jax.ShapeDtypeStruct((B * ng, RWKV_W, RWKV_W), F32)],
        scratch_shapes=[pltpu.VMEM((nb * ng, RWKV_W, RWKV_W), F32)],
        compiler_params=_cparams(("parallel", "arbitrary")),
        name="rwkv_scan",
    )(r, ld, k, v, kkn, ab, g, bonus, lnw, lnb)
    ht = ht.reshape(B, ng, RWKV_GROUP, HEAD_DIM, RWKV_GROUP, HEAD_DIM)
    idx = jnp.arange(RWKV_GROUP)
    wkv = ht[:, :, idx, :, idx, :]
    return y, jnp.moveaxis(wkv, 0, 2).reshape(B, H_A, HEAD_DIM, HEAD_DIM)


def _rwkv_step_kernel(s_ref, r_ref, ld_ref, k_ref, v_ref, kkn_ref, ab_ref, g_ref, bonus_ref, lnw_ref, lnb_ref,
                      y_ref, s_out):
    rows = H_A * HEAD_DIM
    pad = 16
    rep = jnp.where((_iota((rows, pad), 0) // HEAD_DIM) == _iota((rows, pad), 1), 1.0, 0.0).astype(BF16)
    rep_t = jnp.where((_iota((pad, rows), 1) // HEAD_DIM) == _iota((pad, rows), 0), 1.0, 0.0).astype(BF16)
    zeros8 = jnp.zeros((pad - H_A, HEAD_DIM), F32)
    spread = lambda x8: _dot_split_rhs(rep, jnp.concatenate([x8, zeros8], axis=0))
    diag = (_iota((rows, HEAD_DIM), 0) % HEAD_DIM) == _iota((rows, HEAD_DIM), 1)
    s = s_ref[...].reshape(rows, HEAD_DIM)
    a_rep = -spread(kkn_ref[...])
    sa = jnp.sum(s * a_rep, axis=1, keepdims=True)
    v_col = jnp.sum(jnp.where(diag, spread(v_ref[...]), 0.0), axis=1, keepdims=True)
    s_new = s * jnp.exp(spread(ld_ref[...])) + sa * spread(ab_ref[...]) + v_col * spread(k_ref[...])
    s_out[...] = s_new.reshape(H_A, HEAD_DIM, HEAD_DIM)
    y_col = jnp.sum(s_new * spread(r_ref[...]), axis=1, keepdims=True)
    y = _dot_split_rhs(rep_t, jnp.where(diag, y_col, 0.0))[:H_A]
    mean = jnp.mean(y, axis=1, keepdims=True)
    yc = y - mean
    var = jnp.mean(yc * yc, axis=1, keepdims=True)
    yn = yc * lax.rsqrt(var + GN_EPS) * lnw_ref[...] + lnb_ref[...]
    y_ref[...] = ((yn + bonus_ref[...]) * g_ref[...]).astype(BF16)


def _rwkv_step(state, r, ld, k, v, kkn, ab, g, bonus, lnw, lnb):
    DB = state.shape[0]
    heads = lambda a: a.reshape(DB, H_A, HEAD_DIM)
    vec = pl.BlockSpec((None, H_A, HEAD_DIM), lambda b: (b, 0, 0))
    par = pl.BlockSpec((H_A, HEAD_DIM), lambda b: (0, 0))
    st = pl.BlockSpec((None, H_A, HEAD_DIM, HEAD_DIM), lambda b: (b, 0, 0, 0))
    y, s_new = pl.pallas_call(
        _rwkv_step_kernel,
        grid=(DB,),
        in_specs=[st] + [vec] * 8 + [par, par],
        out_specs=[vec, st],
        out_shape=[jax.ShapeDtypeStruct((DB, H_A, HEAD_DIM), BF16), jax.ShapeDtypeStruct(state.shape, F32)],
        compiler_params=_cparams(("parallel",)),
        name="rwkv_step",
    )(state, *(heads(a) for a in (r, ld, k, v, kkn, ab, g, bonus)), lnw.reshape(H_A, HEAD_DIM), lnb.reshape(H_A, HEAD_DIM))
    return y.reshape(DB, D_A), s_new


DSA_PG = 8


def _dsa_sample_select_kernel(topk, n_groups, idx_bits, pt_ref, qi_ref, w_ref, kin_ref, *rest):
    pages = rest[:DSA_PG]
    bias_ref, bias_new_ref, key_scr = rest[DSA_PG:]
    g = pl.program_id(1)
    qi8 = qi_ref[...]
    w8 = w_ref[...]
    allp = jnp.concatenate([p[...] for p in pages], axis=1)
    res = _dot(qi8, allp)
    score = jnp.sum(jnp.maximum(res, 0.0) * w8, axis=0, keepdims=True)
    key = _order_key(score)
    for i in range(DSA_PG):
        key_scr[pl.ds(g * DSA_PG + i, 1), :] = key[:, i * PAGE_SIZE:(i + 1) * PAGE_SIZE]

    @pl.when(g == n_groups - 1)
    def _():
        n_pages = n_groups * DSA_PG
        s_new = jnp.sum(jnp.maximum(_dot_nt(qi8, kin_ref[...]), 0.0) * w8, axis=0, keepdims=True)[:, :1]
        key_new = _order_key(s_new)
        keys = key_scr[...]
        idx = _iota(keys.shape, 0) * PAGE_SIZE + _iota(keys.shape, 1)
        idx_new = n_pages * PAGE_SIZE
        total = lambda m: jnp.sum(jnp.sum(m, axis=1, keepdims=True), axis=0, keepdims=True)
        count_ge = lambda c: total(jnp.where(keys >= c, 1.0, 0.0)) + jnp.where(key_new >= c, 1.0, 0.0)
        thr = _kth_largest_key(count_ge, topk, (1, 1))
        n_gt = total(jnp.where(keys > thr, 1.0, 0.0)) + jnp.where(key_new > thr, 1.0, 0.0)
        need = float(topk) - n_gt

        def bit_body(i, j):
            cand = j + lax.shift_left(jnp.int32(1), idx_bits - 1 - i)
            ties = (total(jnp.where(keys == thr, jnp.where(idx <= cand, 1.0, 0.0), 0.0))
                    + jnp.where(key_new == thr, jnp.where(idx_new <= cand, 1.0, 0.0), 0.0))
            return jnp.where(ties < need, cand, j)

        idx_thr = lax.fori_loop(0, idx_bits, bit_body, jnp.full((1, 1), -1, I32)) + 1
        sel = lambda kk, ii: jnp.where(kk > thr, 0.0, jnp.where(kk == thr, jnp.where(ii <= idx_thr, 0.0, NEG_BIG), NEG_BIG))
        bias_ref[...] = sel(keys, idx)
        bias_new_ref[...] = jnp.where(_iota(bias_new_ref.shape, 1) == 0, sel(key_new, idx_new), NEG_BIG)


def _dsa_sample_select(page_table, qi, wi, ki_new, kidx_t):
    DB, n_pages = page_table.shape
    n_groups = n_pages // DSA_PG
    L = n_pages * PAGE_SIZE + 1
    topk = min(IDX_TOPK_MAX, L // 4)
    idx_bits = int(np.ceil(np.log2(L))) + 1
    per_seq = lambda a: pl.BlockSpec((None,) + a.shape[1:], lambda b, g, pt: (b,) + (0,) * (a.ndim - 1))
    page = lambda i: pl.BlockSpec((None, D_IDX, PAGE_SIZE), lambda b, g, pt: (pt[b, g * DSA_PG + i], 0, 0))
    grid_spec = pltpu.PrefetchScalarGridSpec(
        num_scalar_prefetch=1,
        grid=(DB, n_groups),
        in_specs=[per_seq(qi), per_seq(wi), per_seq(ki_new)] + [page(i) for i in range(DSA_PG)],
        out_specs=[pl.BlockSpec((None, n_pages, PAGE_SIZE), lambda b, g, pt: (b, 0, 0)),
                   pl.BlockSpec((None, 8, LANES), lambda b, g, pt: (b, 0, 0))],
        scratch_shapes=[pltpu.VMEM((n_pages, PAGE_SIZE), I32)],
    )
    return pl.pallas_call(
        functools.partial(_dsa_sample_select_kernel, topk, n_groups, idx_bits),
        grid_spec=grid_spec,
        out_shape=[jax.ShapeDtypeStruct((DB, n_pages, PAGE_SIZE), F32), jax.ShapeDtypeStruct((DB, 8, LANES), F32)],
        compiler_params=_cparams(("parallel", "arbitrary")),
        name="dsa_sample_select",
    )(page_table, qi, wi, ki_new, *([kidx_t] * DSA_PG))


def _keys_on_lanes(cache):
    return jnp.moveaxis(cache, -3, -1)


def _on_lanes(x):
    return jnp.broadcast_to(x.astype(F32)[..., None], x.shape + (LANES,))


def _lane0(x):
    return jnp.pad(x.astype(F32)[..., None], ((0, 0), (0, 0), (0, 0), (0, LANES - 1)))


def _lane_attend_scratch(n_heads, width):
    return [pltpu.VMEM((n_heads, LANES), F32), pltpu.VMEM((n_heads, LANES), F32),
            pltpu.VMEM((n_heads, HEAD_DIM, LANES), F32), pltpu.VMEM((n_heads, width), F32),
            pltpu.VMEM((n_heads, LANES), F32)]


def _lane_attend_init(m_scr, l_scr, acc_scr, p_scr, a_scr):
    m_scr[...] = jnp.full(m_scr.shape, NEG_BIG, F32)
    l_scr[...] = jnp.zeros_like(l_scr)
    acc_scr[...] = jnp.zeros_like(acc_scr)


def _lane_attend(kt_refs, vt_refs, qb_ref, bias, m_scr, l_scr, acc_scr, p_scr, a_scr):
    n_heads = kt_refs[0].shape[0]
    pieces = [(ref_i, g) for ref_i, ref in enumerate(kt_refs) for g in range(ref.shape[2] // LANES)]
    groups = len(pieces)
    width = groups * LANES
    for h in range(n_heads):
        q = qb_ref[h]
        for j, (ref_i, g) in enumerate(pieces):
            p_scr[h:h + 1, j * LANES:(j + 1) * LANES] = jnp.sum(
                kt_refs[ref_i][h, :, g * LANES:(g + 1) * LANES] * q, axis=0, keepdims=True)
    s = p_scr[:, :width] + bias
    m_old = m_scr[...]
    m_new = jnp.maximum(m_old, jnp.max(s, axis=1, keepdims=True))
    alpha = jnp.exp(m_old - m_new)
    p = jnp.exp(s - (jnp.concatenate([m_new] * groups, axis=1) if groups > 1 else m_new))
    l_scr[...] = alpha * l_scr[...] + jnp.sum(p, axis=1, keepdims=True)
    m_scr[...] = m_new
    p_scr[:, :width] = p
    a_scr[...] = alpha
    for h in range(n_heads):
        acc = acc_scr[h] * a_scr[h:h + 1, :]
        for j, (ref_i, g) in enumerate(pieces):
            acc = acc + p_scr[h:h + 1, j * LANES:(j + 1) * LANES] * vt_refs[ref_i][h, :, g * LANES:(g + 1) * LANES]
        acc_scr[h] = acc


def _lane_attend_finish(o_ref, m_scr, l_scr, acc_scr, p_scr, a_scr):
    a_scr[...] = 1.0 / l_scr[...]
    for h in range(o_ref.shape[0]):
        o_ref[h] = jnp.sum(acc_scr[h] * a_scr[h:h + 1, :], axis=1, keepdims=True)


def _dsa_sample_attend_kernel(n_groups, pt_ref, qb_ref, kn_ref, vn_ref, bias_ref, bias_new_ref, *rest):
    kpages = rest[:DSA_PG]
    vpages = rest[DSA_PG:2 * DSA_PG]
    o_ref = rest[2 * DSA_PG]
    state = rest[2 * DSA_PG + 1:]
    g = pl.program_id(1)

    @pl.when(g == 0)
    def _():
        _lane_attend_init(*state)

    bias = jnp.concatenate([bias_ref[i:i + 1, :] for i in range(DSA_PG)], axis=1)
    _lane_attend(list(kpages), list(vpages), qb_ref, bias, *state)

    @pl.when(g == n_groups - 1)
    def _():
        _lane_attend([kn_ref], [vn_ref], qb_ref, bias_new_ref[0:1, :], *state)
        _lane_attend_finish(o_ref, *state)


def _dsa_sample_attend(page_table, q, k_new, v_new, bias, bias_new, cache_k, cache_v):
    DB, n_pages = page_table.shape
    n_groups = n_pages // DSA_PG
    per_seq = pl.BlockSpec((None, H_B, HEAD_DIM, LANES), lambda b, g, pt: (b, 0, 0, 0))
    page = lambda i: pl.BlockSpec((None, H_B, HEAD_DIM, PAGE_SIZE), lambda b, g, pt: (pt[b, g * DSA_PG + i], 0, 0, 0))
    grid_spec = pltpu.PrefetchScalarGridSpec(
        num_scalar_prefetch=1,
        grid=(DB, n_groups),
        in_specs=[per_seq, per_seq, per_seq,
                  pl.BlockSpec((None, DSA_PG, PAGE_SIZE), lambda b, g, pt: (b, g, 0)),
                  pl.BlockSpec((None, 8, LANES), lambda b, g, pt: (b, 0, 0))] + [page(i) for i in range(DSA_PG)] * 2,
        out_specs=pl.BlockSpec((None, H_B, HEAD_DIM, 1), lambda b, g, pt: (b, 0, 0, 0)),
        scratch_shapes=_lane_attend_scratch(H_B, DSA_PG * PAGE_SIZE),
    )
    ck, cv = _keys_on_lanes(cache_k), _keys_on_lanes(cache_v)
    return pl.pallas_call(
        functools.partial(_dsa_sample_attend_kernel, n_groups),
        grid_spec=grid_spec,
        out_shape=jax.ShapeDtypeStruct((DB, H_B, HEAD_DIM, 1), F32),
        compiler_params=_cparams(("parallel", "arbitrary")),
        name="dsa_sample_attend",
    )(page_table, _on_lanes(q), _lane0(k_new), _lane0(v_new), bias, bias_new, *([ck] * DSA_PG), *([cv] * DSA_PG))


DIL_CHUNK = 512


def _dilated_sample_kernel(w_len, qb_ref, kn_ref, vn_ref, kc_ref, vc_ref, o_ref, *state):
    c = pl.program_id(1)

    @pl.when(c == 0)
    def _():
        _lane_attend_init(*state)

    width = kc_ref.shape[2]
    dist = w_len - (c * width + _iota((1, width), 1))
    count = jnp.zeros((1, width), F32)
    for window, dil in C_PATTERNS:
        count = count + jnp.where(dist <= window, jnp.where(dist % dil == 0, 1.0, 0.0), 0.0)
    bias = jnp.where(count > 0.0, jnp.log(jnp.maximum(count, 1.0)), NEG_BIG)
    _lane_attend([kc_ref], [vc_ref], qb_ref, bias, *state)

    @pl.when(c == pl.num_programs(1) - 1)
    def _():
        bias_new = jnp.where(_iota((1, LANES), 1) == 0, float(np.log(len(C_PATTERNS))), NEG_BIG)
        _lane_attend([kn_ref], [vn_ref], qb_ref, bias_new, *state)
        _lane_attend_finish(o_ref, *state)


def _dilated_sample(q, k_new, v_new, cache_k, cache_v):
    DB, w_len = cache_k.shape[:2]
    width = min(DIL_CHUNK, w_len)
    per_seq = pl.BlockSpec((None, H_C, HEAD_DIM, LANES), lambda b, c: (b, 0, 0, 0))
    chunk = pl.BlockSpec((None, H_C, HEAD_DIM, width), lambda b, c: (b, 0, 0, c))
    return pl.pallas_call(
        functools.partial(_dilated_sample_kernel, w_len),
        grid=(DB, w_len // width),
        in_specs=[per_seq, per_seq, per_seq, chunk, chunk],
        out_specs=pl.BlockSpec((None, H_C, HEAD_DIM, 1), lambda b, c: (b, 0, 0, 0)),
        out_shape=jax.ShapeDtypeStruct((DB, H_C, HEAD_DIM, 1), F32),
        scratch_shapes=_lane_attend_scratch(H_C, width),
        compiler_params=_cparams(("parallel", "arbitrary")),
        name="dilated_sample",
    )(_on_lanes(q), _lane0(k_new), _lane0(v_new), _keys_on_lanes(cache_k), _keys_on_lanes(cache_v))


def _silu(x):
    return x * jax.nn.sigmoid(x)


def _outproj_ffn_kernel(x_ref, ya_ref, yb_ref, wo_ref, g_ref, wg_ref, wu_ref, wd_ref, o_ref, h_scr, hn_scr, acc_scr):
    j = pl.program_id(1)

    @pl.when(j == 0)
    def _():
        y = jnp.concatenate([ya_ref[...], yb_ref[...]], axis=1)
        h = x_ref[...] + jnp.dot(y, wo_ref[...], preferred_element_type=F32)
        h_scr[...] = h
        hn_scr[...] = _rms(h, g_ref[...]).astype(BF16)
        acc_scr[...] = jnp.zeros_like(acc_scr)

    hn = hn_scr[...]
    act = _silu(jnp.dot(hn, wg_ref[...], preferred_element_type=F32)) * jnp.dot(hn, wu_ref[...], preferred_element_type=F32)
    acc_scr[...] += jnp.dot(act.astype(BF16), wd_ref[...], preferred_element_type=F32)

    @pl.when(j == pl.num_programs(1) - 1)
    def _():
        o_ref[...] = h_scr[...] + acc_scr[...]


def _outproj_ffn(x, ya, yb, wo, g, wg, wu, wd, tm, tf):
    N = x.shape[0]
    nf = D_FF // tf
    tok = lambda width: pl.BlockSpec((tm, width), lambda i, j: (i, 0))
    full = lambda a: pl.BlockSpec(a.shape, lambda i, j: (0,) * a.ndim)
    return pl.pallas_call(
        _outproj_ffn_kernel,
        grid=(N // tm, nf),
        in_specs=[tok(D_MODEL), tok(D_A), tok(D_B), full(wo), full(g),
                  pl.BlockSpec((D_MODEL, tf), lambda i, j: (0, j)), pl.BlockSpec((D_MODEL, tf), lambda i, j: (0, j)),
                  pl.BlockSpec((tf, D_MODEL), lambda i, j: (j, 0))],
        out_specs=tok(D_MODEL),
        out_shape=jax.ShapeDtypeStruct((N, D_MODEL), F32),
        scratch_shapes=[pltpu.VMEM((tm, D_MODEL), F32), pltpu.VMEM((tm, D_MODEL), BF16), pltpu.VMEM((tm, D_MODEL), F32)],
        compiler_params=_cparams(("parallel", "arbitrary")),
        name="outproj_ffn",
    )(x, ya, yb, wo, g, wg, wu, wd)


def _inproj1_kernel(keep_tiles, x_ref, g_ref, w_ref, cos_ref, sin_ref, q_ref, k_ref, v_ref, kf_ref, vf_ref, *streams):
    h = _rms(x_ref[...], g_ref[...]).astype(BF16)
    p = jnp.dot(h, w_ref[...], preferred_element_type=F32)
    cos = cos_ref[...]
    sin = sin_ref[...]
    q = _rope(p[:, :D_C], cos, sin) * (HEAD_DIM ** -0.5)
    k = _rope(p[:, D_C:2 * D_C], cos, sin)
    v = p[:, 2 * D_C:]
    q_ref[...] = q.astype(BF16)
    k_ref[...] = k.astype(BF16)
    v_ref[...] = v.astype(BF16)
    if keep_tiles is None:
        kf_ref[...] = k
        vf_ref[...] = v
    else:
        scr = streams[-1]
        n_groups, tm, _ = scr.shape
        n_dil = len(C_PATTERNS) - 1
        for which, val in enumerate((q, k, v)):
            for g in range(n_groups):
                scr[g] = val[:, g * LANES:(g + 1) * LANES]
            for di, (_, dil) in enumerate(C_PATTERNS[1:]):
                out = streams[which * n_dil + di]
                for r in range(dil):
                    for g in range(n_groups):
                        out[:, r * D_C + g * LANES:r * D_C + (g + 1) * LANES] = (
                            scr[g, pl.ds(r, tm // dil, stride=dil), :].astype(BF16))

        @pl.when(pl.program_id(1) >= pl.num_programs(1) - keep_tiles)
        def _():
            kf_ref[...] = k.T
            vf_ref[...] = v.T


def _inproj1(x, g, w, cos, sin, tm, keep=None):
    B, T, _ = x.shape
    tok = pl.BlockSpec((None, tm, D_C), lambda b, t: (b, t, 0))
    full = lambda a: pl.BlockSpec(a.shape, lambda b, t: (0,) * a.ndim)
    tab = pl.BlockSpec((tm, LANES), lambda b, t: (t, 0))
    bf = jax.ShapeDtypeStruct((B, T, D_C), BF16)
    extra_specs, extra_shapes, scratch = [], [], []
    if keep is None:
        keep_tiles, f_spec, ff = None, tok, jax.ShapeDtypeStruct((B, T, D_C), F32)
    else:
        keep_tiles = keep // tm
        first = T // tm - keep_tiles
        f_spec = pl.BlockSpec((None, D_C, tm), lambda b, t: (b, 0, jnp.maximum(t - first, 0)))
        ff = jax.ShapeDtypeStruct((B, D_C, keep), F32)
        for _ in range(3):
            for _, dil in C_PATTERNS[1:]:
                extra_specs.append(pl.BlockSpec((None, tm // dil, dil * D_C), lambda b, t: (b, t, 0)))
                extra_shapes.append(jax.ShapeDtypeStruct((B, T // dil, dil * D_C), BF16))
        scratch = [pltpu.VMEM((D_C // LANES, tm, LANES), F32)]
    return pl.pallas_call(
        functools.partial(_inproj1_kernel, keep_tiles),
        grid=(B, T // tm),
        in_specs=[tok, full(g), full(w), tab, tab],
        out_specs=[tok, tok, tok, f_spec, f_spec] + extra_specs,
        out_shape=[bf, bf, bf, ff, ff] + extra_shapes,
        scratch_shapes=scratch,
        compiler_params=_cparams(("parallel", "arbitrary")),
        name="inproj1",
    )(x, g, w, cos, sin)


def _dilated_kernel(lookback, q_ref, kc_ref, kp_ref, vc_ref, vp_ref, o_ref, lse_ref):
    QB = C_BLOCK
    c = pl.program_id(2)
    lane128 = _iota((QB, LANES), 1)
    upper = (lane128 // HEAD_DIM) == 1
    upper_v = (_iota((2 * QB, LANES), 1) // HEAD_DIM) == 1
    qi = _iota((QB, 2 * QB), 0)
    kj = _iota((QB, 2 * QB), 1)
    dist = qi + QB - kj
    ok = (dist >= 0) & (dist <= lookback) & ((kj >= QB) | (c > 0))
    bias = jnp.where(ok, 0.0, NEG_BIG)
    q = q_ref[...]
    lse_blk = jnp.zeros((QB, LANES), F32)
    outs = []
    for j in range(H_C // 2):
        sl = slice(j * LANES, (j + 1) * LANES)
        pair = q[:, sl]
        q2 = jnp.concatenate([jnp.where(upper, jnp.zeros_like(pair), pair),
                              jnp.where(upper, pair, jnp.zeros_like(pair))], axis=0)
        k2 = jnp.concatenate([kp_ref[:, sl], kc_ref[:, sl]], axis=0)
        v2 = jnp.concatenate([vp_ref[:, sl], vc_ref[:, sl]], axis=0)
        s2 = _dot_nt(q2, k2)
        vv = jnp.concatenate([jnp.where(upper_v, jnp.zeros_like(v2), v2),
                              jnp.where(upper_v, v2, jnp.zeros_like(v2))], axis=0)
        ps, ls = [], []
        for u in range(2):
            s = s2[u * QB:(u + 1) * QB] + bias
            m = jnp.max(s, axis=1, keepdims=True)
            p = jnp.exp(s - m)
            l = jnp.sum(p, axis=1, keepdims=True)
            ps.append(p.astype(BF16))
            ls.append(l)
            lse_blk = lse_blk + jnp.where(lane128 == 2 * j + u, m + jnp.log(l), 0.0)
        pv = jnp.dot(jnp.concatenate(ps, axis=1), vv, preferred_element_type=F32)
        outs.append(pv / jnp.where(upper, ls[1], ls[0]))
    o_ref[...] = jnp.concatenate(outs, axis=1).astype(BF16)
    lse_ref[...] = lse_blk


def _dilated_branch(q, k, v, window, dil):
    B, n, _ = q.shape
    view = lambda a: a
    cur = pl.BlockSpec((None, C_BLOCK, D_C), lambda b, r, c: (b, c, r))
    prev = pl.BlockSpec((None, C_BLOCK, D_C), lambda b, r, c: (b, jnp.maximum(c - 1, 0), r))
    o, lse = pl.pallas_call(
        functools.partial(_dilated_kernel, window // dil),
        grid=(B, dil, n // C_BLOCK),
        in_specs=[cur, cur, prev, cur, prev],
        out_specs=[cur, pl.BlockSpec((None, C_BLOCK, LANES), lambda b, r, c: (b, c, r))],
        out_shape=[jax.ShapeDtypeStruct((B, n, dil * D_C), BF16), jax.ShapeDtypeStruct((B, n, dil * LANES), F32)],
        compiler_params=_cparams(("parallel", "parallel", "arbitrary")),
        name="dilated_w%d_d%d" % (window, dil),
    )(view(q), view(k), view(k), view(v), view(v))
    return o, lse


def _merge_outproj_kernel(x_ref, o1_ref, o2_ref, o3_ref, l1_ref, l2_ref, l3_ref, ex_ref, wo_ref, out_ref,
                          o_scr, l_scr):
    tm = x_ref.shape[0]

    def in_token_order(o_ref, l_ref, slot, dil):
        if dil == 1:
            return o_ref[...].astype(F32), l_ref[...]
        n_groups = D_C // LANES
        for r in range(dil):
            rows = pl.ds(r, tm // dil, stride=dil)
            for g in range(n_groups):
                o_scr[slot, g, rows, :] = o_ref[:, r * D_C + g * LANES:r * D_C + (g + 1) * LANES].astype(F32)
            l_scr[slot, rows, :] = l_ref[:, r * LANES:(r + 1) * LANES]
        return jnp.concatenate([o_scr[slot, g] for g in range(n_groups)], axis=1), l_scr[slot]

    branches = [in_token_order(o_ref, l_ref, i - 1, dil) for i, (o_ref, l_ref, (_, dil)) in
                enumerate(zip((o1_ref, o2_ref, o3_ref), (l1_ref, l2_ref, l3_ref), C_PATTERNS))]
    lses = [l for _, l in branches]
    m = jnp.maximum(jnp.maximum(lses[0], lses[1]), lses[2])
    es = [jnp.exp(l - m) for l in lses]
    inv = 1.0 / (es[0] + es[1] + es[2])
    y = jnp.zeros(x_ref.shape, F32)
    for e, (o, _) in zip(es, branches):
        y = y + _dot_split_lhs(e * inv, ex_ref[...]) * o
    out_ref[...] = x_ref[...] + jnp.dot(y.astype(BF16), wo_ref[...], preferred_element_type=F32)


def _head_expand_matrix():
    e = np.zeros((LANES, D_C), np.float32)
    for h in range(H_C):
        e[h, h * HEAD_DIM:(h + 1) * HEAD_DIM] = 1.0
    return jnp.asarray(e, BF16)


def _merge_outproj(x, os_, lses, wo, tm):
    N = x.shape[0]
    tok = lambda width: pl.BlockSpec((tm, width), lambda i: (i, 0))
    full = lambda a: pl.BlockSpec(a.shape, lambda i: (0,) * a.ndim)
    stream = lambda width, dil: pl.BlockSpec((tm // dil, dil * width), lambda i: (i, 0))
    ex = _head_expand_matrix()
    dils = [dil for _, dil in C_PATTERNS]
    return pl.pallas_call(
        _merge_outproj_kernel,
        grid=(N // tm,),
        in_specs=([tok(D_MODEL)] + [stream(D_C, d) for d in dils] + [stream(LANES, d) for d in dils]
                  + [full(ex), full(wo)]),
        out_specs=tok(D_MODEL),
        out_shape=jax.ShapeDtypeStruct((N, D_MODEL), F32),
        scratch_shapes=[pltpu.VMEM((2, D_C // LANES, tm, LANES), F32), pltpu.VMEM((2, tm, LANES), F32)],
        compiler_params=_cparams(("parallel",)),
        name="merge_outproj",
    )(x, *os_, *lses, ex, wo)


def _top2_gates(logits):
    lane = _iota(logits.shape, 1)
    m1 = jnp.max(logits, axis=1, keepdims=True)
    i1 = jnp.min(jnp.where(logits == m1, lane, LANES), axis=1, keepdims=True)
    rest = jnp.where(lane == i1, -jnp.inf, logits)
    m2 = jnp.max(rest, axis=1, keepdims=True)
    i2 = jnp.min(jnp.where(rest == m2, lane, LANES), axis=1, keepdims=True)
    e2 = jnp.exp(m2 - m1)
    g1 = 1.0 / (1.0 + e2)
    return jnp.where(lane == i1, g1, 0.0) + jnp.where(lane == i2, e2 * g1, 0.0)


def _moe_dense_kernel(x_ref, g_ref, rw_ref, rb_ref, wg_ref, wu_ref, wd_ref, gf_ref, o_ref, hn_scr, gate_scr, acc_scr):
    e = pl.program_id(1)
    j = pl.program_id(2)

    @pl.when((e == 0) & (j == 0))
    def _():
        hn = _rms(x_ref[...], g_ref[...])
        hn_scr[...] = hn.astype(BF16)
        logits = jnp.dot(hn.astype(BF16), rw_ref[...], preferred_element_type=F32) + rb_ref[...]
        gate_scr[...] = _top2_gates(logits)
        acc_scr[...] = jnp.zeros_like(acc_scr)

    hn = hn_scr[...]
    gate = gate_scr[...]
    gate_e = jnp.sum(jnp.where(_iota(gate.shape, 1) == e, gate, 0.0), axis=1, keepdims=True)
    act = _silu(jnp.dot(hn, wg_ref[...], preferred_element_type=F32)) * jnp.dot(hn, wu_ref[...], preferred_element_type=F32)
    acc_scr[...] += gate_e * jnp.dot(act.astype(BF16), wd_ref[...], preferred_element_type=F32)

    @pl.when((e == pl.num_programs(1) - 1) & (j == pl.num_programs(2) - 1))
    def _():
        o_ref[...] = _rms(x_ref[...] + acc_scr[...], gf_ref[...])


def _moe_dense(x, g, rw, rb, wg, wu, wd, gf, tm, tf):
    N = x.shape[0]
    tok = pl.BlockSpec((tm, D_MODEL), lambda i, e, j: (i, 0))
    full = lambda a: pl.BlockSpec(a.shape, lambda i, e, j: (0,) * a.ndim)
    return pl.pallas_call(
        _moe_dense_kernel,
        grid=(N // tm, N_EXPERTS, D_FF_EXPERT // tf),
        in_specs=[tok, full(g), full(rw), full(rb),
                  pl.BlockSpec((None, D_MODEL, tf), lambda i, e, j: (e, 0, j)),
                  pl.BlockSpec((None, D_MODEL, tf), lambda i, e, j: (e, 0, j)),
                  pl.BlockSpec((None, tf, D_MODEL), lambda i, e, j: (e, j, 0)), full(gf)],
        out_specs=tok,
        out_shape=jax.ShapeDtypeStruct((N, D_MODEL), F32),
        scratch_shapes=[pltpu.VMEM((tm, D_MODEL), BF16), pltpu.VMEM((tm, LANES), F32), pltpu.VMEM((tm, D_MODEL), F32)],
        compiler_params=_cparams(("parallel", "arbitrary", "arbitrary")),
        name="moe_dense",
    )(x, g, rw, rb, wg, wu, wd, gf)


MOE_TR = 1024
MOE_RC = 128
MOE_CAP = 2 * MOE_TR + N_EXPERTS * MOE_RC
MOE_PAD = 16


def _moe_route_kernel(x_ref, g_ref, rw_ref, rb_ref, hn_ref, gate_ref, pos_ref, rankt_ref, cnt_ref):
    TR = x_ref.shape[0]
    hn = _rms(x_ref[...], g_ref[...]).astype(BF16)
    hn_ref[...] = hn
    gate = _top2_gates(jnp.dot(hn, rw_ref[...], preferred_element_type=F32) + rb_ref[...])
    gate_ref[...] = gate
    sel = jnp.where(gate.T[:MOE_PAD] > 0.0, 1.0, 0.0)
    triu = jnp.where(_iota((LANES, LANES), 0) <= _iota((LANES, LANES), 1), 1.0, 0.0).astype(BF16)
    carry = jnp.zeros((MOE_PAD, 1), F32)
    ranks = []
    for c in range(TR // LANES):
        blk = sel[:, c * LANES:(c + 1) * LANES]
        pref = jnp.dot(blk.astype(BF16), triu, preferred_element_type=F32)
        ranks.append(jnp.where(blk > 0.0, carry + pref - 1.0, -1.0))
        carry = carry + pref[:, LANES - 1:]
    rank_t = jnp.concatenate(ranks, axis=1)
    rankt_ref[...] = rank_t[:N_EXPERTS]
    padded = jnp.ceil(carry * (1.0 / MOE_RC)) * MOE_RC
    lower = jnp.where(_iota((MOE_PAD, MOE_PAD), 0) > _iota((MOE_PAD, MOE_PAD), 1), 1.0, 0.0).astype(BF16)
    offs = jnp.dot(lower, jnp.broadcast_to(padded, (MOE_PAD, LANES)).astype(BF16), preferred_element_type=F32)
    cnt_ref[0:N_EXPERTS, :] = jnp.broadcast_to(carry[:N_EXPERTS], (N_EXPERTS, LANES))
    cnt_ref[N_EXPERTS:, :] = offs[:N_EXPERTS]
    pos_t = jnp.where(rank_t >= 0.0, rank_t + offs[:, :1], -1.0)
    p1 = jnp.max(pos_t, axis=0, keepdims=True)
    p2 = jnp.max(jnp.where(pos_t == p1, -1.0, pos_t), axis=0, keepdims=True)
    rows = _iota((LANES, TR), 0)
    pos_ref[...] = jnp.where(rows == 0, p1, jnp.where(rows == 1, p2, -1.0)).T


def _moe_route(x, g, rw, rb):
    N = x.shape[0]
    nt = N // MOE_TR
    tok = lambda width: pl.BlockSpec((MOE_TR, width), lambda i: (i, 0))
    full = lambda a: pl.BlockSpec(a.shape, lambda i: (0,) * a.ndim)
    return pl.pallas_call(
        _moe_route_kernel,
        grid=(nt,),
        in_specs=[tok(D_MODEL), full(g), full(rw), full(rb)],
        out_specs=[tok(D_MODEL), tok(LANES), tok(LANES), pl.BlockSpec((N_EXPERTS, MOE_TR), lambda i: (0, i)),
                   pl.BlockSpec((None, 2 * N_EXPERTS, LANES), lambda i: (i, 0, 0))],
        out_shape=[jax.ShapeDtypeStruct((N, D_MODEL), BF16), jax.ShapeDtypeStruct((N, LANES), F32),
                   jax.ShapeDtypeStruct((N, LANES), F32), jax.ShapeDtypeStruct((N_EXPERTS, N), F32),
                   jax.ShapeDtypeStruct((nt, 2 * N_EXPERTS, LANES), F32)],
        compiler_params=_cparams(("parallel",)),
        name="moe_route",
    )(x, g, rw, rb)


def _moe_routed_kernel(meta_ref, x_ref, hn_ref, gate_ref, pos_ref, rankt_ref, wg_ref, wu_ref, wd_ref, gf_ref, o_ref,
                       xc_scr, y_scr, grow_scr):
    i = pl.program_id(0)
    e = pl.program_id(1)
    j = pl.program_id(2)
    TR, RC = MOE_TR, MOE_RC
    n_chunks = lax.div(meta_ref[i, e] + (RC - 1), RC)
    off = meta_ref[i, N_EXPERTS + e]

    @pl.when((e == 0) & (j == 0))
    def _():
        y_scr[...] = jnp.zeros_like(y_scr)
        grow_scr[...] = jnp.zeros_like(grow_scr)

    @pl.when(j == 0)
    def _():
        rank_row = rankt_ref[pl.ds(e, 1), :]
        gate = gate_ref[...]
        gate_e = jnp.where(_iota(gate.shape, 1) == e, gate, 0.0)

        def gather(c, carry):
            rows = pl.multiple_of(off + c * RC, RC)
            want = (c * RC + _iota((RC, 1), 0)).astype(F32)
            onehot = jnp.where(rank_row == want, 1.0, 0.0).astype(BF16)
            xc_scr[pl.ds(rows, RC), :] = jnp.dot(onehot, hn_ref[...], preferred_element_type=F32).astype(BF16)
            grow_scr[pl.ds(rows, RC), :] = _dot_split_rhs(onehot, gate_e)
            return carry

        lax.fori_loop(0, n_chunks, gather, 0)

    def expert(c, carry):
        rows = pl.multiple_of(off + c * RC, RC)
        xc = xc_scr[pl.ds(rows, RC), :]
        act = _silu(jnp.dot(xc, wg_ref[...], preferred_element_type=F32)) * jnp.dot(xc, wu_ref[...], preferred_element_type=F32)
        y_scr[pl.ds(rows, RC), :] += jnp.dot(act.astype(BF16), wd_ref[...], preferred_element_type=F32)
        return carry

    lax.fori_loop(0, n_chunks, expert, 0)

    @pl.when((e == pl.num_programs(1) - 1) & (j == pl.num_programs(2) - 1))
    def _():
        step = 256
        for r in range(MOE_CAP // step):
            sl = slice(r * step, (r + 1) * step)
            gr = jnp.sum(grow_scr[sl, :], axis=1, keepdims=True)
            xc_scr[sl, :] = (y_scr[sl, :] * gr).astype(BF16)
        lane = _iota((step, MOE_CAP), 1).astype(F32)
        for r in range(TR // step):
            sl = slice(r * step, (r + 1) * step)
            pos = pos_ref[sl, :]
            scatter = jnp.where(lane == pos[:, 0:1], 1.0, jnp.where(lane == pos[:, 1:2], 1.0, 0.0)).astype(BF16)
            y = jnp.dot(scatter, xc_scr[...], preferred_element_type=F32)
            o_ref[sl, :] = _rms(x_ref[sl, :] + y, gf_ref[...])


def _moe_routed(x, g, rw, rb, wg, wu, wd, gf, tf):
    N = x.shape[0]
    nt = N // MOE_TR
    hn, gate, pos, rank_t, meta = _moe_route(x, g, rw, rb)
    meta = meta[:, :, 0].astype(I32)
    tok = lambda width: pl.BlockSpec((MOE_TR, width), lambda i, e, j, m: (i, 0))
    full = lambda a: pl.BlockSpec(a.shape, lambda i, e, j, m: (0,) * a.ndim)
    grid_spec = pltpu.PrefetchScalarGridSpec(
        num_scalar_prefetch=1,
        grid=(nt, N_EXPERTS, D_FF_EXPERT // tf),
        in_specs=[tok(D_MODEL), tok(D_MODEL), tok(LANES), tok(LANES),
                  pl.BlockSpec((N_EXPERTS, MOE_TR), lambda i, e, j, m: (0, i)),
                  pl.BlockSpec((None, D_MODEL, tf), lambda i, e, j, m: (e, 0, j)),
                  pl.BlockSpec((None, D_MODEL, tf), lambda i, e, j, m: (e, 0, j)),
                  pl.BlockSpec((None, tf, D_MODEL), lambda i, e, j, m: (e, j, 0)), full(gf)],
        out_specs=tok(D_MODEL),
        scratch_shapes=[pltpu.VMEM((MOE_CAP, D_MODEL), BF16), pltpu.VMEM((MOE_CAP, D_MODEL), F32),
                        pltpu.VMEM((MOE_CAP, LANES), F32)],
    )
    return pl.pallas_call(
        _moe_routed_kernel,
        grid_spec=grid_spec,
        out_shape=jax.ShapeDtypeStruct((N, D_MODEL), F32),
        compiler_params=_cparams(("parallel", "arbitrary", "arbitrary")),
        name="moe_routed",
    )(meta, x, hn, gate, pos, rank_t, wg, wu, wd, gf)


def _outproj_kernel(x_ref, y_ref, wo_ref, o_ref):
    o_ref[...] = x_ref[...] + jnp.dot(y_ref[...], wo_ref[...], preferred_element_type=F32)


def _outproj(x, y, wo, tm):
    N = x.shape[0]
    tok = lambda width: pl.BlockSpec((tm, width), lambda i: (i, 0))
    return pl.pallas_call(
        _outproj_kernel,
        grid=(N // tm,),
        in_specs=[tok(D_MODEL), tok(y.shape[1]), pl.BlockSpec(wo.shape, lambda i: (0, 0))],
        out_specs=tok(D_MODEL),
        out_shape=jax.ShapeDtypeStruct((N, D_MODEL), F32),
        compiler_params=_cparams(("parallel",)),
        name="outproj",
    )(x, y, wo)


def _pad_router(router_w, router_b):
    rw = jnp.concatenate([router_w, jnp.zeros((D_MODEL, LANES - N_EXPERTS), router_w.dtype)], axis=1).astype(BF16)
    rb = jnp.concatenate([router_b.astype(F32), jnp.full((LANES - N_EXPERTS,), NEG_BIG, F32)]).reshape(1, LANES)
    return rw, rb


def _pad_w_in0(w_in_0):
    o = A_COLS + 3 * D_B + H_IDX * D_IDX
    wi = w_in_0[:, o:o + H_IDX]
    ki = w_in_0[:, o + H_IDX:o + H_IDX + D_IDX]
    pad = jnp.zeros((D_MODEL, LANES - D_IDX - H_IDX), w_in_0.dtype)
    return jnp.concatenate([w_in_0[:, :o], ki, wi, pad], axis=1).astype(BF16)


def _rwkv_params(w):
    row = lambda a: a.reshape(1, -1).astype(F32)
    zeros = jnp.zeros((DECAY_LORA, D_A), F32)
    return {
        "mu": row(w["a_mu"]), "w0": row(w["a_w0"]), "a0": row(w["a_a0"]),
        "w2": jnp.concatenate([w["a_w2"], zeros], axis=0).astype(BF16),
        "a2": jnp.concatenate([zeros, w["a_a2"]], axis=0).astype(BF16),
        "g2": w["a_g2"].astype(BF16),
        "kk": row(w["a_kk"]), "ka": row(w["a_ka"]), "rk": row(w["a_rk"]),
        "bd": _block_diag_ones(D_A), "lnw": row(w["a_ln_w"]), "lnb": row(w["a_ln_b"]),
    }


DSA_QB = 128


def _order_key(score):
    bits = lax.bitcast_convert_type(score, I32)
    key = jnp.where(bits < 0, bits ^ jnp.int32(0x7FFFFFFF), bits)
    return jnp.where(score == 0.0, 0, key)


def _fold_lanes(x):
    part = x[:, :LANES]
    for j in range(1, x.shape[1] // LANES):
        part = part + x[:, j * LANES:(j + 1) * LANES]
    return part


def _fold_rows(x, rows=64):
    rows = min(rows, x.shape[0])
    part = x[:rows]
    for j in range(1, x.shape[0] // rows):
        part = part + x[j * rows:(j + 1) * rows]
    return part


def _kth_largest_key(count, topk, shape):
    kf = float(topk)
    base = jnp.where(count(jnp.zeros(shape, I32)) >= kf, 0, INT_MIN).astype(I32)

    def bit_body(i, base):
        cand = base + lax.shift_left(jnp.int32(1), 30 - i)
        return jnp.where(count(cand) >= kf, cand, base)

    return lax.fori_loop(0, 31, bit_body, base)


def _dsa_prompt_kernel(topk, KB, KA, idx_bits, qt_ref, k_ref, vt_ref, qit_ref, wit_ref, ki_ref, o_ref,
                       key_scr, thr_scr, s_scr, bias_scr, p_scr, m_scr, l_scr, acc_scr, half_scr):
    QB = DSA_QB
    q_pos0 = pl.program_id(1) * QB
    nkb = lax.div(q_pos0 + QB - 1, KB) + 1
    nka = lax.div(q_pos0 + QB - 1, KA) + 1
    key_row = _iota((KB, QB), 0)
    q_lane = _iota((KB, QB), 1)

    qit = qit_ref[...]
    q_cat = jnp.concatenate([qit[h * D_IDX:(h + 1) * D_IDX] for h in range(H_IDX)], axis=1)
    wit = wit_ref[...]
    w_cat = jnp.concatenate([wit[h:h + 1] for h in range(H_IDX)], axis=1)

    def score_body(kb, carry):
        off = pl.multiple_of(kb * KB, KB)
        res = jnp.dot(ki_ref[pl.ds(off, KB), :], q_cat, preferred_element_type=F32)
        weighted = jnp.maximum(res, 0.0) * w_cat
        admissible = (off + key_row) <= (q_pos0 + q_lane)
        key = jnp.where(admissible, _order_key(_fold_lanes(weighted)), INT_MIN)
        key_scr[pl.ds(off, KB), :] = key
        half_scr[pl.ds(off, KB), :] = lax.shift_right_arithmetic(key, 16).astype(I16)
        return carry

    lax.fori_loop(0, nkb, score_body, 0)

    def count_where(pred):
        def body(kb, acc):
            off = pl.multiple_of(kb * KB, KB)
            return acc + _fold_rows(pred(key_scr[pl.ds(off, KB), :], off + key_row))
        acc = lax.fori_loop(0, nkb, body, jnp.zeros((min(64, KB), QB), F32))
        return jnp.sum(acc, axis=0, keepdims=True)

    def kth_largest_half():
        one, zero = jnp.int16(1), jnp.int16(0)

        def bit_body(i, base):
            cand = base + lax.shift_left(jnp.int32(1), 15 - i)
            cand16 = cand.astype(I16)

            def body(kb, acc):
                off = pl.multiple_of(kb * KB, KB)
                return acc + _fold_rows(jnp.where(half_scr[pl.ds(off, KB), :] >= cand16, one, zero))
            acc = lax.fori_loop(0, nkb, body, jnp.zeros((min(64, KB), QB), I16))
            n = jnp.sum(acc.astype(F32), axis=0, keepdims=True)
            return jnp.where(n >= float(topk), cand, base)

        return lax.fori_loop(0, 16, bit_body, jnp.full((1, QB), -(2 ** 15), I32))

    hi = kth_largest_half()

    def low_body(kb, carry):
        off = pl.multiple_of(kb * KB, KB)
        key = key_scr[pl.ds(off, KB), :]
        key_hi = lax.shift_right_arithmetic(key, 16)
        low = (key & 0xFFFF) - 2 ** 15
        half_scr[pl.ds(off, KB), :] = jnp.where(key_hi > hi, 2 ** 15 - 1,
                                                jnp.where(key_hi < hi, -(2 ** 15), low)).astype(I16)
        return carry

    lax.fori_loop(0, nkb, low_body, 0)
    thr = lax.shift_left(hi, 16) + (kth_largest_half() + 2 ** 15)
    count_ge = lambda cand: count_where(lambda blk, idx: jnp.where(blk >= cand, 1.0, 0.0))
    n_gt = count_where(lambda blk, idx: jnp.where(blk > thr, 1.0, 0.0))
    n_ge = count_ge(thr)
    need = float(topk) - n_gt
    thr_scr[...] = jnp.full((8, QB), 2 ** 30, I32)

    @pl.when(jnp.max(n_ge - float(topk)) > 0.0)
    def _():
        def bit_body(i, j):
            cand = j + lax.shift_left(jnp.int32(1), idx_bits - 1 - i)
            ties = count_where(lambda blk, idx: jnp.where(blk == thr, jnp.where(idx <= cand, 1.0, 0.0), 0.0))
            return jnp.where(ties < need, cand, j)
        j = lax.fori_loop(0, idx_bits, bit_body, jnp.full((1, QB), -1, I32))
        thr_scr[...] = jnp.broadcast_to(j + 1, (8, QB))

    idx_thr = thr_scr[0:1, :]
    tie_bias = jnp.where(thr == INT_MIN, NEG_BIG, 0.0)

    qt = qt_ref[...]
    feat = _iota((LANES, QB), 0)
    qtm = []
    for h in range(H_B):
        pair = qt[(h // 2) * LANES:(h // 2 + 1) * LANES]
        qtm.append(jnp.where((feat // HEAD_DIM) == (h % 2), pair, jnp.zeros_like(pair)))
    q_pairs = [jnp.concatenate([qtm[2 * j], qtm[2 * j + 1]], axis=1) for j in range(H_B // 2)]
    key_row_a = _iota((KA, QB), 0)

    def scores_into(slot, ka):
        off = pl.multiple_of(ka * KA, KA)
        for j in range(H_B // 2):
            s2 = jnp.dot(k_ref[pl.ds(off, KA), j * LANES:(j + 1) * LANES], q_pairs[j], preferred_element_type=F32)
            s_scr[slot, 2 * j] = s2[:, :QB]
            s_scr[slot, 2 * j + 1] = s2[:, QB:]

    scores_into(0, 0)

    m_scr[...] = jnp.full(m_scr.shape, NEG_BIG, F32)
    l_scr[...] = jnp.zeros_like(l_scr)
    acc_scr[...] = jnp.zeros_like(acc_scr)
    CH = 64
    n_ch = KA // CH

    def attend_block(slot, ka):
        live = ka < nka
        ka = jnp.minimum(ka, nka - 1)
        off = pl.multiple_of(ka * KA, KA)
        blk = key_scr[pl.ds(off, KA), :]
        hit = jnp.where(live, 0.0, NEG_BIG)
        bias_scr[...] = jnp.where(blk > thr, hit, jnp.where(
            blk == thr, jnp.where((off + key_row_a) <= idx_thr, tie_bias + hit, NEG_BIG), NEG_BIG))
        for h in range(H_B):
            top = None
            for c in range(n_ch):
                rows = slice(c * CH, (c + 1) * CH)
                piece = s_scr[slot, h, rows, :] + bias_scr[rows, :]
                s_scr[slot, h, rows, :] = piece
                top = piece if top is None else jnp.maximum(top, piece)
            m_old = m_scr[h:h + 1, :]
            m_new = jnp.maximum(m_old, jnp.max(top, axis=0, keepdims=True))
            alpha = jnp.exp2(m_old - m_new)
            total = None
            for c in range(n_ch):
                rows = slice(c * CH, (c + 1) * CH)
                e = jnp.exp2(s_scr[slot, h, rows, :] - m_new)
                p_scr[h, rows, :] = e.astype(BF16)
                total = e if total is None else total + e
            m_scr[h:h + 1, :] = m_new
            l_scr[h:h + 1, :] = alpha * l_scr[h:h + 1, :] + jnp.sum(total, axis=0, keepdims=True)
            pv = jnp.dot(vt_ref[h * HEAD_DIM:(h + 1) * HEAD_DIM, pl.ds(off, KA)], p_scr[h],
                         preferred_element_type=F32)
            acc_scr[h] = acc_scr[h] * alpha + pv

    def attn_body(i, carry):
        scores_into(1, jnp.minimum(2 * i + 1, nka - 1))
        attend_block(0, 2 * i)
        scores_into(0, jnp.minimum(2 * i + 2, nka - 1))
        attend_block(1, 2 * i + 1)
        return carry

    lax.fori_loop(0, lax.div(nka + 1, 2), attn_body, 0)
    out_t = jnp.concatenate([acc_scr[h] / l_scr[h:h + 1, :] for h in range(H_B)], axis=0)
    o_ref[...] = out_t.T.astype(BF16)


def _dsa_prompt(qt, k, vt, qit, wit, ki):
    B, T, _ = k.shape
    topk = min(IDX_TOPK_MAX, T // 4)
    kb = min(512, T)
    ka = min(512, T)
    idx_bits = max(1, int(np.ceil(np.log2(T))))
    qcols = lambda rows: pl.BlockSpec((None, rows, DSA_QB), lambda b, i: (b, 0, i))
    whole = lambda r, c: pl.BlockSpec((None, r, c), lambda b, i: (b, 0, 0))
    return pl.pallas_call(
        functools.partial(_dsa_prompt_kernel, topk, kb, ka, idx_bits),
        grid=(B, T // DSA_QB),
        in_specs=[qcols(D_B), whole(T, D_B), whole(D_B, T), qcols(D_B), qcols(H_IDX), whole(T, D_IDX)],
        out_specs=pl.BlockSpec((None, DSA_QB, D_B), lambda b, i: (b, i, 0)),
        out_shape=jax.ShapeDtypeStruct((B, T, D_B), BF16),
        scratch_shapes=[pltpu.VMEM((T, DSA_QB), I32), pltpu.VMEM((8, DSA_QB), I32),
                        pltpu.VMEM((2, H_B, ka, DSA_QB), F32), pltpu.VMEM((ka, DSA_QB), F32),
                        pltpu.VMEM((H_B, ka, DSA_QB), BF16), pltpu.VMEM((H_B, DSA_QB), F32),
                        pltpu.VMEM((H_B, DSA_QB), F32), pltpu.VMEM((H_B, HEAD_DIM, DSA_QB), F32),
                        pltpu.VMEM((T, DSA_QB), I16)],
        compiler_params=_cparams(("parallel", "arbitrary")),
        name="dsa_prompt",
    )(qt, k, vt, qit, wit, ki)


def _tile(n, pref):
    return pref if n % pref == 0 else n


def kernel(x_prompt, x_sample, state_a_wkv, state_a_shift, cache_b_k, cache_b_v, cache_b_kidx, cache_c_k, cache_c_v, page_table, norm_mix, norm_ffn, norm_final, w_in_0, w_out_0, a_mu, a_w0, a_w2, a_a0, a_a2, a_g2, a_kk, a_ka, a_rk, a_ln_w, a_ln_b, ffn_wg, ffn_wu, ffn_wd, w_in_1, w_out_1, router_w, router_b, moe_wg, moe_wu, moe_wd):
    B, T, D = x_prompt.shape
    DB, S, _ = x_sample.shape
    assert S == 1 and D == D_MODEL
    past = page_table.shape[1] * PAGE_SIZE
    row = lambda a: a.reshape(1, -1).astype(F32)
    b16 = lambda a: a.astype(BF16)

    prm = _rwkv_params(dict(a_mu=a_mu, a_w0=a_w0, a_w2=a_w2, a_a0=a_a0, a_a2=a_a2, a_g2=a_g2, a_kk=a_kk, a_ka=a_ka,
                            a_rk=a_rk, a_ln_w=a_ln_w, a_ln_b=a_ln_b))
    w_in0 = _pad_w_in0(w_in_0)
    w_out0, w_in1, w_out1 = b16(w_out_0), b16(w_in_1), b16(w_out_1)
    f_wg, f_wu, f_wd = b16(ffn_wg), b16(ffn_wu), b16(ffn_wd)
    m_wg, m_wu, m_wd = b16(moe_wg), b16(moe_wu), b16(moe_wd)
    rw, rb = _pad_router(router_w, router_b)
    g_mix0, g_mix1 = row(norm_mix[0]), row(norm_mix[1])
    g_ffn0, g_ffn1, g_fin = row(norm_ffn[0]), row(norm_ffn[1]), row(norm_final)
    tf_ffn = D_FF // 2
    tf_moe = 896

    N = B * T
    cos_p, sin_p = _rope_tables(jnp.arange(T, dtype=I32))
    pa, kf, vf, kif, qt, kb, vt, qit, wit, kib = _inproj0(x_prompt, g_mix0, w_in0, cos_p, sin_p, _tile(T, 256), True)
    prep = _rwkv_prep(pa, jnp.zeros((B, A_COLS), F32), prm, _tile(T, 256), True)
    ya, p_a_wkv = _rwkv_scan(*prep, prm["lnw"], prm["lnb"], _tile(T, 128))
    yb = _dsa_prompt(qt, kb, vt, qit, wit, kib)
    h = _outproj_ffn(x_prompt.reshape(N, D), ya.reshape(N, D_A), yb.reshape(N, D_B), w_out0, g_ffn0, f_wg, f_wu, f_wd,
                     _tile(N, 512), tf_ffn)
    keep = min(C_WINDOW_MAX, T)
    q1, k1, v1, k1f, v1f, *streams = _inproj1(h.reshape(B, T, D), g_mix1, w_in1, cos_p, sin_p, _tile(T, 256), keep)
    n_dil = len(C_PATTERNS) - 1
    outs, lses = [], []
    for i, (window, dil) in enumerate(C_PATTERNS):
        qkv = (q1, k1, v1) if i == 0 else tuple(streams[which * n_dil + i - 1] for which in range(3))
        o, lse = _dilated_branch(*qkv, window, dil)
        outs.append(o.reshape(N // dil, dil * D_C))
        lses.append(lse.reshape(N // dil, dil * LANES))
    h = _merge_outproj(h, outs, lses, w_out1, _tile(N, 512))
    if N % MOE_TR == 0:
        y_prompt = _moe_routed(h, g_ffn1, rw, rb, m_wg, m_wu, m_wd, g_fin, tf_moe).reshape(B, T, D)
    else:
        y_prompt = _moe_dense(h, g_ffn1, rw, rb, m_wg, m_wu, m_wd, g_fin, N, tf_moe).reshape(B, T, D)
    rows = lambda a, n: jnp.moveaxis(a.reshape(B, n, HEAD_DIM, a.shape[-1]), -1, 1)
    prompt_state = (p_a_wkv, pa[:, -1], rows(kf, H_B), rows(vf, H_B), jnp.swapaxes(kif, 1, 2),
                    rows(k1f, H_C), rows(v1f, H_C))

    cos_s, sin_s = _rope_tables(jnp.full((DB,), past, I32))
    xs = x_sample.reshape(1, DB, D)
    pa, kf, vf, kif, q, qi, tail = _inproj0(xs, g_mix0, w_in0, cos_s, sin_s, DB, False)
    prep = _rwkv_prep(pa, state_a_shift.astype(F32), prm, DB, False)
    ya, s_a_wkv = _rwkv_step(state_a_wkv.astype(F32), *(a.reshape(DB, D_A) for a in prep), prm["lnw"], prm["lnb"])
    pad8 = lambda a: jnp.concatenate([a, jnp.zeros_like(a)], axis=1)
    qi16 = pad8(qi.reshape(DB, H_IDX, D_IDX))
    wi16 = pad8((tail[0, :, D_IDX:D_IDX + H_IDX] * IDX_SCALE).reshape(DB, H_IDX, 1))
    ki16 = jnp.broadcast_to(kif.reshape(DB, 1, D_IDX), (DB, 16, D_IDX))
    bias, bias_new = _dsa_sample_select(page_table, qi16, wi16, ki16, jnp.swapaxes(cache_b_kidx.astype(F32), 1, 2))
    heads = lambda a, n: a.astype(F32).reshape(DB, n, HEAD_DIM)
    yb = _dsa_sample_attend(page_table, heads(q, H_B), heads(kf, H_B), heads(vf, H_B), bias, bias_new,
                            cache_b_k.astype(F32), cache_b_v.astype(F32))
    hs = _outproj_ffn(x_sample.reshape(DB, D), ya, yb.reshape(DB, D_B).astype(BF16), w_out0, g_ffn0, f_wg, f_wu, f_wd,
                      DB, tf_ffn)
    q1, k1, v1, k1f, v1f = _inproj1(hs.reshape(1, DB, D), g_mix1, w_in1, cos_s, sin_s, DB)
    yc = _dilated_sample(heads(q1, H_C), heads(k1f, H_C), heads(v1f, H_C), cache_c_k.astype(F32), cache_c_v.astype(F32))
    hs = _outproj(hs, yc.reshape(DB, D_C).astype(BF16), w_out1, DB)
    y_sample = _moe_dense(hs, g_ffn1, rw, rb, m_wg, m_wu, m_wd, g_fin, DB, tf_moe).reshape(DB, 1, D)
    keep = min(C_WINDOW_MAX, cache_c_k.shape[1] + 1)
    s_c_k = jnp.concatenate([cache_c_k, k1f.reshape(DB, 1, H_C, HEAD_DIM)], axis=1)[:, -keep:]
    s_c_v = jnp.concatenate([cache_c_v, v1f.reshape(DB, 1, H_C, HEAD_DIM)], axis=1)[:, -keep:]
    sample_state = (s_a_wkv, pa[0], kf.reshape(DB, 1, H_B, HEAD_DIM), vf.reshape(DB, 1, H_B, HEAD_DIM),
                    kif.reshape(DB, 1, D_IDX), s_c_k, s_c_v)
    return (y_prompt, y_sample) + prompt_state + sample_state
```

```python
import functools

import numpy as np
import jax
import jax.numpy as jnp
from jax import lax
from jax.experimental import pallas as pl
from jax.experimental.pallas import tpu as pltpu

F32 = jnp.float32
BF16 = jnp.bfloat16
I32 = jnp.int32

D_MODEL = 1024
HEAD_DIM = 64
ROPE_THETA = 10000.0
NORM_EPS = 1e-6
PAGE_SIZE = 128

H_A = 8
D_A = H_A * HEAD_DIM
DECAY_LORA = 64
AAA_LORA = 64
GATE_LORA = 128
A_COLS = 3 * D_A + DECAY_LORA + AAA_LORA + GATE_LORA
GN_EPS = 64e-5

H_B = 8
D_B = H_B * HEAD_DIM
H_IDX = 8
D_IDX = 64
IDX_TOPK_MAX = 256
IDX_SCALE = (H_IDX ** -0.5) * (D_IDX ** -0.5)
B_COLS_PAD = 3 * D_B + H_IDX * D_IDX + 128

H_C = 16
D_C = H_C * HEAD_DIM
C_PATTERNS = ((128, 1), (512, 4), (2048, 16))
C_WINDOW_MAX = 2048
C_BLOCK = 128

D_FF = 2816
N_EXPERTS = 8
D_FF_EXPERT = 3584

LANES = 128
VMEM_LIMIT = 56 << 20
INT_MIN = -(2 ** 31)
NEG_BIG = -1e30
LOG2E = 1.4426950408889634


def _cparams(sem, vmem=VMEM_LIMIT):
    return pltpu.CompilerParams(dimension_semantics=sem, vmem_limit_bytes=vmem)


def _dot(a, b):
    return jnp.dot(a.astype(BF16), b.astype(BF16), preferred_element_type=F32)


def _dot_nt(a, b):
    return lax.dot_general(a.astype(BF16), b.astype(BF16), (((1,), (1,)), ((), ())), preferred_element_type=F32)


def _split(x):
    hi = x.astype(BF16)
    lo = (x - hi.astype(F32)).astype(BF16)
    return hi, lo


def _dot_split_lhs(a, b_exact):
    hi, lo = _split(a)
    return jnp.dot(hi, b_exact, preferred_element_type=F32) + jnp.dot(lo, b_exact, preferred_element_type=F32)


def _dot_split_rhs(a_exact, b):
    hi, lo = _split(b)
    return jnp.dot(a_exact, hi, preferred_element_type=F32) + jnp.dot(a_exact, lo, preferred_element_type=F32)


def _dot3(a, b):
    ah, al = _split(a)
    bh, bl = _split(b)
    return (jnp.dot(ah, bh, preferred_element_type=F32) + jnp.dot(ah, bl, preferred_element_type=F32)
            + jnp.dot(al, bh, preferred_element_type=F32))


def _iota(shape, axis):
    return lax.broadcasted_iota(I32, shape, axis)


def _rope_tables(pos):
    half = HEAD_DIM // 2
    inv_freq = jnp.power(ROPE_THETA, -jnp.arange(half, dtype=F32) / half)
    ang = pos.astype(F32)[:, None] * inv_freq[None, :]
    cos = jnp.cos(ang)
    sin = jnp.sin(ang)
    return jnp.tile(cos, (1, 4)), jnp.tile(jnp.concatenate([-sin, sin], axis=1), (1, 2))


def _rope(x, cos, sin):
    w = x.shape[-1]
    reps = w // LANES
    if reps > 1:
        cos = jnp.concatenate([cos] * reps, axis=1)
        sin = jnp.concatenate([sin] * reps, axis=1)
    first_half = (_iota(x.shape, 1) % HEAD_DIM) < (HEAD_DIM // 2)
    swapped = jnp.where(first_half, pltpu.roll(x, w - HEAD_DIM // 2, 1), pltpu.roll(x, HEAD_DIM // 2, 1))
    return x * cos + swapped * sin


def _rms(x, g):
    ms = jnp.mean(x * x, axis=-1, keepdims=True)
    return x * lax.rsqrt(ms + NORM_EPS) * g


def _inproj0_kernel(for_prompt, x_ref, g_ref, w_ref, cos_ref, sin_ref, pa_ref, kf_ref, vf_ref, kif_ref, *outs):
    h = _rms(x_ref[...], g_ref[...]).astype(BF16)
    p = jnp.dot(h, w_ref[...], preferred_element_type=F32)
    cos = cos_ref[...]
    sin = sin_ref[...]
    o = A_COLS
    pa_ref[...] = p[:, :o]
    q = _rope(p[:, o:o + D_B], cos, sin) * (HEAD_DIM ** -0.5)
    k = _rope(p[:, o + D_B:o + 2 * D_B], cos, sin)
    v = p[:, o + 2 * D_B:o + 3 * D_B]
    qi = _rope(p[:, o + 3 * D_B:o + 4 * D_B], cos, sin)
    tail = p[:, o + 4 * D_B:]
    ki = _rope(tail, cos, sin)[:, :D_IDX]
    if for_prompt:
        qt_ref, k_ref, vt_ref, qit_ref, wit_ref, ki_ref = outs
        v_t = v.T
        kf_ref[...] = k.T
        vf_ref[...] = v_t
        kif_ref[...] = ki.T
        qt_ref[...] = (q * LOG2E).T.astype(BF16)
        k_ref[...] = k.astype(BF16)
        vt_ref[...] = v_t.astype(BF16)
        qit_ref[...] = qi.T.astype(BF16)
        wit_ref[...] = tail.T[D_IDX:D_IDX + H_IDX, :] * IDX_SCALE
        ki_ref[...] = ki.astype(BF16)
    else:
        q_ref, qi_ref, tail_ref = outs
        kf_ref[...] = k
        vf_ref[...] = v
        kif_ref[...] = ki
        q_ref[...] = q.astype(BF16)
        qi_ref[...] = qi.astype(BF16)
        tail_ref[...] = tail


def _inproj0(x, g, w_pad, cos, sin, tm, for_prompt):
    B, T, _ = x.shape
    tok = lambda width: pl.BlockSpec((None, tm, width), lambda b, t: (b, t, 0))
    tr = lambda rows: pl.BlockSpec((None, rows, tm), lambda b, t: (b, 0, t))
    full = lambda a: pl.BlockSpec(a.shape, lambda b, t: (0,) * a.ndim)
    tab = pl.BlockSpec((tm, LANES), lambda b, t: (t, 0))
    sds = lambda shape, dt: jax.ShapeDtypeStruct(shape, dt)
    if for_prompt:
        specs = [tok(A_COLS), tr(D_B), tr(D_B), tr(D_IDX), tr(D_B), tok(D_B), tr(D_B), tr(D_B), tr(H_IDX), tok(D_IDX)]
        shapes = [sds((B, T, A_COLS), F32), sds((B, D_B, T), F32), sds((B, D_B, T), F32), sds((B, D_IDX, T), F32),
                  sds((B, D_B, T), BF16), sds((B, T, D_B), BF16), sds((B, D_B, T), BF16), sds((B, D_B, T), BF16),
                  sds((B, H_IDX, T), F32), sds((B, T, D_IDX), BF16)]
    else:
        specs = [tok(A_COLS), tok(D_B), tok(D_B), tok(D_IDX), tok(D_B), tok(D_B), tok(LANES)]
        shapes = [sds((B, T, A_COLS), F32), sds((B, T, D_B), F32), sds((B, T, D_B), F32), sds((B, T, D_IDX), F32),
                  sds((B, T, D_B), BF16), sds((B, T, D_B), BF16), sds((B, T, LANES), F32)]
    return pl.pallas_call(
        functools.partial(_inproj0_kernel, for_prompt),
        grid=(B, T // tm),
        in_specs=[tok(D_MODEL), full(g), full(w_pad), tab, tab],
        out_specs=specs,
        out_shape=shapes,
        compiler_params=_cparams(("parallel", "arbitrary")),
        name="inproj0",
    )(x, g, w_pad, cos, sin)


def _seg_sum(x, bd):
    return _dot_split_lhs(x, bd)


def _rwkv_prep_kernel(seq_mode, p_ref, prev_ref, shift_ref, mu_ref, w0_ref, w2_ref, a0_ref, a2_ref, g2_ref,
                      kk_ref, ka_ref, rk_ref, bd_ref,
                      r_out, ld_out, k_out, v_out, kkn_out, ab_out, g_out, bonus_out):
    p = p_ref[...]
    if seq_mode:
        last = jnp.where(pl.program_id(1) == 0, shift_ref[...], prev_ref[7:8, :])
        prev = jnp.where(_iota(p.shape, 0) == 0, last, pltpu.roll(p, 1, 0))
    else:
        prev = prev_ref[...]
    xm = p + (prev - p) * mu_ref[...]
    r = xm[:, :D_A]
    k = xm[:, D_A:2 * D_A]
    v = xm[:, 2 * D_A:3 * D_A]
    wa = xm[:, 3 * D_A:3 * D_A + LANES]
    gl = xm[:, 3 * D_A + LANES:]
    z = -(w0_ref[...] + _dot(jnp.tanh(wa), w2_ref[...]))
    softplus = jnp.maximum(z, 0.0) + jnp.log(1.0 + jnp.exp(-jnp.abs(z)))
    ld_out[...] = -jnp.exp(-softplus - 0.5)
    a = jax.nn.sigmoid(a0_ref[...] + _dot(wa, a2_ref[...]))
    g_out[...] = _dot(jax.nn.sigmoid(gl), g2_ref[...])
    bd = bd_ref[...]
    kk = k * kk_ref[...]
    kkn = kk * lax.rsqrt(jnp.maximum(_seg_sum(kk * kk, bd), 1e-24))
    k2 = k * (1.0 + (a - 1.0) * ka_ref[...])
    r_out[...] = r
    k_out[...] = k2
    v_out[...] = v
    kkn_out[...] = kkn
    ab_out[...] = kkn * a
    bonus_out[...] = _seg_sum(r * k2 * rk_ref[...], bd) * v


def _block_diag_ones(n, seg=HEAD_DIM):
    i = np.arange(n)
    return jnp.asarray((i[:, None] // seg) == (i[None, :] // seg), BF16)


def _rwkv_prep(pa, shift_prev, prm, tm, seq_mode):
    B, T, _ = pa.shape
    nt = T // tm
    tok = lambda width: pl.BlockSpec((None, tm, width), lambda b, t: (b, t, 0))
    full = lambda a: pl.BlockSpec(a.shape, lambda b, t: (0,) * a.ndim)
    if seq_mode:
        prev_spec = pl.BlockSpec((None, 8, A_COLS), lambda b, t: (b, jnp.maximum(t * (tm // 8) - 1, 0), 0))
        prev_arr = pa
        shift_arr = shift_prev.reshape(B, 1, A_COLS)
        shift_spec = pl.BlockSpec((None, 1, A_COLS), lambda b, t: (b, 0, 0))
    else:
        prev_spec = tok(A_COLS)
        prev_arr = shift_prev.reshape(1, T, A_COLS)
        shift_arr = jnp.zeros((1, 1, A_COLS), F32)
        shift_spec = pl.BlockSpec((None, 1, A_COLS), lambda b, t: (0, 0, 0))
    params = [prm[n] for n in ("mu", "w0", "w2", "a0", "a2", "g2", "kk", "ka", "rk", "bd")]
    out = jax.ShapeDtypeStruct((B, T, D_A), F32)
    return pl.pallas_call(
        functools.partial(_rwkv_prep_kernel, seq_mode),
        grid=(B, nt),
        in_specs=[tok(A_COLS), prev_spec, shift_spec] + [full(a) for a in params],
        out_specs=[tok(D_A)] * 8,
        out_shape=[out] * 8,
        compiler_params=_cparams(("parallel", "arbitrary")),
        name="rwkv_prep",
    )(pa, prev_arr, shift_arr, *params)


RWKV_CHUNK = 64
RWKV_GROUP = 4
RWKV_W = RWKV_GROUP * HEAD_DIM


def _rwkv_chunk(r, ld, k, v, kkn, ab, h, tri, same_head, strict, incl, eye):
    each = lambda f, *xs: [f(*a) for a in zip(*xs)]
    w = RWKV_W

    def expand(x):
        return jnp.where(same_head, jnp.concatenate([x] * RWKV_GROUP, axis=0), 0.0).astype(BF16)

    cum = each(lambda x: _dot_split_rhs(tri, x), ld)
    cum_end = each(lambda c: c[RWKV_CHUNK - 1:RWKV_CHUNK, :], cum)
    a_t = each(lambda x, c, l: expand(-(x * jnp.exp(c - l))), kkn, cum, ld)
    r_t = each(lambda x, c: expand(x * jnp.exp(c)), r, cum)
    e_neg = each(lambda c: jnp.exp(-c), cum)
    b_t = each(lambda x, e: expand(x * e), ab, e_neg)
    k_t = each(lambda x, e: expand(x * e), k, e_neg)
    v_e = each(expand, v)
    gram = each(lambda a, rr, b, kk: _dot_nt(jnp.concatenate([a, rr], axis=0), jnp.concatenate([b, kk], axis=0)),
                a_t, r_t, b_t, k_t)
    l_ab = each(lambda g: jnp.where(strict, g[:w, :w], 0.0), gram)
    a_ak = each(lambda g: jnp.where(strict, g[:w, w:], 0.0), gram)
    a_r = each(lambda g: jnp.concatenate([jnp.where(incl, g[w:, :w], 0.0), jnp.where(incl, g[w:, w:], 0.0)], axis=1), gram)
    pinv = each(lambda l: jnp.where(eye, 1.0, 0.0) + l, l_ab)
    qpow = l_ab
    for _ in range(5):
        qpow = each(lambda q: _dot(q, q), qpow)
        pinv = each(lambda p, q: p + _dot(q, p), pinv, qpow)
    x0 = each(_dot, a_ak, v_e)
    wu = each(lambda p, a, x: _dot(p, jnp.concatenate([a, x.astype(BF16)], axis=1)), pinv, a_t, x0)
    u_e = each(lambda m, hh: _dot(m[:, :w], hh) + m[:, w:], wu, h)
    uv = each(lambda u, vv: jnp.concatenate([u.astype(BF16), vv], axis=0), u_e, v_e)
    y_e = each(lambda rr, hh, a, x: _dot(rr, hh) + _dot(a, x), r_t, h, a_r, uv)
    y = each(lambda ye: sum(ye[i * RWKV_CHUNK:(i + 1) * RWKV_CHUNK] for i in range(1, RWKV_GROUP)) + ye[:RWKV_CHUNK], y_e)
    e_rem = each(lambda ce, c: jnp.exp(ce - c), cum_end, cum)
    bk_t = each(lambda b, kk, e: jnp.concatenate(
        [jnp.where(same_head, jnp.concatenate([b * e] * RWKV_GROUP, axis=0), 0.0).T,
         jnp.where(same_head, jnp.concatenate([kk * e] * RWKV_GROUP, axis=0), 0.0).T], axis=1), ab, k, e_rem)
    g_col = each(lambda ce: jnp.exp(jnp.broadcast_to(ce, (8, w))).T[:, :1], cum_end)
    h_new = each(lambda hh, g, b, x: hh * g + _dot(b, x), h, g_col, bk_t, uv)
    return y, h_new


def _rwkv_scan_kernel(n_chunks, r_ref, ld_ref, k_ref, v_ref, kkn_ref, ab_ref, g_ref, bonus_ref, lnw_ref, lnb_ref,
                      y_ref, ht_ref, h_scr):
    @pl.when(pl.program_id(1) == 0)
    def _():
        h_scr[...] = jnp.zeros_like(h_scr)

    w = RWKV_W
    c = RWKV_CHUNK
    row = _iota((w, w), 0)
    col = _iota((w, w), 1)
    same_head = (row // c) == (col // HEAD_DIM)
    strict = (row % c) > (col % c)
    incl = (row % c) >= (col % c)
    eye = row == col
    tri = jnp.where(_iota((c, c), 0) >= _iota((c, c), 1), 1.0, 0.0).astype(BF16)
    seg_avg = jnp.where((row // HEAD_DIM) == (col // HEAD_DIM), 1.0 / HEAD_DIM, 0.0).astype(BF16)
    n_seq = r_ref.shape[0]
    chains = [(s, slice(g * w, (g + 1) * w)) for s in range(n_seq) for g in range(D_A // w)]
    hs = [h_scr[j] for j in range(len(chains))]
    for i in range(n_chunks):
        sl = slice(i * c, (i + 1) * c)
        pick = lambda ref: [ref[s, sl, ln] for s, ln in chains]
        ys, hs = _rwkv_chunk(pick(r_ref), pick(ld_ref), pick(k_ref), pick(v_ref), pick(kkn_ref), pick(ab_ref), hs,
                             tri, same_head, strict, incl, eye)
        for y, (s, ln) in zip(ys, chains):
            mean = _dot_split_lhs(y, seg_avg)
            yc = y - mean
            var = _dot_split_lhs(yc * yc, seg_avg)
            yn = yc * lax.rsqrt(var + GN_EPS) * lnw_ref[:, ln] + lnb_ref[:, ln]
            y_ref[s, sl, ln] = ((yn + bonus_ref[s, sl, ln]) * g_ref[s, sl, ln]).astype(BF16)
    for j in range(len(chains)):
        h_scr[j] = hs[j]

    @pl.when(pl.program_id(1) == pl.num_programs(1) - 1)
    def _():
        for j in range(len(chains)):
            ht_ref[j] = hs[j].T


def _rwkv_scan(r, ld, k, v, kkn, ab, g, bonus, lnw, lnb, tb):
    B, T, _ = r.shape
    ng = D_A // RWKV_W
    nb = 2 if B % 2 == 0 else 1
    blk = pl.BlockSpec((nb, tb, D_A), lambda b, t: (b, t, 0))
    par = pl.BlockSpec((1, D_A), lambda b, t: (0, 0))
    y, ht = pl.pallas_call(
        functools.partial(_rwkv_scan_kernel, tb // RWKV_CHUNK),
        grid=(B // nb, T // tb),
        in_specs=[blk] * 8 + [par, par],
        out_specs=[blk, pl.BlockSpec((nb * ng, RWKV_W, RWKV_W), lambda b, t: (b, 0, 0))],
        out_shape=[jax.ShapeDtypeStruct((B, T, D_A), BF16), jax.ShapeDtypeStruct((B * ng, RWKV_W, RWKV_W), F32)],
        scratch_shapes=[pltpu.VMEM((nb * ng, RWKV_W, RWKV_W), F32)],
        compiler_params=_cparams(("parallel", "arbitrary")),
        name="rwkv_scan",
    )(r, ld, k, v, kkn, ab, g, bonus, lnw, lnb)
    ht = ht.reshape(B, ng, RWKV_GROUP, HEAD_DIM, RWKV_GROUP, HEAD_DIM)
    idx = jnp.arange(RWKV_GROUP)
    wkv = ht[:, :, idx, :, idx, :]
    return y, jnp.moveaxis(wkv, 0, 2).reshape(B, H_A, HEAD_DIM, HEAD_DIM)


def _rwkv_step_kernel(s_ref, r_ref, ld_ref, k_ref, v_ref, kkn_ref, ab_ref, g_ref, bonus_ref, lnw_ref, lnb_ref,
                      y_ref, s_out):
    rows = H_A * HEAD_DIM
    pad = 16
    rep = jnp.where((_iota((rows, pad), 0) // HEAD_DIM) == _iota((rows, pad), 1), 1.0, 0.0).astype(BF16)
    rep_t = jnp.where((_iota((pad, rows), 1) // HEAD_DIM) == _iota((pad, rows), 0), 1.0, 0.0).astype(BF16)
    zeros8 = jnp.zeros((pad - H_A, HEAD_DIM), F32)
    spread = lambda x8: _dot_split_rhs(rep, jnp.concatenate([x8, zeros8], axis=0))
    diag = (_iota((rows, HEAD_DIM), 0) % HEAD_DIM) == _iota((rows, HEAD_DIM), 1)
    s = s_ref[...].reshape(rows, HEAD_DIM)
    a_rep = -spread(kkn_ref[...])
    sa = jnp.sum(s * a_rep, axis=1, keepdims=True)
    v_col = jnp.sum(jnp.where(diag, spread(v_ref[...]), 0.0), axis=1, keepdims=True)
    s_new = s * jnp.exp(spread(ld_ref[...])) + sa * spread(ab_ref[...]) + v_col * spread(k_ref[...])
    s_out[...] = s_new.reshape(H_A, HEAD_DIM, HEAD_DIM)
    y_col = jnp.sum(s_new * spread(r_ref[...]), axis=1, keepdims=True)
    y = _dot_split_rhs(rep_t, jnp.where(diag, y_col, 0.0))[:H_A]
    mean = jnp.mean(y, axis=1, keepdims=True)
    yc = y - mean
    var = jnp.mean(yc * yc, axis=1, keepdims=True)
    yn = yc * lax.rsqrt(var + GN_EPS) * lnw_ref[...] + lnb_ref[...]
    y_ref[...] = ((yn + bonus_ref[...]) * g_ref[...]).astype(BF16)


def _rwkv_step(state, r, ld, k, v, kkn, ab, g, bonus, lnw, lnb):
    DB = state.shape[0]
    heads = lambda a: a.reshape(DB, H_A, HEAD_DIM)
    vec = pl.BlockSpec((None, H_A, HEAD_DIM), lambda b: (b, 0, 0))
    par = pl.BlockSpec((H_A, HEAD_DIM), lambda b: (0, 0))
    st = pl.BlockSpec((None, H_A, HEAD_DIM, HEAD_DIM), lambda b: (b, 0, 0, 0))
    y, s_new = pl.pallas_call(
        _rwkv_step_kernel,
        grid=(DB,),
        in_specs=[st] + [vec] * 8 + [par, par],
        out_specs=[vec, st],
        out_shape=[jax.ShapeDtypeStruct((DB, H_A, HEAD_DIM), BF16), jax.ShapeDtypeStruct(state.shape, F32)],
        compiler_params=_cparams(("parallel",)),
        name="rwkv_step",
    )(state, *(heads(a) for a in (r, ld, k, v, kkn, ab, g, bonus)), lnw.reshape(H_A, HEAD_DIM), lnb.reshape(H_A, HEAD_DIM))
    return y.reshape(DB, D_A), s_new


DSA_PG = 8
DSA_PG_IDX = 16


def _dsa_sample_select_kernel(topk, n_groups, idx_bits, pt_ref, qi_ref, w_ref, kin_ref, *rest):
    n_pg = len(rest) - 3
    pages = rest[:n_pg]
    bias_ref, bias_new_ref, key_scr = rest[n_pg:]
    g = pl.program_id(1)
    qi8 = qi_ref[...]
    w8 = w_ref[...]
    allp = jnp.concatenate([p[...] for p in pages], axis=1)
    res = _dot(qi8, allp)
    score = jnp.sum(jnp.maximum(res, 0.0) * w8, axis=0, keepdims=True)
    key = _order_key(score)
    for i in range(n_pg):
        key_scr[pl.ds(g * n_pg + i, 1), :] = key[:, i * PAGE_SIZE:(i + 1) * PAGE_SIZE]

    @pl.when(g == n_groups - 1)
    def _():
        n_pages = n_groups * n_pg
        s_new = jnp.sum(jnp.maximum(_dot_nt(qi8, kin_ref[...]), 0.0) * w8, axis=0, keepdims=True)[:, :1]
        key_new = _order_key(s_new)
        keys = key_scr[...]
        idx = _iota(keys.shape, 0) * PAGE_SIZE + _iota(keys.shape, 1)
        idx_new = n_pages * PAGE_SIZE
        total = lambda m: jnp.sum(jnp.sum(m, axis=1, keepdims=True), axis=0, keepdims=True)
        count_ge = lambda c: total(jnp.where(keys >= c, 1.0, 0.0)) + jnp.where(key_new >= c, 1.0, 0.0)
        thr = _kth_largest_key(count_ge, topk, (1, 1))
        n_gt = total(jnp.where(keys > thr, 1.0, 0.0)) + jnp.where(key_new > thr, 1.0, 0.0)
        need = float(topk) - n_gt

        def bit_body(i, j):
            cand = j + lax.shift_left(jnp.int32(1), idx_bits - 1 - i)
            ties = (total(jnp.where(keys == thr, jnp.where(idx <= cand, 1.0, 0.0), 0.0))
                    + jnp.where(key_new == thr, jnp.where(idx_new <= cand, 1.0, 0.0), 0.0))
            return jnp.where(ties < need, cand, j)

        idx_thr = lax.fori_loop(0, idx_bits, bit_body, jnp.full((1, 1), -1, I32)) + 1
        sel = lambda kk, ii: jnp.where(kk > thr, 0.0, jnp.where(kk == thr, jnp.where(ii <= idx_thr, 0.0, NEG_BIG), NEG_BIG))
        bias_ref[...] = sel(keys, idx)
        bias_new_ref[...] = jnp.where(_iota(bias_new_ref.shape, 1) == 0, sel(key_new, idx_new), NEG_BIG)


def _dsa_sample_select(page_table, qi, wi, ki_new, kidx_t):
    DB, n_pages = page_table.shape
    n_pg = DSA_PG_IDX if n_pages % DSA_PG_IDX == 0 else DSA_PG
    n_groups = n_pages // n_pg
    L = n_pages * PAGE_SIZE + 1
    topk = min(IDX_TOPK_MAX, L // 4)
    idx_bits = int(np.ceil(np.log2(L))) + 1
    per_seq = lambda a: pl.BlockSpec((None,) + a.shape[1:], lambda b, g, pt: (b,) + (0,) * (a.ndim - 1))
    page = lambda i: pl.BlockSpec((None, D_IDX, PAGE_SIZE), lambda b, g, pt: (pt[b, g * n_pg + i], 0, 0))
    grid_spec = pltpu.PrefetchScalarGridSpec(
        num_scalar_prefetch=1,
        grid=(DB, n_groups),
        in_specs=[per_seq(qi), per_seq(wi), per_seq(ki_new)] + [page(i) for i in range(n_pg)],
        out_specs=[pl.BlockSpec((None, n_pages, PAGE_SIZE), lambda b, g, pt: (b, 0, 0)),
                   pl.BlockSpec((None, 8, LANES), lambda b, g, pt: (b, 0, 0))],
        scratch_shapes=[pltpu.VMEM((n_pages, PAGE_SIZE), I32)],
    )
    return pl.pallas_call(
        functools.partial(_dsa_sample_select_kernel, topk, n_groups, idx_bits),
        grid_spec=grid_spec,
        out_shape=[jax.ShapeDtypeStruct((DB, n_pages, PAGE_SIZE), F32), jax.ShapeDtypeStruct((DB, 8, LANES), F32)],
        compiler_params=_cparams(("parallel", "arbitrary")),
        name="dsa_sample_select",
    )(page_table, qi, wi, ki_new, *([kidx_t] * n_pg))


def _keys_on_lanes(cache):
    return jnp.moveaxis(cache, -3, -1)


def _on_lanes(x):
    return jnp.broadcast_to(x.astype(F32)[..., None], x.shape + (LANES,))


def _lane0(x):
    return jnp.pad(x.astype(F32)[..., None], ((0, 0), (0, 0), (0, 0), (0, LANES - 1)))


def _lane_attend_scratch(n_heads, width):
    return [pltpu.VMEM((n_heads, LANES), F32), pltpu.VMEM((n_heads, LANES), F32),
            pltpu.VMEM((n_heads, HEAD_DIM, LANES), F32), pltpu.VMEM((n_heads, width), F32),
            pltpu.VMEM((n_heads, LANES), F32)]


def _lane_attend_init(m_scr, l_scr, acc_scr, p_scr, a_scr):
    m_scr[...] = jnp.full(m_scr.shape, NEG_BIG, F32)
    l_scr[...] = jnp.zeros_like(l_scr)
    acc_scr[...] = jnp.zeros_like(acc_scr)


def _lane_attend(kt_refs, vt_refs, qb_ref, bias, m_scr, l_scr, acc_scr, p_scr, a_scr):
    n_heads = kt_refs[0].shape[0]
    pieces = [(ref_i, g) for ref_i, ref in enumerate(kt_refs) for g in range(ref.shape[2] // LANES)]
    groups = len(pieces)
    width = groups * LANES
    for h in range(n_heads):
        q = qb_ref[h]
        for j, (ref_i, g) in enumerate(pieces):
            p_scr[h:h + 1, j * LANES:(j + 1) * LANES] = jnp.sum(
                kt_refs[ref_i][h, :, g * LANES:(g + 1) * LANES] * q, axis=0, keepdims=True)
    s = p_scr[:, :width] + bias
    m_old = m_scr[...]
    m_new = jnp.maximum(m_old, jnp.max(s, axis=1, keepdims=True))
    alpha = jnp.exp(m_old - m_new)
    p = jnp.exp(s - (jnp.concatenate([m_new] * groups, axis=1) if groups > 1 else m_new))
    l_scr[...] = alpha * l_scr[...] + jnp.sum(p, axis=1, keepdims=True)
    m_scr[...] = m_new
    p_scr[:, :width] = p
    a_scr[...] = alpha
    for h in range(n_heads):
        acc = acc_scr[h] * a_scr[h:h + 1, :]
        for j, (ref_i, g) in enumerate(pieces):
            acc = acc + p_scr[h:h + 1, j * LANES:(j + 1) * LANES] * vt_refs[ref_i][h, :, g * LANES:(g + 1) * LANES]
        acc_scr[h] = acc


def _lane_attend_finish(o_ref, m_scr, l_scr, acc_scr, p_scr, a_scr):
    a_scr[...] = 1.0 / l_scr[...]
    for h in range(o_ref.shape[0]):
        o_ref[h] = jnp.sum(acc_scr[h] * a_scr[h:h + 1, :], axis=1, keepdims=True)


def _dsa_sample_attend_kernel(n_groups, pt_ref, qb_ref, kn_ref, vn_ref, bias_ref, bias_new_ref, *rest):
    kpages = rest[:DSA_PG]
    vpages = rest[DSA_PG:2 * DSA_PG]
    o_ref = rest[2 * DSA_PG]
    state = rest[2 * DSA_PG + 1:]
    g = pl.program_id(1)

    @pl.when(g == 0)
    def _():
        _lane_attend_init(*state)

    bias = jnp.concatenate([bias_ref[i:i + 1, :] for i in range(DSA_PG)], axis=1)
    _lane_attend(list(kpages), list(vpages), qb_ref, bias, *state)

    @pl.when(g == n_groups - 1)
    def _():
        _lane_attend([kn_ref], [vn_ref], qb_ref, bias_new_ref[0:1, :], *state)
        _lane_attend_finish(o_ref, *state)


def _dsa_sample_attend(page_table, q, k_new, v_new, bias, bias_new, cache_k, cache_v):
    DB, n_pages = page_table.shape
    n_groups = n_pages // DSA_PG
    per_seq = pl.BlockSpec((None, H_B, HEAD_DIM, LANES), lambda b, g, pt: (b, 0, 0, 0))
    page = lambda i: pl.BlockSpec((None, H_B, HEAD_DIM, PAGE_SIZE), lambda b, g, pt: (pt[b, g * DSA_PG + i], 0, 0, 0))
    grid_spec = pltpu.PrefetchScalarGridSpec(
        num_scalar_prefetch=1,
        grid=(DB, n_groups),
        in_specs=[per_seq, per_seq, per_seq,
                  pl.BlockSpec((None, DSA_PG, PAGE_SIZE), lambda b, g, pt: (b, g, 0)),
                  pl.BlockSpec((None, 8, LANES), lambda b, g, pt: (b, 0, 0))] + [page(i) for i in range(DSA_PG)] * 2,
        out_specs=pl.BlockSpec((None, H_B, HEAD_DIM, 1), lambda b, g, pt: (b, 0, 0, 0)),
        scratch_shapes=_lane_attend_scratch(H_B, DSA_PG * PAGE_SIZE),
    )
    ck, cv = _keys_on_lanes(cache_k), _keys_on_lanes(cache_v)
    return pl.pallas_call(
        functools.partial(_dsa_sample_attend_kernel, n_groups),
        grid_spec=grid_spec,
        out_shape=jax.ShapeDtypeStruct((DB, H_B, HEAD_DIM, 1), F32),
        compiler_params=_cparams(("parallel", "arbitrary")),
        name="dsa_sample_attend",
    )(page_table, _on_lanes(q), _lane0(k_new), _lane0(v_new), bias, bias_new, *([ck] * DSA_PG), *([cv] * DSA_PG))


DIL_CHUNK = 512


def _dilated_sample_kernel(w_len, qb_ref, kn_ref, vn_ref, kc_ref, vc_ref, o_ref, *state):
    c = pl.program_id(1)

    @pl.when(c == 0)
    def _():
        _lane_attend_init(*state)

    width = kc_ref.shape[2]
    dist = w_len - (c * width + _iota((1, width), 1))
    count = jnp.zeros((1, width), F32)
    for window, dil in C_PATTERNS:
        count = count + jnp.where(dist <= window, jnp.where(dist % dil == 0, 1.0, 0.0), 0.0)
    bias = jnp.where(count > 0.0, jnp.log(jnp.maximum(count, 1.0)), NEG_BIG)
    _lane_attend([kc_ref], [vc_ref], qb_ref, bias, *state)

    @pl.when(c == pl.num_programs(1) - 1)
    def _():
        bias_new = jnp.where(_iota((1, LANES), 1) == 0, float(np.log(len(C_PATTERNS))), NEG_BIG)
        _lane_attend([kn_ref], [vn_ref], qb_ref, bias_new, *state)
        _lane_attend_finish(o_ref, *state)


def _dilated_sample(q, k_new, v_new, cache_k, cache_v):
    DB, w_len = cache_k.shape[:2]
    width = min(DIL_CHUNK, w_len)
    per_seq = pl.BlockSpec((None, H_C, HEAD_DIM, LANES), lambda b, c: (b, 0, 0, 0))
    chunk = pl.BlockSpec((None, H_C, HEAD_DIM, width), lambda b, c: (b, 0, 0, c))
    return pl.pallas_call(
        functools.partial(_dilated_sample_kernel, w_len),
        grid=(DB, w_len // width),
        in_specs=[per_seq, per_seq, per_seq, chunk, chunk],
        out_specs=pl.BlockSpec((None, H_C, HEAD_DIM, 1), lambda b, c: (b, 0, 0, 0)),
        out_shape=jax.ShapeDtypeStruct((DB, H_C, HEAD_DIM, 1), F32),
        scratch_shapes=_lane_attend_scratch(H_C, width),
        compiler_params=_cparams(("parallel", "arbitrary")),
        name="dilated_sample",
    )(_on_lanes(q), _lane0(k_new), _lane0(v_new), _keys_on_lanes(cache_k), _keys_on_lanes(cache_v))


def _silu(x):
    return x * jax.nn.sigmoid(x)


def _outproj_ffn_kernel(x_ref, ya_ref, yb_ref, wo_ref, g_ref, wg_ref, wu_ref, wd_ref, o_ref, h_scr, hn_scr, acc_scr):
    j = pl.program_id(1)

    @pl.when(j == 0)
    def _():
        y = jnp.concatenate([ya_ref[...], yb_ref[...]], axis=1)
        h = x_ref[...] + jnp.dot(y, wo_ref[...], preferred_element_type=F32)
        h_scr[...] = h
        hn_scr[...] = _rms(h, g_ref[...]).astype(BF16)
        acc_scr[...] = jnp.zeros_like(acc_scr)

    hn = hn_scr[...]
    act = _silu(jnp.dot(hn, wg_ref[...], preferred_element_type=F32)) * jnp.dot(hn, wu_ref[...], preferred_element_type=F32)
    acc_scr[...] += jnp.dot(act.astype(BF16), wd_ref[...], preferred_element_type=F32)

    @pl.when(j == pl.num_programs(1) - 1)
    def _():
        o_ref[...] = h_scr[...] + acc_scr[...]


def _outproj_ffn(x, ya, yb, wo, g, wg, wu, wd, tm, tf):
    N = x.shape[0]
    nf = D_FF // tf
    tok = lambda width: pl.BlockSpec((tm, width), lambda i, j: (i, 0))
    full = lambda a: pl.BlockSpec(a.shape, lambda i, j: (0,) * a.ndim)
    return pl.pallas_call(
        _outproj_ffn_kernel,
        grid=(N // tm, nf),
        in_specs=[tok(D_MODEL), tok(D_A), tok(D_B), full(wo), full(g),
                  pl.BlockSpec((D_MODEL, tf), lambda i, j: (0, j)), pl.BlockSpec((D_MODEL, tf), lambda i, j: (0, j)),
                  pl.BlockSpec((tf, D_MODEL), lambda i, j: (j, 0))],
        out_specs=tok(D_MODEL),
        out_shape=jax.ShapeDtypeStruct((N, D_MODEL), F32),
        scratch_shapes=[pltpu.VMEM((tm, D_MODEL), F32), pltpu.VMEM((tm, D_MODEL), BF16), pltpu.VMEM((tm, D_MODEL), F32)],
        compiler_params=_cparams(("parallel", "arbitrary")),
        name="outproj_ffn",
    )(x, ya, yb, wo, g, wg, wu, wd)


def _inproj1_kernel(keep_tiles, x_ref, g_ref, w_ref, cos_ref, sin_ref, q_ref, k_ref, v_ref, kf_ref, vf_ref, *streams):
    h = _rms(x_ref[...], g_ref[...]).astype(BF16)
    p = jnp.dot(h, w_ref[...], preferred_element_type=F32)
    cos = cos_ref[...]
    sin = sin_ref[...]
    q = _rope(p[:, :D_C], cos, sin) * (HEAD_DIM ** -0.5)
    k = _rope(p[:, D_C:2 * D_C], cos, sin)
    v = p[:, 2 * D_C:]
    q_ref[...] = q.astype(BF16)
    k_ref[...] = k.astype(BF16)
    v_ref[...] = v.astype(BF16)
    if keep_tiles is None:
        kf_ref[...] = k
        vf_ref[...] = v
    else:
        scr = streams[-1]
        n_groups, tm, _ = scr.shape
        n_dil = len(C_PATTERNS) - 1
        for which, val in enumerate((q, k, v)):
            for g in range(n_groups):
                scr[g] = val[:, g * LANES:(g + 1) * LANES]
            for di, (_, dil) in enumerate(C_PATTERNS[1:]):
                out = streams[which * n_dil + di]
                for r in range(dil):
                    for g in range(n_groups):
                        out[:, r * D_C + g * LANES:r * D_C + (g + 1) * LANES] = (
                            scr[g, pl.ds(r, tm // dil, stride=dil), :].astype(BF16))

        @pl.when(pl.program_id(1) >= pl.num_programs(1) - keep_tiles)
        def _():
            kf_ref[...] = k.T
            vf_ref[...] = v.T


def _inproj1(x, g, w, cos, sin, tm, keep=None):
    B, T, _ = x.shape
    tok = pl.BlockSpec((None, tm, D_C), lambda b, t: (b, t, 0))
    full = lambda a: pl.BlockSpec(a.shape, lambda b, t: (0,) * a.ndim)
    tab = pl.BlockSpec((tm, LANES), lambda b, t: (t, 0))
    bf = jax.ShapeDtypeStruct((B, T, D_C), BF16)
    extra_specs, extra_shapes, scratch = [], [], []
    if keep is None:
        keep_tiles, f_spec, ff = None, tok, jax.ShapeDtypeStruct((B, T, D_C), F32)
    else:
        keep_tiles = keep // tm
        first = T // tm - keep_tiles
        f_spec = pl.BlockSpec((None, D_C, tm), lambda b, t: (b, 0, jnp.maximum(t - first, 0)))
        ff = jax.ShapeDtypeStruct((B, D_C, keep), F32)
        for _ in range(3):
            for _, dil in C_PATTERNS[1:]:
                extra_specs.append(pl.BlockSpec((None, tm // dil, dil * D_C), lambda b, t: (b, t, 0)))
                extra_shapes.append(jax.ShapeDtypeStruct((B, T // dil, dil * D_C), BF16))
        scratch = [pltpu.VMEM((D_C // LANES, tm, LANES), F32)]
    return pl.pallas_call(
        functools.partial(_inproj1_kernel, keep_tiles),
        grid=(B, T // tm),
        in_specs=[tok, full(g), full(w), tab, tab],
        out_specs=[tok, tok, tok, f_spec, f_spec] + extra_specs,
        out_shape=[bf, bf, bf, ff, ff] + extra_shapes,
        scratch_shapes=scratch,
        compiler_params=_cparams(("parallel", "arbitrary")),
        name="inproj1",
    )(x, g, w, cos, sin)


def _dilated_kernel(lookback, q_ref, kc_ref, kp_ref, vc_ref, vp_ref, o_ref, lse_ref):
    QB = C_BLOCK
    c = pl.program_id(2)
    lane128 = _iota((QB, LANES), 1)
    upper = (lane128 // HEAD_DIM) == 1
    upper_v = (_iota((2 * QB, LANES), 1) // HEAD_DIM) == 1
    qi = _iota((QB, 2 * QB), 0)
    kj = _iota((QB, 2 * QB), 1)
    dist = qi + QB - kj
    ok = (dist >= 0) & (dist <= lookback) & ((kj >= QB) | (c > 0))
    bias = jnp.where(ok, 0.0, NEG_BIG)
    q = q_ref[...]
    lse_blk = jnp.zeros((QB, LANES), F32)
    outs = []
    for j in range(H_C // 2):
        sl = slice(j * LANES, (j + 1) * LANES)
        pair = q[:, sl]
        q2 = jnp.concatenate([jnp.where(upper, jnp.zeros_like(pair), pair),
                              jnp.where(upper, pair, jnp.zeros_like(pair))], axis=0)
        k2 = jnp.concatenate([kp_ref[:, sl], kc_ref[:, sl]], axis=0)
        v2 = jnp.concatenate([vp_ref[:, sl], vc_ref[:, sl]], axis=0)
        s2 = _dot_nt(q2, k2)
        vv = jnp.concatenate([jnp.where(upper_v, jnp.zeros_like(v2), v2),
                              jnp.where(upper_v, v2, jnp.zeros_like(v2))], axis=0)
        ps, ls = [], []
        for u in range(2):
            s = s2[u * QB:(u + 1) * QB] + bias
            m = jnp.max(s, axis=1, keepdims=True)
            p = jnp.exp(s - m)
            l = jnp.sum(p, axis=1, keepdims=True)
            ps.append(p.astype(BF16))
            ls.append(l)
            lse_blk = lse_blk + jnp.where(lane128 == 2 * j + u, m + jnp.log(l), 0.0)
        pv = jnp.dot(jnp.concatenate(ps, axis=1), vv, preferred_element_type=F32)
        outs.append(pv / jnp.where(upper, ls[1], ls[0]))
    o_ref[...] = jnp.concatenate(outs, axis=1).astype(BF16)
    lse_ref[...] = lse_blk


def _dilated_branch(q, k, v, window, dil):
    B, n, _ = q.shape
    view = lambda a: a
    cur = pl.BlockSpec((None, C_BLOCK, D_C), lambda b, r, c: (b, c, r))
    prev = pl.BlockSpec((None, C_BLOCK, D_C), lambda b, r, c: (b, jnp.maximum(c - 1, 0), r))
    o, lse = pl.pallas_call(
        functools.partial(_dilated_kernel, window // dil),
        grid=(B, dil, n // C_BLOCK),
        in_specs=[cur, cur, prev, cur, prev],
        out_specs=[cur, pl.BlockSpec((None, C_BLOCK, LANES), lambda b, r, c: (b, c, r))],
        out_shape=[jax.ShapeDtypeStruct((B, n, dil * D_C), BF16), jax.ShapeDtypeStruct((B, n, dil * LANES), F32)],
        compiler_params=_cparams(("parallel", "parallel", "arbitrary")),
        name="dilated_w%d_d%d" % (window, dil),
    )(view(q), view(k), view(k), view(v), view(v))
    return o, lse


def _merge_outproj_kernel(x_ref, o1_ref, o2_ref, o3_ref, l1_ref, l2_ref, l3_ref, ex_ref, wo_ref, out_ref,
                          o_scr, l_scr):
    tm = x_ref.shape[0]

    def in_token_order(o_ref, l_ref, slot, dil):
        if dil == 1:
            return o_ref[...].astype(F32), l_ref[...]
        n_groups = D_C // LANES
        for r in range(dil):
            rows = pl.ds(r, tm // dil, stride=dil)
            for g in range(n_groups):
                o_scr[slot, g, rows, :] = o_ref[:, r * D_C + g * LANES:r * D_C + (g + 1) * LANES].astype(F32)
            l_scr[slot, rows, :] = l_ref[:, r * LANES:(r + 1) * LANES]
        return jnp.concatenate([o_scr[slot, g] for g in range(n_groups)], axis=1), l_scr[slot]

    branches = [in_token_order(o_ref, l_ref, i - 1, dil) for i, (o_ref, l_ref, (_, dil)) in
                enumerate(zip((o1_ref, o2_ref, o3_ref), (l1_ref, l2_ref, l3_ref), C_PATTERNS))]
    lses = [l for _, l in branches]
    m = jnp.maximum(jnp.maximum(lses[0], lses[1]), lses[2])
    es = [jnp.exp(l - m) for l in lses]
    inv = 1.0 / (es[0] + es[1] + es[2])
    y = jnp.zeros(x_ref.shape, F32)
    for e, (o, _) in zip(es, branches):
        y = y + _dot_split_lhs(e * inv, ex_ref[...]) * o
    out_ref[...] = x_ref[...] + jnp.dot(y.astype(BF16), wo_ref[...], preferred_element_type=F32)


def _head_expand_matrix():
    e = np.zeros((LANES, D_C), np.float32)
    for h in range(H_C):
        e[h, h * HEAD_DIM:(h + 1) * HEAD_DIM] = 1.0
    return jnp.asarray(e, BF16)


def _merge_outproj(x, os_, lses, wo, tm):
    N = x.shape[0]
    tok = lambda width: pl.BlockSpec((tm, width), lambda i: (i, 0))
    full = lambda a: pl.BlockSpec(a.shape, lambda i: (0,) * a.ndim)
    stream = lambda width, dil: pl.BlockSpec((tm // dil, dil * width), lambda i: (i, 0))
    ex = _head_expand_matrix()
    dils = [dil for _, dil in C_PATTERNS]
    return pl.pallas_call(
        _merge_outproj_kernel,
        grid=(N // tm,),
        in_specs=([tok(D_MODEL)] + [stream(D_C, d) for d in dils] + [stream(LANES, d) for d in dils]
                  + [full(ex), full(wo)]),
        out_specs=tok(D_MODEL),
        out_shape=jax.ShapeDtypeStruct((N, D_MODEL), F32),
        scratch_shapes=[pltpu.VMEM((2, D_C // LANES, tm, LANES), F32), pltpu.VMEM((2, tm, LANES), F32)],
        compiler_params=_cparams(("parallel",)),
        name="merge_outproj",
    )(x, *os_, *lses, ex, wo)


def _top2_gates(logits):
    lane = _iota(logits.shape, 1)
    m1 = jnp.max(logits, axis=1, keepdims=True)
    i1 = jnp.min(jnp.where(logits == m1, lane, LANES), axis=1, keepdims=True)
    rest = jnp.where(lane == i1, -jnp.inf, logits)
    m2 = jnp.max(rest, axis=1, keepdims=True)
    i2 = jnp.min(jnp.where(rest == m2, lane, LANES), axis=1, keepdims=True)
    e2 = jnp.exp(m2 - m1)
    g1 = 1.0 / (1.0 + e2)
    return jnp.where(lane == i1, g1, 0.0) + jnp.where(lane == i2, e2 * g1, 0.0)


def _moe_dense_kernel(x_ref, g_ref, rw_ref, rb_ref, wg_ref, wu_ref, wd_ref, gf_ref, o_ref, hn_scr, gate_scr, acc_scr):
    e = pl.program_id(1)
    j = pl.program_id(2)

    @pl.when((e == 0) & (j == 0))
    def _():
        hn = _rms(x_ref[...], g_ref[...])
        hn_scr[...] = hn.astype(BF16)
        logits = jnp.dot(hn.astype(BF16), rw_ref[...], preferred_element_type=F32) + rb_ref[...]
        gate_scr[...] = _top2_gates(logits)
        acc_scr[...] = jnp.zeros_like(acc_scr)

    hn = hn_scr[...]
    gate = gate_scr[...]
    gate_e = jnp.sum(jnp.where(_iota(gate.shape, 1) == e, gate, 0.0), axis=1, keepdims=True)
    act = _silu(jnp.dot(hn, wg_ref[...], preferred_element_type=F32)) * jnp.dot(hn, wu_ref[...], preferred_element_type=F32)
    acc_scr[...] += gate_e * jnp.dot(act.astype(BF16), wd_ref[...], preferred_element_type=F32)

    @pl.when((e == pl.num_programs(1) - 1) & (j == pl.num_programs(2) - 1))
    def _():
        o_ref[...] = _rms(x_ref[...] + acc_scr[...], gf_ref[...])


def _moe_dense(x, g, rw, rb, wg, wu, wd, gf, tm, tf):
    N = x.shape[0]
    tok = pl.BlockSpec((tm, D_MODEL), lambda i, e, j: (i, 0))
    full = lambda a: pl.BlockSpec(a.shape, lambda i, e, j: (0,) * a.ndim)
    return pl.pallas_call(
        _moe_dense_kernel,
        grid=(N // tm, N_EXPERTS, D_FF_EXPERT // tf),
        in_specs=[tok, full(g), full(rw), full(rb),
                  pl.BlockSpec((None, D_MODEL, tf), lambda i, e, j: (e, 0, j)),
                  pl.BlockSpec((None, D_MODEL, tf), lambda i, e, j: (e, 0, j)),
                  pl.BlockSpec((None, tf, D_MODEL), lambda i, e, j: (e, j, 0)), full(gf)],
        out_specs=tok,
        out_shape=jax.ShapeDtypeStruct((N, D_MODEL), F32),
        scratch_shapes=[pltpu.VMEM((tm, D_MODEL), BF16), pltpu.VMEM((tm, LANES), F32), pltpu.VMEM((tm, D_MODEL), F32)],
        compiler_params=_cparams(("parallel", "arbitrary", "arbitrary")),
        name="moe_dense",
    )(x, g, rw, rb, wg, wu, wd, gf)


MOE_TR = 1024
MOE_RC = 128
MOE_CAP = 2 * MOE_TR + N_EXPERTS * MOE_RC
MOE_PAD = 16


def _moe_route_kernel(x_ref, g_ref, rw_ref, rb_ref, hn_ref, gate_ref, pos_ref, rankt_ref, cnt_ref):
    TR = x_ref.shape[0]
    hn = _rms(x_ref[...], g_ref[...]).astype(BF16)
    hn_ref[...] = hn
    gate = _top2_gates(jnp.dot(hn, rw_ref[...], preferred_element_type=F32) + rb_ref[...])
    gate_ref[...] = gate
    sel = jnp.where(gate.T[:MOE_PAD] > 0.0, 1.0, 0.0)
    triu = jnp.where(_iota((LANES, LANES), 0) <= _iota((LANES, LANES), 1), 1.0, 0.0).astype(BF16)
    carry = jnp.zeros((MOE_PAD, 1), F32)
    ranks = []
    for c in range(TR // LANES):
        blk = sel[:, c * LANES:(c + 1) * LANES]
        pref = jnp.dot(blk.astype(BF16), triu, preferred_element_type=F32)
        ranks.append(jnp.where(blk > 0.0, carry + pref - 1.0, -1.0))
        carry = carry + pref[:, LANES - 1:]
    rank_t = jnp.concatenate(ranks, axis=1)
    rankt_ref[...] = rank_t[:N_EXPERTS]
    padded = jnp.ceil(carry * (1.0 / MOE_RC)) * MOE_RC
    lower = jnp.where(_iota((MOE_PAD, MOE_PAD), 0) > _iota((MOE_PAD, MOE_PAD), 1), 1.0, 0.0).astype(BF16)
    offs = jnp.dot(lower, jnp.broadcast_to(padded, (MOE_PAD, LANES)).astype(BF16), preferred_element_type=F32)
    cnt_ref[0:N_EXPERTS, :] = jnp.broadcast_to(carry[:N_EXPERTS], (N_EXPERTS, LANES))
    cnt_ref[N_EXPERTS:, :] = offs[:N_EXPERTS]
    pos_t = jnp.where(rank_t >= 0.0, rank_t + offs[:, :1], -1.0)
    p1 = jnp.max(pos_t, axis=0, keepdims=True)
    p2 = jnp.max(jnp.where(pos_t == p1, -1.0, pos_t), axis=0, keepdims=True)
    rows = _iota((LANES, TR), 0)
    pos_ref[...] = jnp.where(rows == 0, p1, jnp.where(rows == 1, p2, -1.0)).T


def _moe_route(x, g, rw, rb):
    N = x.shape[0]
    nt = N // MOE_TR
    tok = lambda width: pl.BlockSpec((MOE_TR, width), lambda i: (i, 0))
    full = lambda a: pl.BlockSpec(a.shape, lambda i: (0,) * a.ndim)
    return pl.pallas_call(
        _moe_route_kernel,
        grid=(nt,),
        in_specs=[tok(D_MODEL), full(g), full(rw), full(rb)],
        out_specs=[tok(D_MODEL), tok(LANES), tok(LANES), pl.BlockSpec((N_EXPERTS, MOE_TR), lambda i: (0, i)),
                   pl.BlockSpec((None, 2 * N_EXPERTS, LANES), lambda i: (i, 0, 0))],
        out_shape=[jax.ShapeDtypeStruct((N, D_MODEL), BF16), jax.ShapeDtypeStruct((N, LANES), F32),
                   jax.ShapeDtypeStruct((N, LANES), F32), jax.ShapeDtypeStruct((N_EXPERTS, N), F32),
                   jax.ShapeDtypeStruct((nt, 2 * N_EXPERTS, LANES), F32)],
        compiler_params=_cparams(("parallel",)),
        name="moe_route",
    )(x, g, rw, rb)


def _moe_routed_kernel(meta_ref, x_ref, hn_ref, gate_ref, pos_ref, rankt_ref, wg_ref, wu_ref, wd_ref, gf_ref, o_ref,
                       xc_scr, y_scr, grow_scr):
    i = pl.program_id(0)
    e = pl.program_id(1)
    j = pl.program_id(2)
    TR, RC = MOE_TR, MOE_RC
    n_chunks = lax.div(meta_ref[i, e] + (RC - 1), RC)
    off = meta_ref[i, N_EXPERTS + e]

    @pl.when((e == 0) & (j == 0))
    def _():
        y_scr[...] = jnp.zeros_like(y_scr)
        grow_scr[...] = jnp.zeros_like(grow_scr)

    @pl.when(j == 0)
    def _():
        rank_row = rankt_ref[pl.ds(e, 1), :]
        gate = gate_ref[...]
        gate_e = jnp.where(_iota(gate.shape, 1) == e, gate, 0.0)

        def gather(c, carry):
            rows = pl.multiple_of(off + c * RC, RC)
            want = (c * RC + _iota((RC, 1), 0)).astype(F32)
            onehot = jnp.where(rank_row == want, 1.0, 0.0).astype(BF16)
            xc_scr[pl.ds(rows, RC), :] = jnp.dot(onehot, hn_ref[...], preferred_element_type=F32).astype(BF16)
            grow_scr[pl.ds(rows, RC), :] = _dot_split_rhs(onehot, gate_e)
            return carry

        lax.fori_loop(0, n_chunks, gather, 0)

    def expert(c, carry):
        rows = pl.multiple_of(off + c * RC, RC)
        xc = xc_scr[pl.ds(rows, RC), :]
        act = _silu(jnp.dot(xc, wg_ref[...], preferred_element_type=F32)) * jnp.dot(xc, wu_ref[...], preferred_element_type=F32)
        y_scr[pl.ds(rows, RC), :] += jnp.dot(act.astype(BF16), wd_ref[...], preferred_element_type=F32)
        return carry

    lax.fori_loop(0, n_chunks, expert, 0)

    @pl.when((e == pl.num_programs(1) - 1) & (j == pl.num_programs(2) - 1))
    def _():
        step = 256
        for r in range(MOE_CAP // step):
            sl = slice(r * step, (r + 1) * step)
            gr = jnp.sum(grow_scr[sl, :], axis=1, keepdims=True)
            xc_scr[sl, :] = (y_scr[sl, :] * gr).astype(BF16)
        lane = _iota((step, MOE_CAP), 1).astype(F32)
        for r in range(TR // step):
            sl = slice(r * step, (r + 1) * step)
            pos = pos_ref[sl, :]
            scatter = jnp.where(lane == pos[:, 0:1], 1.0, jnp.where(lane == pos[:, 1:2], 1.0, 0.0)).astype(BF16)
            y = jnp.dot(scatter, xc_scr[...], preferred_element_type=F32)
            o_ref[sl, :] = _rms(x_ref[sl, :] + y, gf_ref[...])


def _moe_routed(x, g, rw, rb, wg, wu, wd, gf, tf):
    N = x.shape[0]
    nt = N // MOE_TR
    hn, gate, pos, rank_t, meta = _moe_route(x, g, rw, rb)
    meta = meta[:, :, 0].astype(I32)
    tok = lambda width: pl.BlockSpec((MOE_TR, width), lambda i, e, j, m: (i, 0))
    full = lambda a: pl.BlockSpec(a.shape, lambda i, e, j, m: (0,) * a.ndim)
    grid_spec = pltpu.PrefetchScalarGridSpec(
        num_scalar_prefetch=1,
        grid=(nt, N_EXPERTS, D_FF_EXPERT // tf),
        in_specs=[tok(D_MODEL), tok(D_MODEL), tok(LANES), tok(LANES),
                  pl.BlockSpec((N_EXPERTS, MOE_TR), lambda i, e, j, m: (0, i)),
                  pl.BlockSpec((None, D_MODEL, tf), lambda i, e, j, m: (e, 0, j)),
                  pl.BlockSpec((None, D_MODEL, tf), lambda i, e, j, m: (e, 0, j)),
                  pl.BlockSpec((None, tf, D_MODEL), lambda i, e, j, m: (e, j, 0)), full(gf)],
        out_specs=tok(D_MODEL),
        scratch_shapes=[pltpu.VMEM((MOE_CAP, D_MODEL), BF16), pltpu.VMEM((MOE_CAP, D_MODEL), F32),
                        pltpu.VMEM((MOE_CAP, LANES), F32)],
    )
    return pl.pallas_call(
        _moe_routed_kernel,
        grid_spec=grid_spec,
        out_shape=jax.ShapeDtypeStruct((N, D_MODEL), F32),
        compiler_params=_cparams(("parallel", "arbitrary", "arbitrary")),
        name="moe_routed",
    )(meta, x, hn, gate, pos, rank_t, wg, wu, wd, gf)


def _outproj_kernel(x_ref, y_ref, wo_ref, o_ref):
    o_ref[...] = x_ref[...] + jnp.dot(y_ref[...], wo_ref[...], preferred_element_type=F32)


def _outproj(x, y, wo, tm):
    N = x.shape[0]
    tok = lambda width: pl.BlockSpec((tm, width), lambda i: (i, 0))
    return pl.pallas_call(
        _outproj_kernel,
        grid=(N // tm,),
        in_specs=[tok(D_MODEL), tok(y.shape[1]), pl.BlockSpec(wo.shape, lambda i: (0, 0))],
        out_specs=tok(D_MODEL),
        out_shape=jax.ShapeDtypeStruct((N, D_MODEL), F32),
        compiler_params=_cparams(("parallel",)),
        name="outproj",
    )(x, y, wo)


def _pad_router(router_w, router_b):
    rw = jnp.concatenate([router_w, jnp.zeros((D_MODEL, LANES - N_EXPERTS), router_w.dtype)], axis=1).astype(BF16)
    rb = jnp.concatenate([router_b.astype(F32), jnp.full((LANES - N_EXPERTS,), NEG_BIG, F32)]).reshape(1, LANES)
    return rw, rb


def _pad_w_in0(w_in_0):
    o = A_COLS + 3 * D_B + H_IDX * D_IDX
    wi = w_in_0[:, o:o + H_IDX]
    ki = w_in_0[:, o + H_IDX:o + H_IDX + D_IDX]
    pad = jnp.zeros((D_MODEL, LANES - D_IDX - H_IDX), w_in_0.dtype)
    return jnp.concatenate([w_in_0[:, :o], ki, wi, pad], axis=1).astype(BF16)


def _rwkv_params(w):
    row = lambda a: a.reshape(1, -1).astype(F32)
    zeros = jnp.zeros((DECAY_LORA, D_A), F32)
    return {
        "mu": row(w["a_mu"]), "w0": row(w["a_w0"]), "a0": row(w["a_a0"]),
        "w2": jnp.concatenate([w["a_w2"], zeros], axis=0).astype(BF16),
        "a2": jnp.concatenate([zeros, w["a_a2"]], axis=0).astype(BF16),
        "g2": w["a_g2"].astype(BF16),
        "kk": row(w["a_kk"]), "ka": row(w["a_ka"]), "rk": row(w["a_rk"]),
        "bd": _block_diag_ones(D_A), "lnw": row(w["a_ln_w"]), "lnb": row(w["a_ln_b"]),
    }


DSA_QB = 128


def _order_key(score):
    bits = lax.bitcast_convert_type(score, I32)
    key = jnp.where(bits < 0, bits ^ jnp.int32(0x7FFFFFFF), bits)
    return jnp.where(score == 0.0, 0, key)


def _fold_lanes(x):
    part = x[:, :LANES]
    for j in range(1, x.shape[1] // LANES):
        part = part + x[:, j * LANES:(j + 1) * LANES]
    return part


def _fold_rows(x, rows=64):
    rows = min(rows, x.shape[0])
    part = x[:rows]
    for j in range(1, x.shape[0] // rows):
        part = part + x[j * rows:(j + 1) * rows]
    return part


def _kth_largest_key(count, topk, shape):
    kf = float(topk)
    base = jnp.where(count(jnp.zeros(shape, I32)) >= kf, 0, INT_MIN).astype(I32)

    def bit_body(i, base):
        cand = base + lax.shift_left(jnp.int32(1), 30 - i)
        return jnp.where(count(cand) >= kf, cand, base)

    return lax.fori_loop(0, 31, bit_body, base)


def _dsa_prompt_kernel(topk, KB, KA, idx_bits, qt_ref, k_ref, vt_ref, qit_ref, wit_ref, ki_ref, o_ref,
                       key_scr, thr_scr, s_scr, bias_scr, p_scr, m_scr, l_scr, acc_scr):
    QB = DSA_QB
    q_pos0 = pl.program_id(1) * QB
    nkb = lax.div(q_pos0 + QB - 1, KB) + 1
    nka = lax.div(q_pos0 + QB - 1, KA) + 1
    key_row = _iota((KB, QB), 0)
    q_lane = _iota((KB, QB), 1)

    qit = qit_ref[...]
    q_cat = jnp.concatenate([qit[h * D_IDX:(h + 1) * D_IDX] for h in range(H_IDX)], axis=1)
    wit = wit_ref[...]
    w_cat = jnp.concatenate([wit[h:h + 1] for h in range(H_IDX)], axis=1)

    def score_body(kb, carry):
        off = pl.multiple_of(kb * KB, KB)
        res = jnp.dot(ki_ref[pl.ds(off, KB), :], q_cat, preferred_element_type=F32)
        weighted = jnp.maximum(res, 0.0) * w_cat
        admissible = (off + key_row) <= (q_pos0 + q_lane)
        key_scr[pl.ds(off, KB), :] = jnp.where(admissible, _order_key(_fold_lanes(weighted)), INT_MIN)
        return carry

    lax.fori_loop(0, nkb, score_body, 0)

    def count_where(pred):
        def body(kb, acc):
            off = pl.multiple_of(kb * KB, KB)
            return acc + _fold_rows(pred(key_scr[pl.ds(off, KB), :], off + key_row))
        acc = lax.fori_loop(0, nkb, body, jnp.zeros((min(64, KB), QB), F32))
        return jnp.sum(acc, axis=0, keepdims=True)

    kf = float(topk)
    count_ge = lambda cand: count_where(lambda blk, idx: jnp.where(blk >= cand, 1.0, 0.0))
    n0 = count_ge(jnp.zeros((1, QB), I32))
    start = (jnp.int32(0), jnp.where(n0 >= kf, 0, INT_MIN).astype(I32),
             jnp.where(n0 >= kf, n0, float(jnp.iinfo(jnp.int32).max)))

    BITS_PER_CHECK = 4

    def unsettled(state):
        i, _, n_base = state
        return (i < 31) & (jnp.max(jnp.abs(n_base - kf)) > 0.0)

    def bit_step(j, state):
        i, base, n_base = state
        bit = 30 - i
        cand = base + jnp.where(bit >= 0, lax.shift_left(jnp.int32(1), jnp.maximum(bit, 0)), 0)
        n = count_ge(cand)
        take = n >= kf
        return i + 1, jnp.where(take, cand, base), jnp.where(take, n, n_base)

    _, thr, n_ge = lax.while_loop(unsettled, lambda st: lax.fori_loop(0, BITS_PER_CHECK, bit_step, st), start)
    thr_scr[...] = jnp.full((8, QB), 2 ** 30, I32)

    @pl.when(jnp.max(jnp.where(n_ge > kf, 1.0, 0.0) * jnp.where(thr > INT_MIN, 1.0, 0.0)) > 0.0)
    def _():
        need = kf - count_where(lambda blk, idx: jnp.where(blk > thr, 1.0, 0.0))

        def bit_body(i, j):
            cand = j + lax.shift_left(jnp.int32(1), idx_bits - 1 - i)
            ties = count_where(lambda blk, idx: jnp.where(blk == thr, jnp.where(idx <= cand, 1.0, 0.0), 0.0))
            return jnp.where(ties < need, cand, j)
        j = lax.fori_loop(0, idx_bits, bit_body, jnp.full((1, QB), -1, I32))
        thr_scr[...] = jnp.broadcast_to(j + 1, (8, QB))

    idx_thr = thr_scr[0:1, :]
    tie_bias = jnp.where(thr == INT_MIN, NEG_BIG, 0.0)

    qt = qt_ref[...]
    feat = _iota((LANES, QB), 0)
    qtm = []
    for h in range(H_B):
        pair = qt[(h // 2) * LANES:(h // 2 + 1) * LANES]
        qtm.append(jnp.where((feat // HEAD_DIM) == (h % 2), pair, jnp.zeros_like(pair)))
    q_pairs = [jnp.concatenate([qtm[2 * j], qtm[2 * j + 1]], axis=1) for j in range(H_B // 2)]
    key_row_a = _iota((KA, QB), 0)

    def scores_into(slot, ka):
        off = pl.multiple_of(ka * KA, KA)
        for j in range(H_B // 2):
            s2 = jnp.dot(k_ref[pl.ds(off, KA), j * LANES:(j + 1) * LANES], q_pairs[j], preferred_element_type=F32)
            s_scr[slot, 2 * j] = s2[:, :QB]
            s_scr[slot, 2 * j + 1] = s2[:, QB:]

    scores_into(0, 0)

    m_scr[...] = jnp.full(m_scr.shape, NEG_BIG, F32)
    l_scr[...] = jnp.zeros_like(l_scr)
    acc_scr[...] = jnp.zeros_like(acc_scr)
    CH = 64
    n_ch = KA // CH

    def attend_block(slot, ka):
        live = ka < nka
        ka = jnp.minimum(ka, nka - 1)
        off = pl.multiple_of(ka * KA, KA)
        blk = key_scr[pl.ds(off, KA), :]
        hit = jnp.where(live, 0.0, NEG_BIG)
        bias_scr[...] = jnp.where(blk > thr, hit, jnp.where(
            blk == thr, jnp.where((off + key_row_a) <= idx_thr, tie_bias + hit, NEG_BIG), NEG_BIG))
        for h in range(H_B):
            top = None
            for c in range(n_ch):
                rows = slice(c * CH, (c + 1) * CH)
                piece = s_scr[slot, h, rows, :] + bias_scr[rows, :]
                s_scr[slot, h, rows, :] = piece
                top = piece if top is None else jnp.maximum(top, piece)
            m_old = m_scr[h:h + 1, :]
            m_new = jnp.maximum(m_old, jnp.max(top, axis=0, keepdims=True))
            alpha = jnp.exp2(m_old - m_new)
            total = None
            for c in range(n_ch):
                rows = slice(c * CH, (c + 1) * CH)
                e = jnp.exp2(s_scr[slot, h, rows, :] - m_new)
                p_scr[h, rows, :] = e.astype(BF16)
                total = e if total is None else total + e
            m_scr[h:h + 1, :] = m_new
            l_scr[h:h + 1, :] = alpha * l_scr[h:h + 1, :] + jnp.sum(total, axis=0, keepdims=True)
            pv = jnp.dot(vt_ref[h * HEAD_DIM:(h + 1) * HEAD_DIM, pl.ds(off, KA)], p_scr[h],
                         preferred_element_type=F32)
            acc_scr[h] = acc_scr[h] * alpha + pv

    def attn_body(i, carry):
        scores_into(1, jnp.minimum(2 * i + 1, nka - 1))
        attend_block(0, 2 * i)
        scores_into(0, jnp.minimum(2 * i + 2, nka - 1))
        attend_block(1, 2 * i + 1)
        return carry

    lax.fori_loop(0, lax.div(nka + 1, 2), attn_body, 0)
    out_t = jnp.concatenate([acc_scr[h] / l_scr[h:h + 1, :] for h in range(H_B)], axis=0)
    o_ref[...] = out_t.T.astype(BF16)


def _dsa_prompt(qt, k, vt, qit, wit, ki):
    B, T, _ = k.shape
    topk = min(IDX_TOPK_MAX, T // 4)
    kb = min(512, T)
    ka = min(512, T)
    idx_bits = max(1, int(np.ceil(np.log2(T))))
    qcols = lambda rows: pl.BlockSpec((None, rows, DSA_QB), lambda b, i: (b, 0, i))
    whole = lambda r, c: pl.BlockSpec((None, r, c), lambda b, i: (b, 0, 0))
    return pl.pallas_call(
        functools.partial(_dsa_prompt_kernel, topk, kb, ka, idx_bits),
        grid=(B, T // DSA_QB),
        in_specs=[qcols(D_B), whole(T, D_B), whole(D_B, T), qcols(D_B), qcols(H_IDX), whole(T, D_IDX)],
        out_specs=pl.BlockSpec((None, DSA_QB, D_B), lambda b, i: (b, i, 0)),
        out_shape=jax.ShapeDtypeStruct((B, T, D_B), BF16),
        scratch_shapes=[pltpu.VMEM((T, DSA_QB), I32), pltpu.VMEM((8, DSA_QB), I32),
                        pltpu.VMEM((2, H_B, ka, DSA_QB), F32), pltpu.VMEM((ka, DSA_QB), F32),
                        pltpu.VMEM((H_B, ka, DSA_QB), BF16), pltpu.VMEM((H_B, DSA_QB), F32),
                        pltpu.VMEM((H_B, DSA_QB), F32), pltpu.VMEM((H_B, HEAD_DIM, DSA_QB), F32)],
        compiler_params=_cparams(("parallel", "arbitrary")),
        name="dsa_prompt",
    )(qt, k, vt, qit, wit, ki)


def _tile(n, pref):
    return pref if n % pref == 0 else n


def kernel(x_prompt, x_sample, state_a_wkv, state_a_shift, cache_b_k, cache_b_v, cache_b_kidx, cache_c_k, cache_c_v, page_table, norm_mix, norm_ffn, norm_final, w_in_0, w_out_0, a_mu, a_w0, a_w2, a_a0, a_a2, a_g2, a_kk, a_ka, a_rk, a_ln_w, a_ln_b, ffn_wg, ffn_wu, ffn_wd, w_in_1, w_out_1, router_w, router_b, moe_wg, moe_wu, moe_wd):
    B, T, D = x_prompt.shape
    DB, S, _ = x_sample.shape
    assert S == 1 and D == D_MODEL
    past = page_table.shape[1] * PAGE_SIZE
    row = lambda a: a.reshape(1, -1).astype(F32)
    b16 = lambda a: a.astype(BF16)

    prm = _rwkv_params(dict(a_mu=a_mu, a_w0=a_w0, a_w2=a_w2, a_a0=a_a0, a_a2=a_a2, a_g2=a_g2, a_kk=a_kk, a_ka=a_ka,
                            a_rk=a_rk, a_ln_w=a_ln_w, a_ln_b=a_ln_b))
    w_in0 = _pad_w_in0(w_in_0)
    w_out0, w_in1, w_out1 = b16(w_out_0), b16(w_in_1), b16(w_out_1)
    f_wg, f_wu, f_wd = b16(ffn_wg), b16(ffn_wu), b16(ffn_wd)
    m_wg, m_wu, m_wd = b16(moe_wg), b16(moe_wu), b16(moe_wd)
    rw, rb = _pad_router(router_w, router_b)
    g_mix0, g_mix1 = row(norm_mix[0]), row(norm_mix[1])
    g_ffn0, g_ffn1, g_fin = row(norm_ffn[0]), row(norm_ffn[1]), row(norm_final)
    tf_ffn = D_FF // 2
    tf_moe = 896

    N = B * T
    cos_p, sin_p = _rope_tables(jnp.arange(T, dtype=I32))
    pa, kf, vf, kif, qt, kb, vt, qit, wit, kib = _inproj0(x_prompt, g_mix0, w_in0, cos_p, sin_p, _tile(T, 256), True)
    prep = _rwkv_prep(pa, jnp.zeros((B, A_COLS), F32), prm, _tile(T, 256), True)
    ya, p_a_wkv = _rwkv_scan(*prep, prm["lnw"], prm["lnb"], _tile(T, 128))
    yb = _dsa_prompt(qt, kb, vt, qit, wit, kib)
    h = _outproj_ffn(x_prompt.reshape(N, D), ya.reshape(N, D_A), yb.reshape(N, D_B), w_out0, g_ffn0, f_wg, f_wu, f_wd,
                     _tile(N, 512), tf_ffn)
    keep = min(C_WINDOW_MAX, T)
    q1, k1, v1, k1f, v1f, *streams = _inproj1(h.reshape(B, T, D), g_mix1, w_in1, cos_p, sin_p, _tile(T, 256), keep)
    n_dil = len(C_PATTERNS) - 1
    outs, lses = [], []
    for i, (window, dil) in enumerate(C_PATTERNS):
        qkv = (q1, k1, v1) if i == 0 else tuple(streams[which * n_dil + i - 1] for which in range(3))
        o, lse = _dilated_branch(*qkv, window, dil)
        outs.append(o.reshape(N // dil, dil * D_C))
        lses.append(lse.reshape(N // dil, dil * LANES))
    h = _merge_outproj(h, outs, lses, w_out1, _tile(N, 512))
    if N % MOE_TR == 0:
        y_prompt = _moe_routed(h, g_ffn1, rw, rb, m_wg, m_wu, m_wd, g_fin, tf_moe).reshape(B, T, D)
    else:
        y_prompt = _moe_dense(h, g_ffn1, rw, rb, m_wg, m_wu, m_wd, g_fin, N, tf_moe).reshape(B, T, D)
    rows = lambda a, n: jnp.moveaxis(a.reshape(B, n, HEAD_DIM, a.shape[-1]), -1, 1)
    prompt_state = (p_a_wkv, pa[:, -1], rows(kf, H_B), rows(vf, H_B), jnp.swapaxes(kif, 1, 2),
                    rows(k1f, H_C), rows(v1f, H_C))

    cos_s, sin_s = _rope_tables(jnp.full((DB,), past, I32))
    xs = x_sample.reshape(1, DB, D)
    pa, kf, vf, kif, q, qi, tail = _inproj0(xs, g_mix0, w_in0, cos_s, sin_s, DB, False)
    prep = _rwkv_prep(pa, state_a_shift.astype(F32), prm, DB, False)
    ya, s_a_wkv = _rwkv_step(state_a_wkv.astype(F32), *(a.reshape(DB, D_A) for a in prep), prm["lnw"], prm["lnb"])
    pad8 = lambda a: jnp.concatenate([a, jnp.zeros_like(a)], axis=1)
    qi16 = pad8(qi.reshape(DB, H_IDX, D_IDX))
    wi16 = pad8((tail[0, :, D_IDX:D_IDX + H_IDX] * IDX_SCALE).reshape(DB, H_IDX, 1))
    ki16 = jnp.broadcast_to(kif.reshape(DB, 1, D_IDX), (DB, 16, D_IDX))
    bias, bias_new = _dsa_sample_select(page_table, qi16, wi16, ki16, jnp.swapaxes(cache_b_kidx.astype(F32), 1, 2))
    heads = lambda a, n: a.astype(F32).reshape(DB, n, HEAD_DIM)
    yb = _dsa_sample_attend(page_table, heads(q, H_B), heads(kf, H_B), heads(vf, H_B), bias, bias_new,
                            cache_b_k.astype(F32), cache_b_v.astype(F32))
    hs = _outproj_ffn(x_sample.reshape(DB, D), ya, yb.reshape(DB, D_B).astype(BF16), w_out0, g_ffn0, f_wg, f_wu, f_wd,
                      DB, tf_ffn)
    q1, k1, v1, k1f, v1f = _inproj1(hs.reshape(1, DB, D), g_mix1, w_in1, cos_s, sin_s, DB)
    yc = _dilated_sample(heads(q1, H_C), heads(k1f, H_C), heads(v1f, H_C), cache_c_k.astype(F32), cache_c_v.astype(F32))
    hs = _outproj(hs, yc.reshape(DB, D_C).astype(BF16), w_out1, DB)
    y_sample = _moe_dense(hs, g_ffn1, rw, rb, m_wg, m_wu, m_wd, g_fin, DB, tf_moe).reshape(DB, 1, D)
    keep = min(C_WINDOW_MAX, cache_c_k.shape[1] + 1)
    s_c_k = jnp.concatenate([cache_c_k, k1f.reshape(DB, 1, H_C, HEAD_DIM)], axis=1)[:, -keep:]
    s_c_v = jnp.concatenate([cache_c_v, v1f.reshape(DB, 1, H_C, HEAD_DIM)], axis=1)[:, -keep:]
    sample_state = (s_a_wkv, pa[0], kf.reshape(DB, 1, H_B, HEAD_DIM), vf.reshape(DB, 1, H_B, HEAD_DIM),
                    kif.reshape(DB, 1, D_IDX), s_c_k, s_c_v)
    return (y_prompt, y_sample) + prompt_state + sample_state
```

```python
import functools

import numpy as np
import jax
import jax.numpy as jnp
from jax import lax
from jax.experimental import pallas as pl
from jax.experimental.pallas import tpu as pltpu

F32 = jnp.float32
BF16 = jnp.bfloat16
I32 = jnp.int32

D_MODEL = 1024
HEAD_DIM = 64
ROPE_THETA = 10000.0
NORM_EPS = 1e-6
PAGE_SIZE = 128

H_A = 8
D_A = H_A * HEAD_DIM
DECAY_LORA = 64
AAA_LORA = 64
GATE_LORA = 128
A_COLS = 3 * D_A + DECAY_LORA + AAA_LORA + GATE_LORA
GN_EPS = 64e-5

H_B = 8
D_B = H_B * HEAD_DIM
H_IDX = 8
D_IDX = 64
IDX_TOPK_MAX = 256
IDX_SCALE = (H_IDX ** -0.5) * (D_IDX ** -0.5)
B_COLS_PAD = 3 * D_B + H_IDX * D_IDX + 128

H_C = 16
D_C = H_C * HEAD_DIM
C_PATTERNS = ((128, 1), (512, 4), (2048, 16))
C_WINDOW_MAX = 2048
C_BLOCK = 128

D_FF = 2816
N_EXPERTS = 8
D_FF_EXPERT = 3584

LANES = 128
VMEM_LIMIT = 56 << 20
INT_MIN = -(2 ** 31)
NEG_BIG = -1e30
LOG2E = 1.4426950408889634


def _cparams(sem, vmem=VMEM_LIMIT):
    return pltpu.CompilerParams(dimension_semantics=sem, vmem_limit_bytes=vmem)


def _dot(a, b):
    return jnp.dot(a.astype(BF16), b.astype(BF16), preferred_element_type=F32)


def _dot_nt(a, b):
    return lax.dot_general(a.astype(BF16), b.astype(BF16), (((1,), (1,)), ((), ())), preferred_element_type=F32)


def _split(x):
    hi = x.astype(BF16)
    lo = (x - hi.astype(F32)).astype(BF16)
    return hi, lo


def _dot_split_lhs(a, b_exact):
    hi, lo = _split(a)
    return jnp.dot(hi, b_exact, preferred_element_type=F32) + jnp.dot(lo, b_exact, preferred_element_type=F32)


def _dot_split_rhs(a_exact, b):
    hi, lo = _split(b)
    return jnp.dot(a_exact, hi, preferred_element_type=F32) + jnp.dot(a_exact, lo, preferred_element_type=F32)


def _dot3(a, b):
    ah, al = _split(a)
    bh, bl = _split(b)
    return (jnp.dot(ah, bh, preferred_element_type=F32) + jnp.dot(ah, bl, preferred_element_type=F32)
            + jnp.dot(al, bh, preferred_element_type=F32))


def _iota(shape, axis):
    return lax.broadcasted_iota(I32, shape, axis)


def _rope_tables(pos):
    half = HEAD_DIM // 2
    inv_freq = jnp.power(ROPE_THETA, -jnp.arange(half, dtype=F32) / half)
    ang = pos.astype(F32)[:, None] * inv_freq[None, :]
    cos = jnp.cos(ang)
    sin = jnp.sin(ang)
    return jnp.tile(cos, (1, 4)), jnp.tile(jnp.concatenate([-sin, sin], axis=1), (1, 2))


def _rope(x, cos, sin):
    w = x.shape[-1]
    reps = w // LANES
    if reps > 1:
        cos = jnp.concatenate([cos] * reps, axis=1)
        sin = jnp.concatenate([sin] * reps, axis=1)
    first_half = (_iota(x.shape, 1) % HEAD_DIM) < (HEAD_DIM // 2)
    swapped = jnp.where(first_half, pltpu.roll(x, w - HEAD_DIM // 2, 1), pltpu.roll(x, HEAD_DIM // 2, 1))
    return x * cos + swapped * sin


def _rms(x, g):
    ms = jnp.mean(x * x, axis=-1, keepdims=True)
    return x * lax.rsqrt(ms + NORM_EPS) * g


def _inproj0_kernel(for_prompt, x_ref, g_ref, w_ref, cos_ref, sin_ref, pa_ref, kf_ref, vf_ref, kif_ref, *outs):
    h = _rms(x_ref[...], g_ref[...]).astype(BF16)
    p = jnp.dot(h, w_ref[...], preferred_element_type=F32)
    cos = cos_ref[...]
    sin = sin_ref[...]
    o = A_COLS
    pa_ref[...] = p[:, :o]
    q = _rope(p[:, o:o + D_B], cos, sin) * (HEAD_DIM ** -0.5)
    k = _rope(p[:, o + D_B:o + 2 * D_B], cos, sin)
    v = p[:, o + 2 * D_B:o + 3 * D_B]
    qi = _rope(p[:, o + 3 * D_B:o + 4 * D_B], cos, sin)
    tail = p[:, o + 4 * D_B:]
    ki = _rope(tail, cos, sin)[:, :D_IDX]
    if for_prompt:
        qt_ref, k_ref, vt_ref, qit_ref, wit_ref, ki_ref = outs
        v_t = v.T
        kf_ref[...] = k.T
        vf_ref[...] = v_t
        kif_ref[...] = ki.T
        qt_ref[...] = (q * LOG2E).T.astype(BF16)
        k_ref[...] = k.astype(BF16)
        vt_ref[...] = v_t.astype(BF16)
        qit_ref[...] = qi.T.astype(BF16)
        wit_ref[...] = tail.T[D_IDX:D_IDX + H_IDX, :] * IDX_SCALE
        ki_ref[...] = ki.astype(BF16)
    else:
        q_ref, qi_ref, tail_ref = outs
        kf_ref[...] = k
        vf_ref[...] = v
        kif_ref[...] = ki
        q_ref[...] = q.astype(BF16)
        qi_ref[...] = qi.astype(BF16)
        tail_ref[...] = tail


def _inproj0(x, g, w_pad, cos, sin, tm, for_prompt):
    B, T, _ = x.shape
    tok = lambda width: pl.BlockSpec((None, tm, width), lambda b, t: (b, t, 0))
    tr = lambda rows: pl.BlockSpec((None, rows, tm), lambda b, t: (b, 0, t))
    full = lambda a: pl.BlockSpec(a.shape, lambda b, t: (0,) * a.ndim)
    tab = pl.BlockSpec((tm, LANES), lambda b, t: (t, 0))
    sds = lambda shape, dt: jax.ShapeDtypeStruct(shape, dt)
    if for_prompt:
        specs = [tok(A_COLS), tr(D_B), tr(D_B), tr(D_IDX), tr(D_B), tok(D_B), tr(D_B), tr(D_B), tr(H_IDX), tok(D_IDX)]
        shapes = [sds((B, T, A_COLS), F32), sds((B, D_B, T), F32), sds((B, D_B, T), F32), sds((B, D_IDX, T), F32),
                  sds((B, D_B, T), BF16), sds((B, T, D_B), BF16), sds((B, D_B, T), BF16), sds((B, D_B, T), BF16),
                  sds((B, H_IDX, T), F32), sds((B, T, D_IDX), BF16)]
    else:
        specs = [tok(A_COLS), tok(D_B), tok(D_B), tok(D_IDX), tok(D_B), tok(D_B), tok(LANES)]
        shapes = [sds((B, T, A_COLS), F32), sds((B, T, D_B), F32), sds((B, T, D_B), F32), sds((B, T, D_IDX), F32),
                  sds((B, T, D_B), BF16), sds((B, T, D_B), BF16), sds((B, T, LANES), F32)]
    return pl.pallas_call(
        functools.partial(_inproj0_kernel, for_prompt),
        grid=(B, T // tm),
        in_specs=[tok(D_MODEL), full(g), full(w_pad), tab, tab],
        out_specs=specs,
        out_shape=shapes,
        compiler_params=_cparams(("parallel", "arbitrary")),
        name="inproj0",
    )(x, g, w_pad, cos, sin)


def _seg_sum(x, bd):
    return _dot_split_lhs(x, bd)


def _rwkv_prep_kernel(seq_mode, p_ref, prev_ref, shift_ref, mu_ref, w0_ref, w2_ref, a0_ref, a2_ref, g2_ref,
                      kk_ref, ka_ref, rk_ref, bd_ref,
                      r_out, ld_out, k_out, v_out, kkn_out, ab_out, g_out, bonus_out):
    p = p_ref[...]
    if seq_mode:
        last = jnp.where(pl.program_id(1) == 0, shift_ref[...], prev_ref[7:8, :])
        prev = jnp.where(_iota(p.shape, 0) == 0, last, pltpu.roll(p, 1, 0))
    else:
        prev = prev_ref[...]
    xm = p + (prev - p) * mu_ref[...]
    r = xm[:, :D_A]
    k = xm[:, D_A:2 * D_A]
    v = xm[:, 2 * D_A:3 * D_A]
    wa = xm[:, 3 * D_A:3 * D_A + LANES]
    gl = xm[:, 3 * D_A + LANES:]
    z = -(w0_ref[...] + _dot(jnp.tanh(wa), w2_ref[...]))
    softplus = jnp.maximum(z, 0.0) + jnp.log(1.0 + jnp.exp(-jnp.abs(z)))
    ld_out[...] = -jnp.exp(-softplus - 0.5)
    a = jax.nn.sigmoid(a0_ref[...] + _dot(wa, a2_ref[...]))
    g_out[...] = _dot(jax.nn.sigmoid(gl), g2_ref[...])
    bd = bd_ref[...]
    kk = k * kk_ref[...]
    kkn = kk * lax.rsqrt(jnp.maximum(_seg_sum(kk * kk, bd), 1e-24))
    k2 = k * (1.0 + (a - 1.0) * ka_ref[...])
    r_out[...] = r
    k_out[...] = k2
    v_out[...] = v
    kkn_out[...] = kkn
    ab_out[...] = kkn * a
    bonus_out[...] = _seg_sum(r * k2 * rk_ref[...], bd) * v


def _block_diag_ones(n, seg=HEAD_DIM):
    i = np.arange(n)
    return jnp.asarray((i[:, None] // seg) == (i[None, :] // seg), BF16)


def _rwkv_prep(pa, shift_prev, prm, tm, seq_mode):
    B, T, _ = pa.shape
    nt = T // tm
    tok = lambda width: pl.BlockSpec((None, tm, width), lambda b, t: (b, t, 0))
    full = lambda a: pl.BlockSpec(a.shape, lambda b, t: (0,) * a.ndim)
    if seq_mode:
        prev_spec = pl.BlockSpec((None, 8, A_COLS), lambda b, t: (b, jnp.maximum(t * (tm // 8) - 1, 0), 0))
        prev_arr = pa
        shift_arr = shift_prev.reshape(B, 1, A_COLS)
        shift_spec = pl.BlockSpec((None, 1, A_COLS), lambda b, t: (b, 0, 0))
    else:
        prev_spec = tok(A_COLS)
        prev_arr = shift_prev.reshape(1, T, A_COLS)
        shift_arr = jnp.zeros((1, 1, A_COLS), F32)
        shift_spec = pl.BlockSpec((None, 1, A_COLS), lambda b, t: (0, 0, 0))
    params = [prm[n] for n in ("mu", "w0", "w2", "a0", "a2", "g2", "kk", "ka", "rk", "bd")]
    out = jax.ShapeDtypeStruct((B, T, D_A), F32)
    return pl.pallas_call(
        functools.partial(_rwkv_prep_kernel, seq_mode),
        grid=(B, nt),
        in_specs=[tok(A_COLS), prev_spec, shift_spec] + [full(a) for a in params],
        out_specs=[tok(D_A)] * 8,
        out_shape=[out] * 8,
        compiler_params=_cparams(("parallel", "arbitrary")),
        name="rwkv_prep",
    )(pa, prev_arr, shift_arr, *params)


RWKV_CHUNK = 64
RWKV_GROUP = 4
RWKV_W = RWKV_GROUP * HEAD_DIM


def _rwkv_chunk(r, ld, k, v, kkn, ab, h, tri, same_head, strict, incl, eye):
    each = lambda f, *xs: [f(*a) for a in zip(*xs)]
    w = RWKV_W

    def expand(x):
        return jnp.where(same_head, jnp.concatenate([x] * RWKV_GROUP, axis=0), 0.0).astype(BF16)

    cum = each(lambda x: _dot_split_rhs(tri, x), ld)
    cum_end = each(lambda c: c[RWKV_CHUNK - 1:RWKV_CHUNK, :], cum)
    a_t = each(lambda x, c, l: expand(-(x * jnp.exp(c - l))), kkn, cum, ld)
    r_t = each(lambda x, c: expand(x * jnp.exp(c)), r, cum)
    e_neg = each(lambda c: jnp.exp(-c), cum)
    b_t = each(lambda x, e: expand(x * e), ab, e_neg)
    k_t = each(lambda x, e: expand(x * e), k, e_neg)
    v_e = each(expand, v)
    gram = each(lambda a, rr, b, kk: _dot_nt(jnp.concatenate([a, rr], axis=0), jnp.concatenate([b, kk], axis=0)),
                a_t, r_t, b_t, k_t)
    l_ab = each(lambda g: jnp.where(strict, g[:w, :w], 0.0), gram)
    a_ak = each(lambda g: jnp.where(strict, g[:w, w:], 0.0), gram)
    a_r = each(lambda g: jnp.concatenate([jnp.where(incl, g[w:, :w], 0.0), jnp.where(incl, g[w:, w:], 0.0)], axis=1), gram)
    pinv = each(lambda l: jnp.where(eye, 1.0, 0.0) + l, l_ab)
    qpow = l_ab
    for _ in range(5):
        qpow = each(lambda q: _dot(q, q), qpow)
        pinv = each(lambda p, q: p + _dot(q, p), pinv, qpow)
    x0 = each(_dot, a_ak, v_e)
    wu = each(lambda p, a, x: _dot(p, jnp.concatenate([a, x.astype(BF16)], axis=1)), pinv, a_t, x0)
    u_e = each(lambda m, hh: _dot(m[:, :w], hh) + m[:, w:], wu, h)
    uv = each(lambda u, vv: jnp.concatenate([u.astype(BF16), vv], axis=0), u_e, v_e)
    y_e = each(lambda rr, hh, a, x: _dot(rr, hh) + _dot(a, x), r_t, h, a_r, uv)
    y = each(lambda ye: sum(ye[i * RWKV_CHUNK:(i + 1) * RWKV_CHUNK] for i in range(1, RWKV_GROUP)) + ye[:RWKV_CHUNK], y_e)
    e_rem = each(lambda ce, c: jnp.exp(ce - c), cum_end, cum)
    bk_t = each(lambda b, kk, e: jnp.concatenate(
        [jnp.where(same_head, jnp.concatenate([b * e] * RWKV_GROUP, axis=0), 0.0).T,
         jnp.where(same_head, jnp.concatenate([kk * e] * RWKV_GROUP, axis=0), 0.0).T], axis=1), ab, k, e_rem)
    g_col = each(lambda ce: jnp.exp(jnp.broadcast_to(ce, (8, w))).T[:, :1], cum_end)
    h_new = each(lambda hh, g, b, x: hh * g + _dot(b, x), h, g_col, bk_t, uv)
    return y, h_new


def _rwkv_scan_kernel(n_chunks, r_ref, ld_ref, k_ref, v_ref, kkn_ref, ab_ref, g_ref, bonus_ref, lnw_ref, lnb_ref,
                      y_ref, ht_ref, h_scr):
    @pl.when(pl.program_id(1) == 0)
    def _():
        h_scr[...] = jnp.zeros_like(h_scr)

    w = RWKV_W
    c = RWKV_CHUNK
    row = _iota((w, w), 0)
    col = _iota((w, w), 1)
    same_head = (row // c) == (col // HEAD_DIM)
    strict = (row % c) > (col % c)
    incl = (row % c) >= (col % c)
    eye = row == col
    tri = jnp.where(_iota((c, c), 0) >= _iota((c, c), 1), 1.0, 0.0).astype(BF16)
    seg_avg = jnp.where((row // HEAD_DIM) == (col // HEAD_DIM), 1.0 / HEAD_DIM, 0.0).astype(BF16)
    n_seq = r_ref.shape[0]
    chains = [(s, slice(g * w, (g + 1) * w)) for s in range(n_seq) for g in range(D_A // w)]
    hs = [h_scr[j] for j in range(len(chains))]
    for i in range(n_chunks):
        sl = slice(i * c, (i + 1) * c)
        pick = lambda ref: [ref[s, sl, ln] for s, ln in chains]
        ys, hs = _rwkv_chunk(pick(r_ref), pick(ld_ref), pick(k_ref), pick(v_ref), pick(kkn_ref), pick(ab_ref), hs,
                             tri, same_head, strict, incl, eye)
        for y, (s, ln) in zip(ys, chains):
            mean = _dot_split_lhs(y, seg_avg)
            yc = y - mean
            var = _dot_split_lhs(yc * yc, seg_avg)
            yn = yc * lax.rsqrt(var + GN_EPS) * lnw_ref[:, ln] + lnb_ref[:, ln]
            y_ref[s, sl, ln] = ((yn + bonus_ref[s, sl, ln]) * g_ref[s, sl, ln]).astype(BF16)
    for j in range(len(chains)):
        h_scr[j] = hs[j]

    @pl.when(pl.program_id(1) == pl.num_programs(1) - 1)
    def _():
        for j in range(len(chains)):
            ht_ref[j] = hs[j].T


def _rwkv_scan(r, ld, k, v, kkn, ab, g, bonus, lnw, lnb, tb):
    B, T, _ = r.shape
    ng = D_A // RWKV_W
    nb = 2 if B % 2 == 0 else 1
    blk = pl.BlockSpec((nb, tb, D_A), lambda b, t: (b, t, 0))
    par = pl.BlockSpec((1, D_A), lambda b, t: (0, 0))
    y, ht = pl.pallas_call(
        functools.partial(_rwkv_scan_kernel, tb // RWKV_CHUNK),
        grid=(B // nb, T // tb),
        in_specs=[blk] * 8 + [par, par],
        out_specs=[blk, pl.BlockSpec((nb * ng, RWKV_W, RWKV_W), lambda b, t: (b, 0, 0))],
        out_shape=[jax.ShapeDtypeStruct((B, T, D_A), BF16), jax.ShapeDtypeStruct((B * ng, RWKV_W, RWKV_W), F32)],
        scratch_shapes=[pltpu.VMEM((nb * ng, RWKV_W, RWKV_W), F32)],
        compiler_params=_cparams(("parallel", "arbitrary")),
        name="rwkv_scan",
    )(r, ld, k, v, kkn, ab, g, bonus, lnw, lnb)
    ht = ht.reshape(B, ng, RWKV_GROUP, HEAD_DIM, RWKV_GROUP, HEAD_DIM)
    idx = jnp.arange(RWKV_GROUP)
    wkv = ht[:, :, idx, :, idx, :]
    return y, jnp.moveaxis(wkv, 0, 2).reshape(B, H_A, HEAD_DIM, HEAD_DIM)


def _rwkv_step_kernel(s_ref, r_ref, ld_ref, k_ref, v_ref, kkn_ref, ab_ref, g_ref, bonus_ref, lnw_ref, lnb_ref,
                      y_ref, s_out):
    rows = H_A * HEAD_DIM
    pad = 16
    rep = jnp.where((_iota((rows, pad), 0) // HEAD_DIM) == _iota((rows, pad), 1), 1.0, 0.0).astype(BF16)
    rep_t = jnp.where((_iota((pad, rows), 1) // HEAD_DIM) == _iota((pad, rows), 0), 1.0, 0.0).astype(BF16)
    zeros8 = jnp.zeros((pad - H_A, HEAD_DIM), F32)
    spread = lambda x8: _dot_split_rhs(rep, jnp.concatenate([x8, zeros8], axis=0))
    diag = (_iota((rows, HEAD_DIM), 0) % HEAD_DIM) == _iota((rows, HEAD_DIM), 1)
    s = s_ref[...].reshape(rows, HEAD_DIM)
    a_rep = -spread(kkn_ref[...])
    sa = jnp.sum(s * a_rep, axis=1, keepdims=True)
    v_col = jnp.sum(jnp.where(diag, spread(v_ref[...]), 0.0), axis=1, keepdims=True)
    s_new = s * jnp.exp(spread(ld_ref[...])) + sa * spread(ab_ref[...]) + v_col * spread(k_ref[...])
    s_out[...] = s_new.reshape(H_A, HEAD_DIM, HEAD_DIM)
    y_col = jnp.sum(s_new * spread(r_ref[...]), axis=1, keepdims=True)
    y = _dot_split_rhs(rep_t, jnp.where(diag, y_col, 0.0))[:H_A]
    mean = jnp.mean(y, axis=1, keepdims=True)
    yc = y - mean
    var = jnp.mean(yc * yc, axis=1, keepdims=True)
    yn = yc * lax.rsqrt(var + GN_EPS) * lnw_ref[...] + lnb_ref[...]
    y_ref[...] = ((yn + bonus_ref[...]) * g_ref[...]).astype(BF16)


def _rwkv_step(state, r, ld, k, v, kkn, ab, g, bonus, lnw, lnb):
    DB = state.shape[0]
    heads = lambda a: a.reshape(DB, H_A, HEAD_DIM)
    vec = pl.BlockSpec((None, H_A, HEAD_DIM), lambda b: (b, 0, 0))
    par = pl.BlockSpec((H_A, HEAD_DIM), lambda b: (0, 0))
    st = pl.BlockSpec((None, H_A, HEAD_DIM, HEAD_DIM), lambda b: (b, 0, 0, 0))
    y, s_new = pl.pallas_call(
        _rwkv_step_kernel,
        grid=(DB,),
        in_specs=[st] + [vec] * 8 + [par, par],
        out_specs=[vec, st],
        out_shape=[jax.ShapeDtypeStruct((DB, H_A, HEAD_DIM), BF16), jax.ShapeDtypeStruct(state.shape, F32)],
        compiler_params=_cparams(("parallel",)),
        name="rwkv_step",
    )(state, *(heads(a) for a in (r, ld, k, v, kkn, ab, g, bonus)), lnw.reshape(H_A, HEAD_DIM), lnb.reshape(H_A, HEAD_DIM))
    return y.reshape(DB, D_A), s_new


DSA_PG = 8
DSA_PG_IDX = 16


def _dsa_sample_select_kernel(topk, n_groups, idx_bits, pt_ref, qi_ref, w_ref, kin_ref, *rest):
    n_pg = len(rest) - 3
    pages = rest[:n_pg]
    bias_ref, bias_new_ref, key_scr = rest[n_pg:]
    g = pl.program_id(1)
    qi8 = qi_ref[...]
    w8 = w_ref[...]
    allp = jnp.concatenate([p[...] for p in pages], axis=1)
    res = _dot(qi8, allp)
    score = jnp.sum(jnp.maximum(res, 0.0) * w8, axis=0, keepdims=True)
    key = _order_key(score)
    for i in range(n_pg):
        key_scr[pl.ds(g * n_pg + i, 1), :] = key[:, i * PAGE_SIZE:(i + 1) * PAGE_SIZE]

    @pl.when(g == n_groups - 1)
    def _():
        n_pages = n_groups * n_pg
        s_new = jnp.sum(jnp.maximum(_dot_nt(qi8, kin_ref[...]), 0.0) * w8, axis=0, keepdims=True)[:, :1]
        key_new = _order_key(s_new)
        keys = key_scr[...]
        idx = _iota(keys.shape, 0) * PAGE_SIZE + _iota(keys.shape, 1)
        idx_new = n_pages * PAGE_SIZE
        total = lambda m: jnp.sum(jnp.sum(m, axis=1, keepdims=True), axis=0, keepdims=True)
        count_ge = lambda c: total(jnp.where(keys >= c, 1.0, 0.0)) + jnp.where(key_new >= c, 1.0, 0.0)
        thr = _kth_largest_key(count_ge, topk, (1, 1))
        n_gt = total(jnp.where(keys > thr, 1.0, 0.0)) + jnp.where(key_new > thr, 1.0, 0.0)
        need = float(topk) - n_gt

        def bit_body(i, j):
            cand = j + lax.shift_left(jnp.int32(1), idx_bits - 1 - i)
            ties = (total(jnp.where(keys == thr, jnp.where(idx <= cand, 1.0, 0.0), 0.0))
                    + jnp.where(key_new == thr, jnp.where(idx_new <= cand, 1.0, 0.0), 0.0))
            return jnp.where(ties < need, cand, j)

        idx_thr = lax.fori_loop(0, idx_bits, bit_body, jnp.full((1, 1), -1, I32)) + 1
        sel = lambda kk, ii: jnp.where(kk > thr, 0.0, jnp.where(kk == thr, jnp.where(ii <= idx_thr, 0.0, NEG_BIG), NEG_BIG))
        bias_ref[...] = sel(keys, idx)
        bias_new_ref[...] = jnp.where(_iota(bias_new_ref.shape, 1) == 0, sel(key_new, idx_new), NEG_BIG)


def _dsa_sample_select(page_table, qi, wi, ki_new, kidx_t):
    DB, n_pages = page_table.shape
    n_pg = DSA_PG_IDX if n_pages % DSA_PG_IDX == 0 else DSA_PG
    n_groups = n_pages // n_pg
    L = n_pages * PAGE_SIZE + 1
    topk = min(IDX_TOPK_MAX, L // 4)
    idx_bits = int(np.ceil(np.log2(L))) + 1
    per_seq = lambda a: pl.BlockSpec((None,) + a.shape[1:], lambda b, g, pt: (b,) + (0,) * (a.ndim - 1))
    page = lambda i: pl.BlockSpec((None, D_IDX, PAGE_SIZE), lambda b, g, pt: (pt[b, g * n_pg + i], 0, 0))
    grid_spec = pltpu.PrefetchScalarGridSpec(
        num_scalar_prefetch=1,
        grid=(DB, n_groups),
        in_specs=[per_seq(qi), per_seq(wi), per_seq(ki_new)] + [page(i) for i in range(n_pg)],
        out_specs=[pl.BlockSpec((None, n_pages, PAGE_SIZE), lambda b, g, pt: (b, 0, 0)),
                   pl.BlockSpec((None, 8, LANES), lambda b, g, pt: (b, 0, 0))],
        scratch_shapes=[pltpu.VMEM((n_pages, PAGE_SIZE), I32)],
    )
    return pl.pallas_call(
        functools.partial(_dsa_sample_select_kernel, topk, n_groups, idx_bits),
        grid_spec=grid_spec,
        out_shape=[jax.ShapeDtypeStruct((DB, n_pages, PAGE_SIZE), F32), jax.ShapeDtypeStruct((DB, 8, LANES), F32)],
        compiler_params=_cparams(("parallel", "arbitrary")),
        name="dsa_sample_select",
    )(page_table, qi, wi, ki_new, *([kidx_t] * n_pg))


def _keys_on_lanes(cache):
    return jnp.moveaxis(cache, -3, -1)


def _on_lanes(x):
    return jnp.broadcast_to(x.astype(F32)[..., None], x.shape + (LANES,))


def _lane0(x):
    return jnp.pad(x.astype(F32)[..., None], ((0, 0), (0, 0), (0, 0), (0, LANES - 1)))


def _lane_attend_scratch(n_heads, width):
    return [pltpu.VMEM((n_heads, LANES), F32), pltpu.VMEM((n_heads, LANES), F32),
            pltpu.VMEM((n_heads, HEAD_DIM, LANES), F32), pltpu.VMEM((n_heads, width), F32),
            pltpu.VMEM((n_heads, LANES), F32)]


def _lane_attend_init(m_scr, l_scr, acc_scr, p_scr, a_scr):
    m_scr[...] = jnp.full(m_scr.shape, NEG_BIG, F32)
    l_scr[...] = jnp.zeros_like(l_scr)
    acc_scr[...] = jnp.zeros_like(acc_scr)


def _lane_attend(kt_refs, vt_refs, qb_ref, bias, m_scr, l_scr, acc_scr, p_scr, a_scr):
    n_heads = kt_refs[0].shape[0]
    pieces = [(ref_i, g) for ref_i, ref in enumerate(kt_refs) for g in range(ref.shape[2] // LANES)]
    groups = len(pieces)
    width = groups * LANES
    for h in range(n_heads):
        q = qb_ref[h]
        for j, (ref_i, g) in enumerate(pieces):
            p_scr[h:h + 1, j * LANES:(j + 1) * LANES] = jnp.sum(
                kt_refs[ref_i][h, :, g * LANES:(g + 1) * LANES] * q, axis=0, keepdims=True)
    s = p_scr[:, :width] + bias
    m_old = m_scr[...]
    m_new = jnp.maximum(m_old, jnp.max(s, axis=1, keepdims=True))
    alpha = jnp.exp(m_old - m_new)
    p = jnp.exp(s - (jnp.concatenate([m_new] * groups, axis=1) if groups > 1 else m_new))
    l_scr[...] = alpha * l_scr[...] + jnp.sum(p, axis=1, keepdims=True)
    m_scr[...] = m_new
    p_scr[:, :width] = p
    a_scr[...] = alpha
    for h in range(n_heads):
        acc = acc_scr[h] * a_scr[h:h + 1, :]
        for j, (ref_i, g) in enumerate(pieces):
            acc = acc + p_scr[h:h + 1, j * LANES:(j + 1) * LANES] * vt_refs[ref_i][h, :, g * LANES:(g + 1) * LANES]
        acc_scr[h] = acc


def _lane_attend_finish(o_ref, m_scr, l_scr, acc_scr, p_scr, a_scr):
    a_scr[...] = 1.0 / l_scr[...]
    for h in range(o_ref.shape[0]):
        o_ref[h] = jnp.sum(acc_scr[h] * a_scr[h:h + 1, :], axis=1, keepdims=True)


def _dsa_sample_attend_kernel(n_groups, pt_ref, qb_ref, kn_ref, vn_ref, bias_ref, bias_new_ref, *rest):
    kpages = rest[:DSA_PG]
    vpages = rest[DSA_PG:2 * DSA_PG]
    o_ref = rest[2 * DSA_PG]
    state = rest[2 * DSA_PG + 1:]
    g = pl.program_id(1)

    @pl.when(g == 0)
    def _():
        _lane_attend_init(*state)

    bias = jnp.concatenate([bias_ref[i:i + 1, :] for i in range(DSA_PG)], axis=1)
    _lane_attend(list(kpages), list(vpages), qb_ref, bias, *state)

    @pl.when(g == n_groups - 1)
    def _():
        _lane_attend([kn_ref], [vn_ref], qb_ref, bias_new_ref[0:1, :], *state)
        _lane_attend_finish(o_ref, *state)


def _dsa_sample_attend(page_table, q, k_new, v_new, bias, bias_new, cache_k, cache_v):
    DB, n_pages = page_table.shape
    n_groups = n_pages // DSA_PG
    per_seq = pl.BlockSpec((None, H_B, HEAD_DIM, LANES), lambda b, g, pt: (b, 0, 0, 0))
    page = lambda i: pl.BlockSpec((None, H_B, HEAD_DIM, PAGE_SIZE), lambda b, g, pt: (pt[b, g * DSA_PG + i], 0, 0, 0))
    grid_spec = pltpu.PrefetchScalarGridSpec(
        num_scalar_prefetch=1,
        grid=(DB, n_groups),
        in_specs=[per_seq, per_seq, per_seq,
                  pl.BlockSpec((None, DSA_PG, PAGE_SIZE), lambda b, g, pt: (b, g, 0)),
                  pl.BlockSpec((None, 8, LANES), lambda b, g, pt: (b, 0, 0))] + [page(i) for i in range(DSA_PG)] * 2,
        out_specs=pl.BlockSpec((None, H_B, HEAD_DIM, 1), lambda b, g, pt: (b, 0, 0, 0)),
        scratch_shapes=_lane_attend_scratch(H_B, DSA_PG * PAGE_SIZE),
    )
    ck, cv = _keys_on_lanes(cache_k), _keys_on_lanes(cache_v)
    return pl.pallas_call(
        functools.partial(_dsa_sample_attend_kernel, n_groups),
        grid_spec=grid_spec,
        out_shape=jax.ShapeDtypeStruct((DB, H_B, HEAD_DIM, 1), F32),
        compiler_params=_cparams(("parallel", "arbitrary")),
        name="dsa_sample_attend",
    )(page_table, _on_lanes(q), _lane0(k_new), _lane0(v_new), bias, bias_new, *([ck] * DSA_PG), *([cv] * DSA_PG))


DIL_CHUNK = 512


def _dilated_sample_kernel(w_len, qb_ref, kn_ref, vn_ref, kc_ref, vc_ref, o_ref, *state):
    c = pl.program_id(1)

    @pl.when(c == 0)
    def _():
        _lane_attend_init(*state)

    width = kc_ref.shape[2]
    dist = w_len - (c * width + _iota((1, width), 1))
    count = jnp.zeros((1, width), F32)
    for window, dil in C_PATTERNS:
        count = count + jnp.where(dist <= window, jnp.where(dist % dil == 0, 1.0, 0.0), 0.0)
    bias = jnp.where(count > 0.0, jnp.log(jnp.maximum(count, 1.0)), NEG_BIG)
    _lane_attend([kc_ref], [vc_ref], qb_ref, bias, *state)

    @pl.when(c == pl.num_programs(1) - 1)
    def _():
        bias_new = jnp.where(_iota((1, LANES), 1) == 0, float(np.log(len(C_PATTERNS))), NEG_BIG)
        _lane_attend([kn_ref], [vn_ref], qb_ref, bias_new, *state)
        _lane_attend_finish(o_ref, *state)


def _dilated_sample(q, k_new, v_new, cache_k, cache_v):
    DB, w_len = cache_k.shape[:2]
    width = min(DIL_CHUNK, w_len)
    per_seq = pl.BlockSpec((None, H_C, HEAD_DIM, LANES), lambda b, c: (b, 0, 0, 0))
    chunk = pl.BlockSpec((None, H_C, HEAD_DIM, width), lambda b, c: (b, 0, 0, c))
    return pl.pallas_call(
        functools.partial(_dilated_sample_kernel, w_len),
        grid=(DB, w_len // width),
        in_specs=[per_seq, per_seq, per_seq, chunk, chunk],
        out_specs=pl.BlockSpec((None, H_C, HEAD_DIM, 1), lambda b, c: (b, 0, 0, 0)),
        out_shape=jax.ShapeDtypeStruct((DB, H_C, HEAD_DIM, 1), F32),
        scratch_shapes=_lane_attend_scratch(H_C, width),
        compiler_params=_cparams(("parallel", "arbitrary")),
        name="dilated_sample",
    )(_on_lanes(q), _lane0(k_new), _lane0(v_new), _keys_on_lanes(cache_k), _keys_on_lanes(cache_v))


def _silu(x):
    return x * jax.nn.sigmoid(x)


def _outproj_ffn_kernel(x_ref, ya_ref, yb_ref, wo_ref, g_ref, wg_ref, wu_ref, wd_ref, o_ref, h_scr, hn_scr, acc_scr):
    j = pl.program_id(1)

    @pl.when(j == 0)
    def _():
        y = jnp.concatenate([ya_ref[...], yb_ref[...]], axis=1)
        h = x_ref[...] + jnp.dot(y, wo_ref[...], preferred_element_type=F32)
        h_scr[...] = h
        hn_scr[...] = _rms(h, g_ref[...]).astype(BF16)
        acc_scr[...] = jnp.zeros_like(acc_scr)

    hn = hn_scr[...]
    act = _silu(jnp.dot(hn, wg_ref[...], preferred_element_type=F32)) * jnp.dot(hn, wu_ref[...], preferred_element_type=F32)
    acc_scr[...] += jnp.dot(act.astype(BF16), wd_ref[...], preferred_element_type=F32)

    @pl.when(j == pl.num_programs(1) - 1)
    def _():
        o_ref[...] = h_scr[...] + acc_scr[...]


def _outproj_ffn(x, ya, yb, wo, g, wg, wu, wd, tm, tf):
    N = x.shape[0]
    nf = D_FF // tf
    tok = lambda width: pl.BlockSpec((tm, width), lambda i, j: (i, 0))
    full = lambda a: pl.BlockSpec(a.shape, lambda i, j: (0,) * a.ndim)
    return pl.pallas_call(
        _outproj_ffn_kernel,
        grid=(N // tm, nf),
        in_specs=[tok(D_MODEL), tok(D_A), tok(D_B), full(wo), full(g),
                  pl.BlockSpec((D_MODEL, tf), lambda i, j: (0, j)), pl.BlockSpec((D_MODEL, tf), lambda i, j: (0, j)),
                  pl.BlockSpec((tf, D_MODEL), lambda i, j: (j, 0))],
        out_specs=tok(D_MODEL),
        out_shape=jax.ShapeDtypeStruct((N, D_MODEL), F32),
        scratch_shapes=[pltpu.VMEM((tm, D_MODEL), F32), pltpu.VMEM((tm, D_MODEL), BF16), pltpu.VMEM((tm, D_MODEL), F32)],
        compiler_params=_cparams(("parallel", "arbitrary")),
        name="outproj_ffn",
    )(x, ya, yb, wo, g, wg, wu, wd)


def _inproj1_kernel(keep_tiles, x_ref, g_ref, w_ref, cos_ref, sin_ref, q_ref, k_ref, v_ref, kf_ref, vf_ref, *streams):
    h = _rms(x_ref[...], g_ref[...]).astype(BF16)
    p = jnp.dot(h, w_ref[...], preferred_element_type=F32)
    cos = cos_ref[...]
    sin = sin_ref[...]
    q = _rope(p[:, :D_C], cos, sin) * (HEAD_DIM ** -0.5 * (1.0 if keep_tiles is None else LOG2E))
    k = _rope(p[:, D_C:2 * D_C], cos, sin)
    v = p[:, 2 * D_C:]
    q_ref[...] = q.astype(BF16)
    k_ref[...] = k.astype(BF16)
    v_ref[...] = v.astype(BF16)
    if keep_tiles is None:
        kf_ref[...] = k
        vf_ref[...] = v
    else:
        scr = streams[-1]
        n_groups, tm, _ = scr.shape
        n_dil = len(C_PATTERNS) - 1
        for which, val in enumerate((q, k, v)):
            for g in range(n_groups):
                scr[g] = val[:, g * LANES:(g + 1) * LANES]
            for di, (_, dil) in enumerate(C_PATTERNS[1:]):
                out = streams[which * n_dil + di]
                for r in range(dil):
                    for g in range(n_groups):
                        out[:, r * D_C + g * LANES:r * D_C + (g + 1) * LANES] = (
                            scr[g, pl.ds(r, tm // dil, stride=dil), :].astype(BF16))

        @pl.when(pl.program_id(1) >= pl.num_programs(1) - keep_tiles)
        def _():
            kf_ref[...] = k.T
            vf_ref[...] = v.T


def _inproj1(x, g, w, cos, sin, tm, keep=None):
    B, T, _ = x.shape
    tok = pl.BlockSpec((None, tm, D_C), lambda b, t: (b, t, 0))
    full = lambda a: pl.BlockSpec(a.shape, lambda b, t: (0,) * a.ndim)
    tab = pl.BlockSpec((tm, LANES), lambda b, t: (t, 0))
    bf = jax.ShapeDtypeStruct((B, T, D_C), BF16)
    extra_specs, extra_shapes, scratch = [], [], []
    if keep is None:
        keep_tiles, f_spec, ff = None, tok, jax.ShapeDtypeStruct((B, T, D_C), F32)
    else:
        keep_tiles = keep // tm
        first = T // tm - keep_tiles
        f_spec = pl.BlockSpec((None, D_C, tm), lambda b, t: (b, 0, jnp.maximum(t - first, 0)))
        ff = jax.ShapeDtypeStruct((B, D_C, keep), F32)
        for _ in range(3):
            for _, dil in C_PATTERNS[1:]:
                extra_specs.append(pl.BlockSpec((None, tm // dil, dil * D_C), lambda b, t: (b, t, 0)))
                extra_shapes.append(jax.ShapeDtypeStruct((B, T // dil, dil * D_C), BF16))
        scratch = [pltpu.VMEM((D_C // LANES, tm, LANES), F32)]
    return pl.pallas_call(
        functools.partial(_inproj1_kernel, keep_tiles),
        grid=(B, T // tm),
        in_specs=[tok, full(g), full(w), tab, tab],
        out_specs=[tok, tok, tok, f_spec, f_spec] + extra_specs,
        out_shape=[bf, bf, bf, ff, ff] + extra_shapes,
        scratch_shapes=scratch,
        compiler_params=_cparams(("parallel", "arbitrary")),
        name="inproj1",
    )(x, g, w, cos, sin)


def _dilated_kernel(lookback, q_ref, kc_ref, kp_ref, vc_ref, vp_ref, o_ref, lse_ref, s_scr):
    QB = C_BLOCK
    c = pl.program_id(2)
    lane128 = _iota((QB, LANES), 1)
    upper = (lane128 // HEAD_DIM) == 1
    upper_v = (_iota((2 * QB, LANES), 1) // HEAD_DIM) == 1
    qi = _iota((QB, 2 * QB), 0)
    kj = _iota((QB, 2 * QB), 1)
    dist = qi + QB - kj
    ok = (dist >= 0) & (dist <= lookback) & ((kj >= QB) | (c > 0))
    bias = jnp.where(ok, 0.0, NEG_BIG)
    q = q_ref[...]
    lse_blk = jnp.zeros((QB, LANES), F32)
    outs = []

    def scores_into(slot, j):
        sl = slice(j * LANES, (j + 1) * LANES)
        pair = q[:, sl]
        q2 = jnp.concatenate([jnp.where(upper, jnp.zeros_like(pair), pair),
                              jnp.where(upper, pair, jnp.zeros_like(pair))], axis=0)
        s_scr[slot] = _dot_nt(q2, jnp.concatenate([kp_ref[:, sl], kc_ref[:, sl]], axis=0))

    scores_into(0, 0)
    for j in range(H_C // 2):
        if j + 1 < H_C // 2:
            scores_into((j + 1) % 2, j + 1)
        sl = slice(j * LANES, (j + 1) * LANES)
        v2 = jnp.concatenate([vp_ref[:, sl], vc_ref[:, sl]], axis=0)
        vv = jnp.concatenate([jnp.where(upper_v, jnp.zeros_like(v2), v2),
                              jnp.where(upper_v, v2, jnp.zeros_like(v2))], axis=0)
        ps, ls = [], []
        for u in range(2):
            s = s_scr[j % 2, u * QB:(u + 1) * QB, :] + bias
            m = jnp.max(s, axis=1, keepdims=True)
            p = jnp.exp2(s - m)
            l = jnp.sum(p, axis=1, keepdims=True)
            ps.append(p.astype(BF16))
            ls.append(l)
            lse_blk = lse_blk + jnp.where(lane128 == 2 * j + u, m + jnp.log2(l), 0.0)
        pv = jnp.dot(jnp.concatenate(ps, axis=1), vv, preferred_element_type=F32)
        outs.append(pv / jnp.where(upper, ls[1], ls[0]))
    o_ref[...] = jnp.concatenate(outs, axis=1).astype(BF16)
    lse_ref[...] = lse_blk


def _dilated_branch(q, k, v, window, dil):
    B, n, _ = q.shape
    view = lambda a: a
    cur = pl.BlockSpec((None, C_BLOCK, D_C), lambda b, r, c: (b, c, r))
    prev = pl.BlockSpec((None, C_BLOCK, D_C), lambda b, r, c: (b, jnp.maximum(c - 1, 0), r))
    o, lse = pl.pallas_call(
        functools.partial(_dilated_kernel, window // dil),
        grid=(B, dil, n // C_BLOCK),
        in_specs=[cur, cur, prev, cur, prev],
        out_specs=[cur, pl.BlockSpec((None, C_BLOCK, LANES), lambda b, r, c: (b, c, r))],
        out_shape=[jax.ShapeDtypeStruct((B, n, dil * D_C), BF16), jax.ShapeDtypeStruct((B, n, dil * LANES), F32)],
        scratch_shapes=[pltpu.VMEM((2, 2 * C_BLOCK, 2 * C_BLOCK), F32)],
        compiler_params=_cparams(("parallel", "parallel", "arbitrary")),
        name="dilated_w%d_d%d" % (window, dil),
    )(view(q), view(k), view(k), view(v), view(v))
    return o, lse


def _merge_outproj_kernel(x_ref, o1_ref, o2_ref, o3_ref, l1_ref, l2_ref, l3_ref, ex_ref, wo_ref, out_ref,
                          o_scr, l_scr):
    tm = x_ref.shape[0]

    def in_token_order(o_ref, l_ref, slot, dil):
        if dil == 1:
            return o_ref[...].astype(F32), l_ref[...]
        n_groups = D_C // LANES
        for r in range(dil):
            rows = pl.ds(r, tm // dil, stride=dil)
            for g in range(n_groups):
                o_scr[slot, g, rows, :] = o_ref[:, r * D_C + g * LANES:r * D_C + (g + 1) * LANES].astype(F32)
            l_scr[slot, rows, :] = l_ref[:, r * LANES:(r + 1) * LANES]
        return jnp.concatenate([o_scr[slot, g] for g in range(n_groups)], axis=1), l_scr[slot]

    branches = [in_token_order(o_ref, l_ref, i - 1, dil) for i, (o_ref, l_ref, (_, dil)) in
                enumerate(zip((o1_ref, o2_ref, o3_ref), (l1_ref, l2_ref, l3_ref), C_PATTERNS))]
    lses = [l for _, l in branches]
    m = jnp.maximum(jnp.maximum(lses[0], lses[1]), lses[2])
    es = [jnp.exp2(l - m) for l in lses]
    inv = 1.0 / (es[0] + es[1] + es[2])
    y = jnp.zeros(x_ref.shape, F32)
    for e, (o, _) in zip(es, branches):
        y = y + _dot_split_lhs(e * inv, ex_ref[...]) * o
    out_ref[...] = x_ref[...] + jnp.dot(y.astype(BF16), wo_ref[...], preferred_element_type=F32)


def _head_expand_matrix():
    e = np.zeros((LANES, D_C), np.float32)
    for h in range(H_C):
        e[h, h * HEAD_DIM:(h + 1) * HEAD_DIM] = 1.0
    return jnp.asarray(e, BF16)


def _merge_outproj(x, os_, lses, wo, tm):
    N = x.shape[0]
    tok = lambda width: pl.BlockSpec((tm, width), lambda i: (i, 0))
    full = lambda a: pl.BlockSpec(a.shape, lambda i: (0,) * a.ndim)
    stream = lambda width, dil: pl.BlockSpec((tm // dil, dil * width), lambda i: (i, 0))
    ex = _head_expand_matrix()
    dils = [dil for _, dil in C_PATTERNS]
    return pl.pallas_call(
        _merge_outproj_kernel,
        grid=(N // tm,),
        in_specs=([tok(D_MODEL)] + [stream(D_C, d) for d in dils] + [stream(LANES, d) for d in dils]
                  + [full(ex), full(wo)]),
        out_specs=tok(D_MODEL),
        out_shape=jax.ShapeDtypeStruct((N, D_MODEL), F32),
        scratch_shapes=[pltpu.VMEM((2, D_C // LANES, tm, LANES), F32), pltpu.VMEM((2, tm, LANES), F32)],
        compiler_params=_cparams(("parallel",)),
        name="merge_outproj",
    )(x, *os_, *lses, ex, wo)


def _top2_gates(logits):
    lane = _iota(logits.shape, 1)
    m1 = jnp.max(logits, axis=1, keepdims=True)
    i1 = jnp.min(jnp.where(logits == m1, lane, LANES), axis=1, keepdims=True)
    rest = jnp.where(lane == i1, -jnp.inf, logits)
    m2 = jnp.max(rest, axis=1, keepdims=True)
    i2 = jnp.min(jnp.where(rest == m2, lane, LANES), axis=1, keepdims=True)
    e2 = jnp.exp(m2 - m1)
    g1 = 1.0 / (1.0 + e2)
    return jnp.where(lane == i1, g1, 0.0) + jnp.where(lane == i2, e2 * g1, 0.0)


def _moe_dense_kernel(x_ref, g_ref, rw_ref, rb_ref, wg_ref, wu_ref, wd_ref, gf_ref, o_ref, hn_scr, gate_scr, acc_scr):
    e = pl.program_id(1)
    j = pl.program_id(2)

    @pl.when((e == 0) & (j == 0))
    def _():
        hn = _rms(x_ref[...], g_ref[...])
        hn_scr[...] = hn.astype(BF16)
        logits = jnp.dot(hn.astype(BF16), rw_ref[...], preferred_element_type=F32) + rb_ref[...]
        gate_scr[...] = _top2_gates(logits)
        acc_scr[...] = jnp.zeros_like(acc_scr)

    hn = hn_scr[...]
    gate = gate_scr[...]
    gate_e = jnp.sum(jnp.where(_iota(gate.shape, 1) == e, gate, 0.0), axis=1, keepdims=True)
    act = _silu(jnp.dot(hn, wg_ref[...], preferred_element_type=F32)) * jnp.dot(hn, wu_ref[...], preferred_element_type=F32)
    acc_scr[...] += gate_e * jnp.dot(act.astype(BF16), wd_ref[...], preferred_element_type=F32)

    @pl.when((e == pl.num_programs(1) - 1) & (j == pl.num_programs(2) - 1))
    def _():
        o_ref[...] = _rms(x_ref[...] + acc_scr[...], gf_ref[...])


def _moe_dense(x, g, rw, rb, wg, wu, wd, gf, tm, tf):
    N = x.shape[0]
    tok = pl.BlockSpec((tm, D_MODEL), lambda i, e, j: (i, 0))
    full = lambda a: pl.BlockSpec(a.shape, lambda i, e, j: (0,) * a.ndim)
    return pl.pallas_call(
        _moe_dense_kernel,
        grid=(N // tm, N_EXPERTS, D_FF_EXPERT // tf),
        in_specs=[tok, full(g), full(rw), full(rb),
                  pl.BlockSpec((None, D_MODEL, tf), lambda i, e, j: (e, 0, j)),
                  pl.BlockSpec((None, D_MODEL, tf), lambda i, e, j: (e, 0, j)),
                  pl.BlockSpec((None, tf, D_MODEL), lambda i, e, j: (e, j, 0)), full(gf)],
        out_specs=tok,
        out_shape=jax.ShapeDtypeStruct((N, D_MODEL), F32),
        scratch_shapes=[pltpu.VMEM((tm, D_MODEL), BF16), pltpu.VMEM((tm, LANES), F32), pltpu.VMEM((tm, D_MODEL), F32)],
        compiler_params=_cparams(("parallel", "arbitrary", "arbitrary")),
        name="moe_dense",
    )(x, g, rw, rb, wg, wu, wd, gf)


MOE_TR = 1024
MOE_RC = 128
MOE_CAP = 2 * MOE_TR + N_EXPERTS * MOE_RC
MOE_PAD = 16


def _moe_route_kernel(x_ref, g_ref, rw_ref, rb_ref, hn_ref, gate_ref, pos_ref, rankt_ref, cnt_ref):
    TR = x_ref.shape[0]
    hn = _rms(x_ref[...], g_ref[...]).astype(BF16)
    hn_ref[...] = hn
    gate = _top2_gates(jnp.dot(hn, rw_ref[...], preferred_element_type=F32) + rb_ref[...])
    gate_ref[...] = gate
    sel = jnp.where(gate.T[:MOE_PAD] > 0.0, 1.0, 0.0)
    triu = jnp.where(_iota((LANES, LANES), 0) <= _iota((LANES, LANES), 1), 1.0, 0.0).astype(BF16)
    carry = jnp.zeros((MOE_PAD, 1), F32)
    ranks = []
    for c in range(TR // LANES):
        blk = sel[:, c * LANES:(c + 1) * LANES]
        pref = jnp.dot(blk.astype(BF16), triu, preferred_element_type=F32)
        ranks.append(jnp.where(blk > 0.0, carry + pref - 1.0, -1.0))
        carry = carry + pref[:, LANES - 1:]
    rank_t = jnp.concatenate(ranks, axis=1)
    rankt_ref[...] = rank_t[:N_EXPERTS]
    padded = jnp.ceil(carry * (1.0 / MOE_RC)) * MOE_RC
    lower = jnp.where(_iota((MOE_PAD, MOE_PAD), 0) > _iota((MOE_PAD, MOE_PAD), 1), 1.0, 0.0).astype(BF16)
    offs = jnp.dot(lower, jnp.broadcast_to(padded, (MOE_PAD, LANES)).astype(BF16), preferred_element_type=F32)
    cnt_ref[0:N_EXPERTS, :] = jnp.broadcast_to(carry[:N_EXPERTS], (N_EXPERTS, LANES))
    cnt_ref[N_EXPERTS:, :] = offs[:N_EXPERTS]
    pos_t = jnp.where(rank_t >= 0.0, rank_t + offs[:, :1], -1.0)
    p1 = jnp.max(pos_t, axis=0, keepdims=True)
    p2 = jnp.max(jnp.where(pos_t == p1, -1.0, pos_t), axis=0, keepdims=True)
    rows = _iota((LANES, TR), 0)
    pos_ref[...] = jnp.where(rows == 0, p1, jnp.where(rows == 1, p2, -1.0)).T


def _moe_route(x, g, rw, rb):
    N = x.shape[0]
    nt = N // MOE_TR
    tok = lambda width: pl.BlockSpec((MOE_TR, width), lambda i: (i, 0))
    full = lambda a: pl.BlockSpec(a.shape, lambda i: (0,) * a.ndim)
    return pl.pallas_call(
        _moe_route_kernel,
        grid=(nt,),
        in_specs=[tok(D_MODEL), full(g), full(rw), full(rb)],
        out_specs=[tok(D_MODEL), tok(LANES), tok(LANES), pl.BlockSpec((N_EXPERTS, MOE_TR), lambda i: (0, i)),
                   pl.BlockSpec((None, 2 * N_EXPERTS, LANES), lambda i: (i, 0, 0))],
        out_shape=[jax.ShapeDtypeStruct((N, D_MODEL), BF16), jax.ShapeDtypeStruct((N, LANES), F32),
                   jax.ShapeDtypeStruct((N, LANES), F32), jax.ShapeDtypeStruct((N_EXPERTS, N), F32),
                   jax.ShapeDtypeStruct((nt, 2 * N_EXPERTS, LANES), F32)],
        compiler_params=_cparams(("parallel",)),
        name="moe_route",
    )(x, g, rw, rb)


def _moe_routed_kernel(meta_ref, x_ref, hn_ref, gate_ref, pos_ref, rankt_ref, wg_ref, wu_ref, wd_ref, gf_ref, o_ref,
                       xc_scr, y_scr, grow_scr):
    i = pl.program_id(0)
    e = pl.program_id(1)
    j = pl.program_id(2)
    TR, RC = MOE_TR, MOE_RC
    n_chunks = lax.div(meta_ref[i, e] + (RC - 1), RC)
    off = meta_ref[i, N_EXPERTS + e]

    @pl.when((e == 0) & (j == 0))
    def _():
        y_scr[...] = jnp.zeros_like(y_scr)
        grow_scr[...] = jnp.zeros_like(grow_scr)

    @pl.when(j == 0)
    def _():
        rank_row = rankt_ref[pl.ds(e, 1), :]
        gate = gate_ref[...]
        gate_e = jnp.where(_iota(gate.shape, 1) == e, gate, 0.0)

        def gather(c, carry):
            rows = pl.multiple_of(off + c * RC, RC)
            want = (c * RC + _iota((RC, 1), 0)).astype(F32)
            onehot = jnp.where(rank_row == want, 1.0, 0.0).astype(BF16)
            xc_scr[pl.ds(rows, RC), :] = jnp.dot(onehot, hn_ref[...], preferred_element_type=F32).astype(BF16)
            grow_scr[pl.ds(rows, RC), :] = _dot_split_rhs(onehot, gate_e)
            return carry

        lax.fori_loop(0, n_chunks, gather, 0)

    def expert(c, carry):
        rows = pl.multiple_of(off + c * RC, RC)
        xc = xc_scr[pl.ds(rows, RC), :]
        act = _silu(jnp.dot(xc, wg_ref[...], preferred_element_type=F32)) * jnp.dot(xc, wu_ref[...], preferred_element_type=F32)
        y_scr[pl.ds(rows, RC), :] += jnp.dot(act.astype(BF16), wd_ref[...], preferred_element_type=F32)
        return carry

    lax.fori_loop(0, n_chunks, expert, 0)

    @pl.when((e == pl.num_programs(1) - 1) & (j == pl.num_programs(2) - 1))
    def _():
        step = 256
        for r in range(MOE_CAP // step):
            sl = slice(r * step, (r + 1) * step)
            gr = jnp.sum(grow_scr[sl, :], axis=1, keepdims=True)
            xc_scr[sl, :] = (y_scr[sl, :] * gr).astype(BF16)
        lane = _iota((step, MOE_CAP), 1).astype(F32)
        for r in range(TR // step):
            sl = slice(r * step, (r + 1) * step)
            pos = pos_ref[sl, :]
            scatter = jnp.where(lane == pos[:, 0:1], 1.0, jnp.where(lane == pos[:, 1:2], 1.0, 0.0)).astype(BF16)
            y = jnp.dot(scatter, xc_scr[...], preferred_element_type=F32)
            o_ref[sl, :] = _rms(x_ref[sl, :] + y, gf_ref[...])


def _moe_routed(x, g, rw, rb, wg, wu, wd, gf, tf):
    N = x.shape[0]
    nt = N // MOE_TR
    hn, gate, pos, rank_t, meta = _moe_route(x, g, rw, rb)
    meta = meta[:, :, 0].astype(I32)
    tok = lambda width: pl.BlockSpec((MOE_TR, width), lambda i, e, j, m: (i, 0))
    full = lambda a: pl.BlockSpec(a.shape, lambda i, e, j, m: (0,) * a.ndim)
    grid_spec = pltpu.PrefetchScalarGridSpec(
        num_scalar_prefetch=1,
        grid=(nt, N_EXPERTS, D_FF_EXPERT // tf),
        in_specs=[tok(D_MODEL), tok(D_MODEL), tok(LANES), tok(LANES),
                  pl.BlockSpec((N_EXPERTS, MOE_TR), lambda i, e, j, m: (0, i)),
                  pl.BlockSpec((None, D_MODEL, tf), lambda i, e, j, m: (e, 0, j)),
                  pl.BlockSpec((None, D_MODEL, tf), lambda i, e, j, m: (e, 0, j)),
                  pl.BlockSpec((None, tf, D_MODEL), lambda i, e, j, m: (e, j, 0)), full(gf)],
        out_specs=tok(D_MODEL),
        scratch_shapes=[pltpu.VMEM((MOE_CAP, D_MODEL), BF16), pltpu.VMEM((MOE_CAP, D_MODEL), F32),
                        pltpu.VMEM((MOE_CAP, LANES), F32)],
    )
    return pl.pallas_call(
        _moe_routed_kernel,
        grid_spec=grid_spec,
        out_shape=jax.ShapeDtypeStruct((N, D_MODEL), F32),
        compiler_params=_cparams(("parallel", "arbitrary", "arbitrary")),
        name="moe_routed",
    )(meta, x, hn, gate, pos, rank_t, wg, wu, wd, gf)


def _outproj_kernel(x_ref, y_ref, wo_ref, o_ref):
    o_ref[...] = x_ref[...] + jnp.dot(y_ref[...], wo_ref[...], preferred_element_type=F32)


def _outproj(x, y, wo, tm):
    N = x.shape[0]
    tok = lambda width: pl.BlockSpec((tm, width), lambda i: (i, 0))
    return pl.pallas_call(
        _outproj_kernel,
        grid=(N // tm,),
        in_specs=[tok(D_MODEL), tok(y.shape[1]), pl.BlockSpec(wo.shape, lambda i: (0, 0))],
        out_specs=tok(D_MODEL),
        out_shape=jax.ShapeDtypeStruct((N, D_MODEL), F32),
        compiler_params=_cparams(("parallel",)),
        name="outproj",
    )(x, y, wo)


def _pad_router(router_w, router_b):
    rw = jnp.concatenate([router_w, jnp.zeros((D_MODEL, LANES - N_EXPERTS), router_w.dtype)], axis=1).astype(BF16)
    rb = jnp.concatenate([router_b.astype(F32), jnp.full((LANES - N_EXPERTS,), NEG_BIG, F32)]).reshape(1, LANES)
    return rw, rb


def _pad_w_in0(w_in_0):
    o = A_COLS + 3 * D_B + H_IDX * D_IDX
    wi = w_in_0[:, o:o + H_IDX]
    ki = w_in_0[:, o + H_IDX:o + H_IDX + D_IDX]
    pad = jnp.zeros((D_MODEL, LANES - D_IDX - H_IDX), w_in_0.dtype)
    return jnp.concatenate([w_in_0[:, :o], ki, wi, pad], axis=1).astype(BF16)


def _rwkv_params(w):
    row = lambda a: a.reshape(1, -1).astype(F32)
    zeros = jnp.zeros((DECAY_LORA, D_A), F32)
    return {
        "mu": row(w["a_mu"]), "w0": row(w["a_w0"]), "a0": row(w["a_a0"]),
        "w2": jnp.concatenate([w["a_w2"], zeros], axis=0).astype(BF16),
        "a2": jnp.concatenate([zeros, w["a_a2"]], axis=0).astype(BF16),
        "g2": w["a_g2"].astype(BF16),
        "kk": row(w["a_kk"]), "ka": row(w["a_ka"]), "rk": row(w["a_rk"]),
        "bd": _block_diag_ones(D_A), "lnw": row(w["a_ln_w"]), "lnb": row(w["a_ln_b"]),
    }


DSA_QB = 128


def _order_key(score):
    bits = lax.bitcast_convert_type(score, I32)
    key = jnp.where(bits < 0, bits ^ jnp.int32(0x7FFFFFFF), bits)
    return jnp.where(score == 0.0, 0, key)


def _fold_lanes(x):
    part = x[:, :LANES]
    for j in range(1, x.shape[1] // LANES):
        part = part + x[:, j * LANES:(j + 1) * LANES]
    return part


def _fold_rows(x, rows=64):
    rows = min(rows, x.shape[0])
    part = x[:rows]
    for j in range(1, x.shape[0] // rows):
        part = part + x[j * rows:(j + 1) * rows]
    return part


def _kth_largest_key(count, topk, shape):
    kf = float(topk)
    base = jnp.where(count(jnp.zeros(shape, I32)) >= kf, 0, INT_MIN).astype(I32)

    def bit_body(i, base):
        cand = base + lax.shift_left(jnp.int32(1), 30 - i)
        return jnp.where(count(cand) >= kf, cand, base)

    return lax.fori_loop(0, 31, bit_body, base)


def _dsa_prompt_kernel(topk, KB, KA, idx_bits, qt_ref, k_ref, vt_ref, qit_ref, wit_ref, ki_ref, o_ref,
                       key_scr, thr_scr, s_scr, p_scr, m_scr, l_scr, acc_scr):
    QB = DSA_QB
    q_pos0 = pl.program_id(1) * QB
    nkb = lax.div(q_pos0 + QB - 1, KB) + 1
    nka = lax.div(q_pos0 + QB - 1, KA) + 1
    key_row = _iota((KB, QB), 0)
    q_lane = _iota((KB, QB), 1)

    qit = qit_ref[...]
    q_cat = jnp.concatenate([qit[h * D_IDX:(h + 1) * D_IDX] for h in range(H_IDX)], axis=1)
    wit = wit_ref[...]
    w_cat = jnp.concatenate([wit[h:h + 1] for h in range(H_IDX)], axis=1)

    def score_body(kb, carry):
        off = pl.multiple_of(kb * KB, KB)
        res = jnp.dot(ki_ref[pl.ds(off, KB), :], q_cat, preferred_element_type=F32)
        weighted = jnp.maximum(res, 0.0) * w_cat
        admissible = (off + key_row) <= (q_pos0 + q_lane)
        key_scr[pl.ds(off, KB), :] = jnp.where(admissible, _order_key(_fold_lanes(weighted)), INT_MIN)
        return carry

    lax.fori_loop(0, nkb, score_body, 0)

    def count_where(pred):
        def body(kb, acc):
            off = pl.multiple_of(kb * KB, KB)
            return acc + _fold_rows(pred(key_scr[pl.ds(off, KB), :], off + key_row))
        acc = lax.fori_loop(0, nkb, body, jnp.zeros((min(64, KB), QB), F32))
        return jnp.sum(acc, axis=0, keepdims=True)

    kf = float(topk)
    count_ge = lambda cand: count_where(lambda blk, idx: jnp.where(blk >= cand, 1.0, 0.0))
    n0 = count_ge(jnp.zeros((1, QB), I32))
    start = (jnp.int32(0), jnp.where(n0 >= kf, 0, INT_MIN).astype(I32),
             jnp.where(n0 >= kf, n0, float(jnp.iinfo(jnp.int32).max)))

    BITS_PER_CHECK = 4

    def unsettled(state):
        i, _, n_base = state
        return (i < 31) & (jnp.max(jnp.abs(n_base - kf)) > 0.0)

    def bit_step(j, state):
        i, base, n_base = state
        bit = 30 - i
        cand = base + jnp.where(bit >= 0, lax.shift_left(jnp.int32(1), jnp.maximum(bit, 0)), 0)
        n = count_ge(cand)
        take = n >= kf
        return i + 1, jnp.where(take, cand, base), jnp.where(take, n, n_base)

    _, thr, n_ge = lax.while_loop(unsettled, lambda st: lax.fori_loop(0, BITS_PER_CHECK, bit_step, st), start)
    thr_scr[...] = jnp.full((8, QB), 2 ** 30, I32)

    @pl.when(jnp.max(jnp.where(n_ge > kf, 1.0, 0.0) * jnp.where(thr > INT_MIN, 1.0, 0.0)) > 0.0)
    def _():
        need = kf - count_where(lambda blk, idx: jnp.where(blk > thr, 1.0, 0.0))

        def bit_body(i, j):
            cand = j + lax.shift_left(jnp.int32(1), idx_bits - 1 - i)
            ties = count_where(lambda blk, idx: jnp.where(blk == thr, jnp.where(idx <= cand, 1.0, 0.0), 0.0))
            return jnp.where(ties < need, cand, j)
        j = lax.fori_loop(0, idx_bits, bit_body, jnp.full((1, QB), -1, I32))
        thr_scr[...] = jnp.broadcast_to(j + 1, (8, QB))

    idx_thr = thr_scr[0:1, :]
    tie_bias = jnp.where(thr == INT_MIN, NEG_BIG, 0.0)

    qt = qt_ref[...]
    feat = _iota((LANES, QB), 0)
    qtm = []
    for h in range(H_B):
        pair = qt[(h // 2) * LANES:(h // 2 + 1) * LANES]
        qtm.append(jnp.where((feat // HEAD_DIM) == (h % 2), pair, jnp.zeros_like(pair)))
    q_pairs = [jnp.concatenate([qtm[2 * j], qtm[2 * j + 1]], axis=1) for j in range(H_B // 2)]
    key_row_a = _iota((KA, QB), 0)

    def scores_into(slot, ka):
        hit = jnp.where(ka < nka, 0.0, NEG_BIG)
        off = pl.multiple_of(jnp.minimum(ka, nka - 1) * KA, KA)
        blk = key_scr[pl.ds(off, KA), :]
        bias = jnp.where(blk > thr, hit, jnp.where(
            blk == thr, jnp.where((off + key_row_a) <= idx_thr, tie_bias + hit, NEG_BIG), NEG_BIG))
        for j in range(H_B // 2):
            s2 = jnp.dot(k_ref[pl.ds(off, KA), j * LANES:(j + 1) * LANES], q_pairs[j], preferred_element_type=F32)
            s_scr[slot, 2 * j] = s2[:, :QB] + bias
            s_scr[slot, 2 * j + 1] = s2[:, QB:] + bias

    scores_into(0, 0)

    m_scr[...] = jnp.full(m_scr.shape, NEG_BIG, F32)
    l_scr[...] = jnp.zeros_like(l_scr)
    acc_scr[...] = jnp.zeros_like(acc_scr)
    CH = 64
    n_ch = KA // CH
    ones_rows = jnp.ones((16, KA), BF16)

    def attend_block(slot, ka):
        off = pl.multiple_of(jnp.minimum(ka, nka - 1) * KA, KA)
        for h in range(H_B):
            top = None
            for c in range(n_ch):
                piece = s_scr[slot, h, c * CH:(c + 1) * CH, :]
                top = piece if top is None else jnp.maximum(top, piece)
            m_old = m_scr[h:h + 1, :]
            m_new = jnp.maximum(m_old, jnp.max(top, axis=0, keepdims=True))
            alpha = jnp.exp2(m_old - m_new)
            for c in range(n_ch):
                rows = slice(c * CH, (c + 1) * CH)
                p_scr[h, rows, :] = jnp.exp2(s_scr[slot, h, rows, :] - m_new).astype(BF16)
            v_aug = jnp.concatenate([vt_ref[h * HEAD_DIM:(h + 1) * HEAD_DIM, pl.ds(off, KA)], ones_rows], axis=0)
            pv = jnp.dot(v_aug, p_scr[h], preferred_element_type=F32)
            m_scr[h:h + 1, :] = m_new
            l_scr[h:h + 1, :] = alpha * l_scr[h:h + 1, :] + pv[HEAD_DIM:HEAD_DIM + 1, :]
            acc_scr[h] = acc_scr[h] * alpha + pv[:HEAD_DIM]

    def attn_body(i, carry):
        scores_into(1, 2 * i + 1)
        attend_block(0, 2 * i)
        scores_into(0, 2 * i + 2)
        attend_block(1, 2 * i + 1)
        return carry

    lax.fori_loop(0, lax.div(nka + 1, 2), attn_body, 0)
    out_t = jnp.concatenate([acc_scr[h] / l_scr[h:h + 1, :] for h in range(H_B)], axis=0)
    o_ref[...] = out_t.T.astype(BF16)


def _dsa_prompt(qt, k, vt, qit, wit, ki):
    B, T, _ = k.shape
    topk = min(IDX_TOPK_MAX, T // 4)
    kb = min(512, T)
    ka = min(512, T)
    idx_bits = max(1, int(np.ceil(np.log2(T))))
    qcols = lambda rows: pl.BlockSpec((None, rows, DSA_QB), lambda b, i: (b, 0, i))
    whole = lambda r, c: pl.BlockSpec((None, r, c), lambda b, i: (b, 0, 0))
    return pl.pallas_call(
        functools.partial(_dsa_prompt_kernel, topk, kb, ka, idx_bits),
        grid=(B, T // DSA_QB),
        in_specs=[qcols(D_B), whole(T, D_B), whole(D_B, T), qcols(D_B), qcols(H_IDX), whole(T, D_IDX)],
        out_specs=pl.BlockSpec((None, DSA_QB, D_B), lambda b, i: (b, i, 0)),
        out_shape=jax.ShapeDtypeStruct((B, T, D_B), BF16),
        scratch_shapes=[pltpu.VMEM((T, DSA_QB), I32), pltpu.VMEM((8, DSA_QB), I32),
                        pltpu.VMEM((2, H_B, ka, DSA_QB), F32),
                        pltpu.VMEM((H_B, ka, DSA_QB), BF16), pltpu.VMEM((H_B, DSA_QB), F32),
                        pltpu.VMEM((H_B, DSA_QB), F32), pltpu.VMEM((H_B, HEAD_DIM, DSA_QB), F32)],
        compiler_params=_cparams(("parallel", "arbitrary")),
        name="dsa_prompt",
    )(qt, k, vt, qit, wit, ki)


def _tile(n, pref):
    return pref if n % pref == 0 else n


def kernel(x_prompt, x_sample, state_a_wkv, state_a_shift, cache_b_k, cache_b_v, cache_b_kidx, cache_c_k, cache_c_v, page_table, norm_mix, norm_ffn, norm_final, w_in_0, w_out_0, a_mu, a_w0, a_w2, a_a0, a_a2, a_g2, a_kk, a_ka, a_rk, a_ln_w, a_ln_b, ffn_wg, ffn_wu, ffn_wd, w_in_1, w_out_1, router_w, router_b, moe_wg, moe_wu, moe_wd):
    B, T, D = x_prompt.shape
    DB, S, _ = x_sample.shape
    assert S == 1 and D == D_MODEL
    past = page_table.shape[1] * PAGE_SIZE
    row = lambda a: a.reshape(1, -1).astype(F32)
    b16 = lambda a: a.astype(BF16)

    prm = _rwkv_params(dict(a_mu=a_mu, a_w0=a_w0, a_w2=a_w2, a_a0=a_a0, a_a2=a_a2, a_g2=a_g2, a_kk=a_kk, a_ka=a_ka,
                            a_rk=a_rk, a_ln_w=a_ln_w, a_ln_b=a_ln_b))
    w_in0 = _pad_w_in0(w_in_0)
    w_out0, w_in1, w_out1 = b16(w_out_0), b16(w_in_1), b16(w_out_1)
    f_wg, f_wu, f_wd = b16(ffn_wg), b16(ffn_wu), b16(ffn_wd)
    m_wg, m_wu, m_wd = b16(moe_wg), b16(moe_wu), b16(moe_wd)
    rw, rb = _pad_router(router_w, router_b)
    g_mix0, g_mix1 = row(norm_mix[0]), row(norm_mix[1])
    g_ffn0, g_ffn1, g_fin = row(norm_ffn[0]), row(norm_ffn[1]), row(norm_final)
    tf_ffn = D_FF // 2
    tf_moe = 896

    N = B * T
    cos_p, sin_p = _rope_tables(jnp.arange(T, dtype=I32))
    pa, kf, vf, kif, qt, kb, vt, qit, wit, kib = _inproj0(x_prompt, g_mix0, w_in0, cos_p, sin_p, _tile(T, 256), True)
    prep = _rwkv_prep(pa, jnp.zeros((B, A_COLS), F32), prm, _tile(T, 256), True)
    ya, p_a_wkv = _rwkv_scan(*prep, prm["lnw"], prm["lnb"], _tile(T, 128))
    yb = _dsa_prompt(qt, kb, vt, qit, wit, kib)
    h = _outproj_ffn(x_prompt.reshape(N, D), ya.reshape(N, D_A), yb.reshape(N, D_B), w_out0, g_ffn0, f_wg, f_wu, f_wd,
                     _tile(N, 512), tf_ffn)
    keep = min(C_WINDOW_MAX, T)
    q1, k1, v1, k1f, v1f, *streams = _inproj1(h.reshape(B, T, D), g_mix1, w_in1, cos_p, sin_p, _tile(T, 256), keep)
    n_dil = len(C_PATTERNS) - 1
    outs, lses = [], []
    for i, (window, dil) in enumerate(C_PATTERNS):
        qkv = (q1, k1, v1) if i == 0 else tuple(streams[which * n_dil + i - 1] for which in range(3))
        o, lse = _dilated_branch(*qkv, window, dil)
        outs.append(o.reshape(N // dil, dil * D_C))
        lses.append(lse.reshape(N // dil, dil * LANES))
    h = _merge_outproj(h, outs, lses, w_out1, _tile(N, 512))
    if N % MOE_TR == 0:
        y_prompt = _moe_routed(h, g_ffn1, rw, rb, m_wg, m_wu, m_wd, g_fin, tf_moe).reshape(B, T, D)
    else:
        y_prompt = _moe_dense(h, g_ffn1, rw, rb, m_wg, m_wu, m_wd, g_fin, N, tf_moe).reshape(B, T, D)
    rows = lambda a, n: jnp.moveaxis(a.reshape(B, n, HEAD_DIM, a.shape[-1]), -1, 1)
    prompt_state = (p_a_wkv, pa[:, -1], rows(kf, H_B), rows(vf, H_B), jnp.swapaxes(kif, 1, 2),
                    rows(k1f, H_C), rows(v1f, H_C))

    cos_s, sin_s = _rope_tables(jnp.full((DB,), past, I32))
    xs = x_sample.reshape(1, DB, D)
    pa, kf, vf, kif, q, qi, tail = _inproj0(xs, g_mix0, w_in0, cos_s, sin_s, DB, False)
    prep = _rwkv_prep(pa, state_a_shift.astype(F32), prm, DB, False)
    ya, s_a_wkv = _rwkv_step(state_a_wkv.astype(F32), *(a.reshape(DB, D_A) for a in prep), prm["lnw"], prm["lnb"])
    pad8 = lambda a: jnp.concatenate([a, jnp.zeros_like(a)], axis=1)
    qi16 = pad8(qi.reshape(DB, H_IDX, D_IDX))
    wi16 = pad8((tail[0, :, D_IDX:D_IDX + H_IDX] * IDX_SCALE).reshape(DB, H_IDX, 1))
    ki16 = jnp.broadcast_to(kif.reshape(DB, 1, D_IDX), (DB, 16, D_IDX))
    bias, bias_new = _dsa_sample_select(page_table, qi16, wi16, ki16, jnp.swapaxes(cache_b_kidx.astype(F32), 1, 2))
    heads = lambda a, n: a.astype(F32).reshape(DB, n, HEAD_DIM)
    yb = _dsa_sample_attend(page_table, heads(q, H_B), heads(kf, H_B), heads(vf, H_B), bias, bias_new,
                            cache_b_k.astype(F32), cache_b_v.astype(F32))
    hs = _outproj_ffn(x_sample.reshape(DB, D), ya, yb.reshape(DB, D_B).astype(BF16), w_out0, g_ffn0, f_wg, f_wu, f_wd,
                      DB, tf_ffn)
    q1, k1, v1, k1f, v1f = _inproj1(hs.reshape(1, DB, D), g_mix1, w_in1, cos_s, sin_s, DB)
    yc = _dilated_sample(heads(q1, H_C), heads(k1f, H_C), heads(v1f, H_C), cache_c_k.astype(F32), cache_c_v.astype(F32))
    hs = _outproj(hs, yc.reshape(DB, D_C).astype(BF16), w_out1, DB)
    y_sample = _moe_dense(hs, g_ffn1, rw, rb, m_wg, m_wu, m_wd, g_fin, DB, tf_moe).reshape(DB, 1, D)
    keep = min(C_WINDOW_MAX, cache_c_k.shape[1] + 1)
    s_c_k = jnp.concatenate([cache_c_k, k1f.reshape(DB, 1, H_C, HEAD_DIM)], axis=1)[:, -keep:]
    s_c_v = jnp.concatenate([cache_c_v, v1f.reshape(DB, 1, H_C, HEAD_DIM)], axis=1)[:, -keep:]
    sample_state = (s_a_wkv, pa[0], kf.reshape(DB, 1, H_B, HEAD_DIM), vf.reshape(DB, 1, H_B, HEAD_DIM),
                    kif.reshape(DB, 1, D_IDX), s_c_k, s_c_v)
    return (y_prompt, y_sample) + prompt_state + sample_state
```
